```python
import math
import jax, jax.numpy as jnp
from jax import lax
import numpy as np

D_MODEL = 1024
BATCH = 4
SEQ = 8192
DEPTH = 1

D_SSM = 1024
D_ATTN = 1024
D_MIX = D_SSM + D_ATTN

SSM_HEAD_DIM = 64
SSM_HEADS = D_SSM // SSM_HEAD_DIM
SSM_GROUPS = 2
SSM_HPG = SSM_HEADS // SSM_GROUPS
D_STATE = 128
D_CONV = 5
CHUNK = 128
CONV_CH = D_SSM + 2 * SSM_GROUPS * D_STATE

ATTN_HEAD_DIM = 64
ATTN_HEADS = D_ATTN // ATTN_HEAD_DIM
KV_HEADS = 4
Q_PER_KV = ATTN_HEADS // KV_HEADS
WINDOW = 128
BLK = 128
NUM_BUCKETS = 32
MAX_DISTANCE = 128
MAX_EXACT = 8

EPS = 1e-6
NEG_INF = -1e30

PROJ_SIZES = [
    D_SSM,
    CONV_CH,
    2 * SSM_HEADS,
    ATTN_HEADS * ATTN_HEAD_DIM,
    KV_HEADS * ATTN_HEAD_DIM,
    KV_HEADS * ATTN_HEAD_DIM,
    D_ATTN,
]
D_PROJ = int(sum(PROJ_SIZES))
PROJ_SPLITS = [int(s) for s in np.cumsum(PROJ_SIZES)[:-1]]

kernel_name = "hymba_ssd_swa_bidir_layer"


def rms_norm(x, g):
    xf = x.astype(jnp.float32)
    y = xf * lax.rsqrt(jnp.mean(xf * xf, axis=-1, keepdims=True) + EPS)
    return (y * g.astype(jnp.float32)).astype(x.dtype)


def ssd_chunked(xh, dt, A, Bm, Cm):
    b, S = xh.shape[0], xh.shape[1]
    c = S // CHUNK
    X = (xh.astype(jnp.float32) * dt[..., None]).reshape(b, c, CHUNK, SSM_GROUPS, SSM_HPG, SSM_HEAD_DIM)
    a = (dt * A).reshape(b, c, CHUNK, SSM_GROUPS, SSM_HPG).transpose(0, 3, 4, 1, 2)
    cs = jnp.cumsum(a, axis=-1)
    tril = jnp.tril(jnp.ones((CHUNK, CHUNK), dtype=bool))
    L = jnp.exp(jnp.where(tril, cs[..., :, None] - cs[..., None, :], -jnp.inf))
    Bc = Bm.astype(jnp.float32).reshape(b, c, CHUNK, SSM_GROUPS, D_STATE)
    Cc = Cm.astype(jnp.float32).reshape(b, c, CHUNK, SSM_GROUPS, D_STATE)
    CB = jnp.einsum('bclgn,bcsgn->bgcls', Cc, Bc)
    scores = CB[:, :, None] * L
    y_diag = jnp.einsum('bghcls,bcsghp->bclghp', scores, X)
    decay_states = jnp.exp(cs[..., -1:] - cs)
    states = jnp.einsum('bcsgn,bghcs,bcsghp->cbghpn', Bc, decay_states, X)
    chunk_decay = jnp.exp(cs[..., -1]).transpose(3, 0, 1, 2)

    def step(carry, inp):
        st, dec = inp
        return dec[..., None, None] * carry + st, carry

    init = jnp.zeros(states.shape[1:], jnp.float32)
    _, states_in = lax.scan(step, init, (states, chunk_decay))
    y_off = jnp.einsum('bclgn,cbghpn,bghcl->bclghp', Cc, states_in, jnp.exp(cs))
    return (y_diag + y_off).reshape(b, S, SSM_GROUPS, SSM_HPG, SSM_HEAD_DIM)


def t5_buckets(rel):
    half = NUM_BUCKETS // 2
    ret = (rel > 0).astype(jnp.int32) * half
    n = jnp.abs(rel)
    is_small = n < MAX_EXACT
    nf = jnp.maximum(n, 1).astype(jnp.float32)
    large = MAX_EXACT + (jnp.log(nf / MAX_EXACT) / math.log(MAX_DISTANCE / MAX_EXACT)
                         * (half - MAX_EXACT)).astype(jnp.int32)
    large = jnp.minimum(large, half - 1)
    return ret + jnp.where(is_small, n, large)


def windowed_gqa(q, k, v, rel_bias, sink):
    b, S = q.shape[0], q.shape[1]
    nb = S // BLK
    scale = ATTN_HEAD_DIM ** -0.5
    qi = jnp.arange(BLK)[:, None]
    t = jnp.arange(3 * BLK)[None, :]
    rel = t - BLK - qi
    in_window = jnp.abs(rel) <= WINDOW
    bias = rel_bias.astype(jnp.float32)[t5_buckets(rel)]
    bias = bias.transpose(2, 0, 1).reshape(KV_HEADS, Q_PER_KV, BLK, 3 * BLK)
    s = sink.astype(jnp.float32).reshape(KV_HEADS, Q_PER_KV)[None, :, :, None, None]

    pad = ((0, 0), (BLK, BLK), (0, 0), (0, 0))
    k_pad = jnp.pad(k, pad).astype(jnp.float32)
    v_pad = jnp.pad(v, pad).astype(jnp.float32)
    q_blocks = q.astype(jnp.float32).reshape(b, nb, BLK, KV_HEADS, Q_PER_KV, ATTN_HEAD_DIM)
    q_blocks = q_blocks.transpose(1, 0, 2, 3, 4, 5)

    def block(args):
        qb, idx = args
        kb = lax.dynamic_slice_in_dim(k_pad, idx * BLK, 3 * BLK, axis=1)
        vb = lax.dynamic_slice_in_dim(v_pad, idx * BLK, 3 * BLK, axis=1)
        key_pos = idx * BLK - BLK + t
        valid = in_window & (key_pos >= 0) & (key_pos < S)
        logits = jnp.einsum('bqkgd,btkd->bkgqt', qb, kb) * scale + bias
        logits = jnp.where(valid, logits, NEG_INF)
        m = jnp.maximum(jnp.max(logits, axis=-1, keepdims=True), s)
        p = jnp.exp(logits - m)
        denom = jnp.sum(p, axis=-1, keepdims=True) + jnp.exp(s - m)
        o = jnp.einsum('bkgqt,btkd->bqkgd', p, vb)
        return o / denom.transpose(0, 3, 1, 2, 4)

    out = lax.map(block, (q_blocks, jnp.arange(nb)))
    out = out.transpose(1, 0, 2, 3, 4, 5).reshape(b, S, ATTN_HEADS * ATTN_HEAD_DIM)
    return out.astype(q.dtype)


def hybrid_layer(x, norm_in_g, w_in, conv_w, conv_b, dt_bias, a_log, d_skip,
                 ssd_norm_g, rel_bias, sink, attn_norm_g, w_out):
    b, S, _ = x.shape
    h = rms_norm(x, norm_in_g)
    proj = jnp.einsum('bsd,de->bse', h, w_in)
    z, xbc, dt_raw, q, k, v, ga = jnp.split(proj, PROJ_SPLITS, axis=-1)

    xbc = lax.conv_general_dilated(
        xbc, conv_w.reshape(D_CONV, 1, CONV_CH), window_strides=(1,),
        padding=[((D_CONV - 1) // 2, (D_CONV - 1) // 2)],
        dimension_numbers=('NWC', 'WIO', 'NWC'), feature_group_count=CONV_CH)
    xbc = jax.nn.silu(xbc + conv_b)
    xs, Bm, Cm = jnp.split(xbc, [D_SSM, D_SSM + SSM_GROUPS * D_STATE], axis=-1)
    xh = xs.reshape(b, S, SSM_GROUPS, SSM_HPG, SSM_HEAD_DIM)
    Bm = Bm.reshape(b, S, SSM_GROUPS, D_STATE)
    Cm = Cm.reshape(b, S, SSM_GROUPS, D_STATE)
    dt = jax.nn.softplus(dt_raw.astype(jnp.float32).reshape(b, S, 2, SSM_HEADS)
                         + dt_bias.astype(jnp.float32))
    A = -jnp.exp(a_log.astype(jnp.float32))
    dt_f = dt[:, :, 0].reshape(b, S, SSM_GROUPS, SSM_HPG)
    dt_b = dt[:, :, 1].reshape(b, S, SSM_GROUPS, SSM_HPG)
    A_f = A[0].reshape(SSM_GROUPS, SSM_HPG)
    A_b = A[1].reshape(SSM_GROUPS, SSM_HPG)
    rev = lambda a: a[:, ::-1]
    y_fwd = ssd_chunked(xh, dt_f, A_f, Bm, Cm)
    y_bwd = rev(ssd_chunked(rev(xh), rev(dt_b), A_b, rev(Bm), rev(Cm)))
    y = y_fwd + y_bwd + d_skip.astype(jnp.float32).reshape(SSM_GROUPS, SSM_HPG)[..., None] * xh.astype(jnp.float32)
    y = y.reshape(b, S, SSM_GROUPS, D_SSM // SSM_GROUPS)
    zg = jax.nn.silu(z.astype(jnp.float32)).reshape(b, S, SSM_GROUPS, D_SSM // SSM_GROUPS)
    y_ssd = rms_norm(y * zg, ssd_norm_g.reshape(SSM_GROUPS, D_SSM // SSM_GROUPS))
    y_ssd = y_ssd.reshape(b, S, D_SSM).astype(x.dtype)

    qh = q.reshape(b, S, ATTN_HEADS, ATTN_HEAD_DIM)
    kh = k.reshape(b, S, KV_HEADS, ATTN_HEAD_DIM)
    vh = v.reshape(b, S, KV_HEADS, ATTN_HEAD_DIM)
    o = windowed_gqa(qh, kh, vh, rel_bias, sink)
    y_attn = rms_norm(o * jax.nn.silu(ga), attn_norm_g)

    cat = jnp.concatenate([y_ssd, y_attn.astype(x.dtype)], axis=-1)
    return x + jnp.einsum('bse,ed->bsd', cat, w_out)


def setup_inputs(seed: int = 0) -> dict:
    key = jax.random.key(seed)
    ks = jax.random.split(key, 16)
    f32 = jnp.float32
    x = jax.random.normal(ks[0], (BATCH, SEQ, D_MODEL), f32)
    norm_in_g = 1.0 + 0.02 * jax.random.normal(ks[1], (D_MODEL,), f32)
    w_in = jax.random.normal(ks[2], (D_MODEL, D_PROJ), f32) * D_MODEL ** -0.5
    conv_w = jax.random.normal(ks[3], (D_CONV, CONV_CH), f32) * D_CONV ** -0.5
    conv_b = 0.02 * jax.random.normal(ks[4], (CONV_CH,), f32)
    dt0 = jnp.exp(jax.random.uniform(ks[5], (2, SSM_HEADS), f32,
                                     math.log(1e-3), math.log(1e-1)))
    dt_bias = dt0 + jnp.log(-jnp.expm1(-dt0))
    a_log = jnp.log(jax.random.uniform(ks[6], (2, SSM_HEADS), f32, 1.0, 16.0))
    d_skip = 1.0 + 0.1 * jax.random.normal(ks[7], (SSM_HEADS,), f32)
    ssd_norm_g = 1.0 + 0.02 * jax.random.normal(ks[8], (D_SSM,), f32)
    rel_bias = 0.1 * jax.random.normal(ks[9], (NUM_BUCKETS, ATTN_HEADS), f32)
    sink = 0.5 * jax.random.normal(ks[10], (ATTN_HEADS,), f32)
    attn_norm_g = 1.0 + 0.02 * jax.random.normal(ks[11], (D_ATTN,), f32)
    w_out = jax.random.normal(ks[12], (DEPTH, D_MIX, D_MODEL), f32) * D_MIX ** -0.5
    final_norm_g = 1.0 + 0.02 * jax.random.normal(ks[13], (D_MODEL,), f32)
    return {"x": x, "norm_in_g": norm_in_g, "w_in": w_in, "conv_w": conv_w,
            "conv_b": conv_b, "dt_bias": dt_bias, "a_log": a_log, "d_skip": d_skip,
            "ssd_norm_g": ssd_norm_g, "rel_bias": rel_bias, "sink": sink,
            "attn_norm_g": attn_norm_g, "w_out": w_out, "final_norm_g": final_norm_g}


def reference(x, norm_in_g, w_in, conv_w, conv_b, dt_bias, a_log, d_skip,
              ssd_norm_g, rel_bias, sink, attn_norm_g, w_out, final_norm_g):
    h = x
    for layer in range(DEPTH):
        h = hybrid_layer(h, norm_in_g, w_in, conv_w, conv_b, dt_bias, a_log, d_skip,
                         ssd_norm_g, rel_bias, sink, attn_norm_g, w_out[layer])
    return rms_norm(h, final_norm_g)
```

```python
import math

import numpy as np
import jax
import jax.numpy as jnp
from jax import lax
from jax.experimental import pallas as pl
from jax.experimental.pallas import tpu as pltpu

F32 = jnp.float32
BF16 = jnp.bfloat16

D_MODEL = 1024
D_SSM = 1024
D_ATTN = 1024
D_MIX = D_SSM + D_ATTN
SSM_HEAD_DIM = 64
SSM_HEADS = D_SSM // SSM_HEAD_DIM
SSM_GROUPS = 2
GROUP_W = D_SSM // SSM_GROUPS
D_STATE = 128
D_CONV = 5
CHUNK = 128
CONV_CH = D_SSM + 2 * SSM_GROUPS * D_STATE
ATTN_HEAD_DIM = 64
ATTN_HEADS = D_ATTN // ATTN_HEAD_DIM
KV_HEADS = 4
WINDOW = 128
BLK = 128
NUM_BUCKETS = 32
MAX_DISTANCE = 128
MAX_EXACT = 8
EPS = 1e-6
NEG_INF = -1e30
SCALE = ATTN_HEAD_DIM ** -0.5

LANES = 128
HALO = 8
TM = 512
Q = 4
VMEM_LIMIT = 56 * 1024 * 1024

OFF_Z = 0
OFF_XBC = OFF_Z + D_SSM
OFF_Q = OFF_XBC + CONV_CH
OFF_KD = OFF_Q + D_ATTN
OFF_V = OFF_KD + 2 * KV_HEADS * ATTN_HEAD_DIM
OFF_GA = OFF_V + KV_HEADS * ATTN_HEAD_DIM
OFF_DT = OFF_GA + D_ATTN
W_COLS = OFF_DT + 2 * SSM_HEADS


def _silu(v):
    return v * jax.nn.sigmoid(v)


def _softplus(v):
    return jnp.maximum(v, 0.0) + jnp.log1p(jnp.exp(-jnp.abs(v)))


def _inproj_kernel(tiles_per_seq, x_ref, xp_ref, xn_ref, g_ref, w_ref, cw_ref, cb_ref, dtb_ref,
                   z_ref, xs_ref, bm_ref, cm_ref, dt_ref, q_ref, kd_ref, v_ref, ga_ref, pad_ref):
    j = lax.rem(pl.program_id(0), tiles_per_seq)
    g = g_ref[...]

    def norm(xv):
        ms = jnp.mean(xv * xv, axis=-1, keepdims=True)
        return (xv * lax.rsqrt(ms + EPS) * g).astype(BF16)

    def proj(hv, off, n):
        return jnp.dot(hv, w_ref[:, off:off + n], preferred_element_type=F32)

    h = norm(x_ref[...])
    hh = norm(jnp.concatenate([xp_ref[...], xn_ref[...]], axis=0))

    z_ref[...] = proj(h, OFF_Z, D_SSM).astype(BF16)
    q_ref[...] = (proj(h, OFF_Q, D_ATTN) * SCALE).astype(BF16)
    kd_ref[...] = proj(h, OFF_KD, 2 * KV_HEADS * ATTN_HEAD_DIM).astype(BF16)
    v_ref[...] = proj(h, OFF_V, KV_HEADS * ATTN_HEAD_DIM).astype(BF16)
    ga_ref[...] = proj(h, OFF_GA, D_ATTN).astype(BF16)
    dt_ref[...] = _softplus(proj(h, OFF_DT, 2 * SSM_HEADS) + dtb_ref[...])

    halo = proj(hh, OFF_XBC, CONV_CH)
    prev_ok = jnp.where(j > 0, 1.0, 0.0)
    next_ok = jnp.where(j < tiles_per_seq - 1, 1.0, 0.0)
    pad_ref[0:HALO, :] = halo[0:HALO] * prev_ok
    pad_ref[HALO:HALO + TM, :] = proj(h, OFF_XBC, CONV_CH)
    pad_ref[HALO + TM:2 * HALO + TM, :] = halo[HALO:2 * HALO] * next_ok

    base = HALO - (D_CONV - 1) // 2
    for cc in range(CONV_CH // LANES):
        sl = slice(cc * LANES, (cc + 1) * LANES)
        acc = cb_ref[:, sl] + cw_ref[0:1, sl] * pad_ref[base:base + TM, sl]
        for k in range(1, D_CONV):
            acc = acc + cw_ref[k:k + 1, sl] * pad_ref[base + k:base + k + TM, sl]
        y = _silu(acc).astype(BF16)
        if cc < D_SSM // LANES:
            xs_ref[:, sl] = y
        elif cc < (D_SSM + SSM_GROUPS * D_STATE) // LANES:
            o = cc * LANES - D_SSM
            bm_ref[:, o:o + LANES] = y
        else:
            o = cc * LANES - D_SSM - SSM_GROUPS * D_STATE
            cm_ref[:, o:o + LANES] = y


def _inproj(x2, norm_in_g, w_all, conv_w, conv_b, dt_bias, seq):
    n = x2.shape[0]
    tiles_per_seq = seq // TM
    nh = n // HALO
    row = lambda i: (i, 0)
    const = lambda i: (0, 0)
    out_cols = (D_SSM, D_SSM, SSM_GROUPS * D_STATE, SSM_GROUPS * D_STATE, 2 * SSM_HEADS,
                D_ATTN, 2 * KV_HEADS * ATTN_HEAD_DIM, KV_HEADS * ATTN_HEAD_DIM, D_ATTN)
    out_dt = (BF16, BF16, BF16, BF16, F32, BF16, BF16, BF16, BF16)
    return pl.pallas_call(
        lambda *refs: _inproj_kernel(tiles_per_seq, *refs),
        grid=(n // TM,),
        in_specs=[
            pl.BlockSpec((TM, D_MODEL), row),
            pl.BlockSpec((HALO, D_MODEL), lambda i: (jnp.maximum(i * (TM // HALO) - 1, 0), 0)),
            pl.BlockSpec((HALO, D_MODEL), lambda i: (jnp.minimum((i + 1) * (TM // HALO), nh - 1), 0)),
            pl.BlockSpec((1, D_MODEL), const),
            pl.BlockSpec((D_MODEL, W_COLS), const, pipeline_mode=pl.Buffered(1)),
            pl.BlockSpec((D_CONV, CONV_CH), const),
            pl.BlockSpec((1, CONV_CH), const),
            pl.BlockSpec((1, 2 * SSM_HEADS), const),
        ],
        out_specs=[pl.BlockSpec((TM, c), row) for c in out_cols],
        out_shape=[jax.ShapeDtypeStruct((n, c), d) for c, d in zip(out_cols, out_dt)],
        scratch_shapes=[pltpu.VMEM((TM + 2 * HALO, CONV_CH), F32)],
        compiler_params=pltpu.CompilerParams(dimension_semantics=("parallel",),
                                             vmem_limit_bytes=VMEM_LIMIT),
        name="inproj",
    )(x2, x2, x2, norm_in_g, w_all, conv_w, conv_b, dt_bias)


def _expand_heads(v, e):
    hi = v.astype(BF16)
    lo = (v - hi.astype(F32)).astype(BF16)
    return (jnp.dot(hi, e, preferred_element_type=F32) + jnp.dot(lo, e, preferred_element_type=F32))


def _cumsum_rows(tril, a):
    return jnp.dot(tril, a, precision=lax.Precision.HIGHEST, preferred_element_type=F32)


def _states_kernel(alog_ref, e_ref, tril_ref, xsf_ref, bf_ref, dtf_ref, xsb_ref, bb_ref, dtb_ref,
                   sf_ref, sb_ref, cf_ref, cb_ref):
    @pl.when(pl.program_id(1) == 0)
    def _():
        cf_ref[...] = jnp.zeros_like(cf_ref)
        cb_ref[...] = jnp.zeros_like(cb_ref)

    a_neg = -jnp.exp(alog_ref[...])
    e = e_ref[...]
    tril = tril_ref[...]

    def chunk(c, xs_ref, b_ref, dt, a_h, to_end, out_ref, carry_ref):
        rows = slice(c * CHUNK, (c + 1) * CHUNK)
        a = dt * a_h
        cs = _cumsum_rows(tril, a)
        tot = cs[CHUNK - 1:CHUNK, :]
        w = dt * jnp.exp(tot - cs if to_end else cs - a)
        wx = _expand_heads(jnp.concatenate([w, jnp.broadcast_to(jnp.exp(tot), (8, SSM_HEADS))], axis=0), e)
        xd = (xs_ref[rows, :].astype(F32) * wx[0:CHUNK]).astype(BF16)
        dec = wx[CHUNK:CHUNK + 1, :]
        for g in range(SSM_GROUPS):
            gr = slice(g * D_STATE, (g + 1) * D_STATE)
            gc = slice(g * GROUP_W, (g + 1) * GROUP_W)
            st = lax.dot_general(b_ref[rows, gr], xd[:, gc], (((0,), (0,)), ((), ())),
                                 preferred_element_type=F32)
            prev = carry_ref[gr, :]
            out_ref[c, gr, :] = prev.astype(BF16)
            carry_ref[gr, :] = prev * dec[:, gc] + st

    for c in range(Q):
        rows = slice(c * CHUNK, (c + 1) * CHUNK)
        chunk(c, xsf_ref, bf_ref, dtf_ref[rows, 0:SSM_HEADS], a_neg[:, 0:SSM_HEADS], True, sf_ref, cf_ref)
    for c in reversed(range(Q)):
        rows = slice(c * CHUNK, (c + 1) * CHUNK)
        chunk(c, xsb_ref, bb_ref, dtb_ref[rows, SSM_HEADS:2 * SSM_HEADS],
              a_neg[:, SSM_HEADS:2 * SSM_HEADS], False, sb_ref, cb_ref)


def _states(a_log, e, tril, xs, bm, dt, batch, seq):
    nstep = seq // (Q * CHUNK)
    nchunk = batch * seq // CHUNK
    fwd = lambda b, i: (b * nstep + i, 0)
    bwd = lambda b, i: (b * nstep + nstep - 1 - i, 0)
    const = lambda b, i: (0, 0)
    rows = Q * CHUNK
    st_shape = jax.ShapeDtypeStruct((nchunk, SSM_GROUPS * D_STATE, GROUP_W), BF16)
    return pl.pallas_call(
        _states_kernel,
        grid=(batch, nstep),
        in_specs=[
            pl.BlockSpec((1, 2 * SSM_HEADS), const),
            pl.BlockSpec((SSM_HEADS, D_SSM), const),
            pl.BlockSpec((CHUNK, CHUNK), const),
            pl.BlockSpec((rows, D_SSM), fwd),
            pl.BlockSpec((rows, SSM_GROUPS * D_STATE), fwd),
            pl.BlockSpec((rows, 2 * SSM_HEADS), fwd),
            pl.BlockSpec((rows, D_SSM), bwd),
            pl.BlockSpec((rows, SSM_GROUPS * D_STATE), bwd),
            pl.BlockSpec((rows, 2 * SSM_HEADS), bwd),
        ],
        out_specs=[
            pl.BlockSpec((Q, SSM_GROUPS * D_STATE, GROUP_W), lambda b, i: (b * nstep + i, 0, 0)),
            pl.BlockSpec((Q, SSM_GROUPS * D_STATE, GROUP_W), lambda b, i: (b * nstep + nstep - 1 - i, 0, 0)),
        ],
        out_shape=[st_shape, st_shape],
        scratch_shapes=[pltpu.VMEM((SSM_GROUPS * D_STATE, GROUP_W), F32),
                        pltpu.VMEM((SSM_GROUPS * D_STATE, GROUP_W), F32)],
        compiler_params=pltpu.CompilerParams(dimension_semantics=("arbitrary", "arbitrary"),
                                             vmem_limit_bytes=VMEM_LIMIT),
        name="ssd_states",
    )(a_log, e, tril, xs, bm, dt, xs, bm, dt)


def _ssd_out_kernel(alog_ref, dskip_ref, gn_ref, e_ref, tril_ref, xs_ref, bm_ref, cm_ref, dt_ref,
                    z_ref, sf_ref, sb_ref, y_ref):
    a_neg = -jnp.exp(alog_ref[...])
    e = e_ref[...]
    tril = tril_ref[...]
    li = lax.broadcasted_iota(jnp.int32, (CHUNK, CHUNK), 0)
    si = lax.broadcasted_iota(jnp.int32, (CHUNK, CHUNK), 1)
    lower = li >= si
    strict_lower = li > si
    strict_upper = li < si
    lane = lax.broadcasted_iota(jnp.int32, (CHUNK, LANES), 1)
    left = lane < SSM_HEAD_DIM
    dir_lane = lax.broadcasted_iota(jnp.int32, (CHUNK, 2 * SSM_HEADS), 1) < SSM_HEADS

    for c in range(Q):
        rows = slice(c * CHUNK, (c + 1) * CHUNK)
        dt = dt_ref[rows, :]
        a = dt * a_neg
        cs = _cumsum_rows(tril, a)
        tot = cs[CHUNK - 1:CHUNK, :]
        colq = jnp.where(dir_lane, cs, tot - cs + a)
        rowq = colq.T
        dtr = dt.T
        ecol = jnp.exp(colq)
        ef = _expand_heads(ecol[:, 0:SSM_HEADS], e)
        eb = _expand_heads(ecol[:, SSM_HEADS:2 * SSM_HEADS], e)

        for g in range(SSM_GROUPS):
            gr = slice(g * D_STATE, (g + 1) * D_STATE)
            gc = slice(g * GROUP_W, (g + 1) * GROUP_W)
            cmg = cm_ref[rows, gr]
            cb = lax.dot_general(cmg, bm_ref[rows, gr], (((1,), (1,)), ((), ())),
                                 preferred_element_type=F32)
            y_off = (jnp.dot(cmg, sf_ref[c, gr, :], preferred_element_type=F32) * ef[:, gc]
                     + jnp.dot(cmg, sb_ref[c, gr, :], preferred_element_type=F32) * eb[:, gc])
            xg = xs_ref[rows, gc]
            yg = y_off + dskip_ref[:, gc] * xg.astype(F32)
            pairs = []
            for pp in range(GROUP_W // LANES):
                scores = []
                for hh in range(2):
                    h = g * (SSM_HEADS // SSM_GROUPS) + pp * 2 + hh
                    hb = SSM_HEADS + h
                    arg = jnp.where(lower, colq[:, h:h + 1] - rowq[h:h + 1, :],
                                    colq[:, hb:hb + 1] - rowq[hb:hb + 1, :])
                    wrow = jnp.where(strict_lower, dtr[h:h + 1, :],
                                     jnp.where(strict_upper, dtr[hb:hb + 1, :],
                                               dtr[h:h + 1, :] + dtr[hb:hb + 1, :]))
                    scores.append((cb * jnp.exp(arg) * wrow).astype(BF16))
                xt = xg[:, pp * LANES:(pp + 1) * LANES]
                zero = jnp.zeros_like(xt)
                rhs = jnp.concatenate([jnp.where(left, xt, zero), jnp.where(left, zero, xt)], axis=0)
                pairs.append(jnp.dot(jnp.concatenate(scores, axis=1), rhs, preferred_element_type=F32))
            yg = yg + jnp.concatenate(pairs, axis=1)
            vg = yg * _silu(z_ref[rows, gc].astype(F32))
            ms = jnp.mean(vg * vg, axis=-1, keepdims=True)
            y_ref[rows, gc] = (vg * lax.rsqrt(ms + EPS) * gn_ref[:, gc]).astype(BF16)


def _ssd_out(a_log, dskip_x, ssd_norm_g, e, tril, xs, bm, cm, dt, z, sf, sb):
    n = xs.shape[0]
    rows = Q * CHUNK
    row = lambda i: (i, 0)
    const = lambda i: (0, 0)
    return pl.pallas_call(
        _ssd_out_kernel,
        grid=(n // rows,),
        in_specs=[
            pl.BlockSpec((1, 2 * SSM_HEADS), const),
            pl.BlockSpec((1, D_SSM), const),
            pl.BlockSpec((1, D_SSM), const),
            pl.BlockSpec((SSM_HEADS, D_SSM), const),
            pl.BlockSpec((CHUNK, CHUNK), const),
            pl.BlockSpec((rows, D_SSM), row),
            pl.BlockSpec((rows, SSM_GROUPS * D_STATE), row),
            pl.BlockSpec((rows, SSM_GROUPS * D_STATE), row),
            pl.BlockSpec((rows, 2 * SSM_HEADS), row),
            pl.BlockSpec((rows, D_SSM), row),
            pl.BlockSpec((Q, SSM_GROUPS * D_STATE, GROUP_W), lambda i: (i, 0, 0)),
            pl.BlockSpec((Q, SSM_GROUPS * D_STATE, GROUP_W), lambda i: (i, 0, 0)),
        ],
        out_specs=pl.BlockSpec((rows, D_SSM), row),
        out_shape=jax.ShapeDtypeStruct((n, D_SSM), BF16),
        compiler_params=pltpu.CompilerParams(dimension_semantics=("parallel",),
                                             vmem_limit_bytes=VMEM_LIMIT),
        name="ssd_out",
    )(a_log, dskip_x, ssd_norm_g, e, tril, xs, bm, cm, dt, z, sf, sb)


def _bucket_table_t():
    rel = jnp.arange(3 * BLK)[:, None] - BLK - jnp.arange(BLK)[None, :]
    half = NUM_BUCKETS // 2
    ret = (rel > 0).astype(jnp.int32) * half
    n = jnp.abs(rel)
    nf = jnp.maximum(n, 1).astype(F32)
    large = MAX_EXACT + (jnp.log(nf / MAX_EXACT) / math.log(MAX_DISTANCE / MAX_EXACT)
                         * (half - MAX_EXACT)).astype(jnp.int32)
    large = jnp.minimum(large, half - 1)
    return (ret + jnp.where(n < MAX_EXACT, n, large)).astype(jnp.int32)


def _attn_kernel(nblk, rb_ref, bucket_ref, sink_ref, gn_ref, q_ref, kp_ref, kc_ref, kn_ref,
                 vp_ref, vc_ref, vn_ref, ga_ref, y_ref, tab_ref, ot_ref):
    i = pl.program_id(1)

    @pl.when((pl.program_id(0) == 0) & (i == 0))
    def _build_bias():
        bk = bucket_ref[...]
        t = lax.broadcasted_iota(jnp.int32, (3 * BLK, BLK), 0)
        qq = lax.broadcasted_iota(jnp.int32, (3 * BLK, BLK), 1)
        in_window = jnp.abs(t - BLK - qq) <= WINDOW

        def per_head(h, carry):
            acc = jnp.zeros((3 * BLK, BLK), F32)
            for b in range(NUM_BUCKETS):
                acc = jnp.where(bk == b, rb_ref[b, h], acc)
            tab_ref[h] = jnp.where(in_window, acc, NEG_INF)
            return carry

        lax.fori_loop(0, ATTN_HEADS, per_head, 0)

    kcat = jnp.concatenate([kp_ref[...], kc_ref[...], kn_ref[...]], axis=0)
    vcat = jnp.concatenate([vp_ref[...], vc_ref[...], vn_ref[...]], axis=0)
    trow = lax.broadcasted_iota(jnp.int32, (3 * BLK, 2 * BLK), 0)
    neg_prev = jnp.where(i == 0, NEG_INF, 0.0)
    neg_next = jnp.where(i == nblk - 1, NEG_INF, 0.0)
    edge = jnp.where(trow < BLK, neg_prev, jnp.where(trow >= 2 * BLK, neg_next, 0.0))
    left = lax.broadcasted_iota(jnp.int32, (BLK, LANES), 1) < ATTN_HEAD_DIM

    for kv in range(KV_HEADS):
        kd = kcat[:, kv * LANES:(kv + 1) * LANES]
        vt = vcat[:, (kv // 2) * LANES:(kv // 2 + 1) * LANES]
        for gp in range(2):
            p = kv * 2 + gp
            qt = q_ref[:, p * LANES:(p + 1) * LANES]
            zero = jnp.zeros_like(qt)
            rhs_t = jnp.concatenate([jnp.where(left, qt, zero), jnp.where(left, zero, qt)], axis=0)
            lg = lax.dot_general(kd, rhs_t, (((1,), (1,)), ((), ())), preferred_element_type=F32)
            lg = lg + jnp.concatenate([tab_ref[2 * p], tab_ref[2 * p + 1]], axis=1) + edge
            s = sink_ref[p:p + 1, :]
            m = jnp.maximum(jnp.max(lg, axis=0, keepdims=True), s)
            pe = jnp.exp(lg - m)
            den = jnp.sum(pe, axis=0, keepdims=True) + jnp.exp(s - m)
            o2 = lax.dot_general(vt, pe.astype(BF16), (((0,), (0,)), ((), ())),
                                 preferred_element_type=F32)
            r0 = (kv % 2) * ATTN_HEAD_DIM
            o = o2[r0:r0 + ATTN_HEAD_DIM, :] / den
            ot_ref[2 * p * ATTN_HEAD_DIM:(2 * p + 1) * ATTN_HEAD_DIM, :] = o[:, 0:BLK]
            ot_ref[(2 * p + 1) * ATTN_HEAD_DIM:(2 * p + 2) * ATTN_HEAD_DIM, :] = o[:, BLK:2 * BLK]

    o = ot_ref[...].T
    v = o * _silu(ga_ref[...].astype(F32))
    ms = jnp.mean(v * v, axis=-1, keepdims=True)
    y_ref[...] = (v * lax.rsqrt(ms + EPS) * gn_ref[...]).astype(BF16)


def _attn(rel_bias, bucket_t, sink_x, attn_norm_g, q, kd, v, ga, batch, seq):
    n = q.shape[0]
    nblk = seq // BLK
    cur = lambda b, i: (b * nblk + i, 0)
    prev = lambda b, i: (b * nblk + jnp.maximum(i - 1, 0), 0)
    nxt = lambda b, i: (b * nblk + jnp.minimum(i + 1, nblk - 1), 0)
    const = lambda b, i: (0, 0)
    kw = 2 * KV_HEADS * ATTN_HEAD_DIM
    vw = KV_HEADS * ATTN_HEAD_DIM
    return pl.pallas_call(
        lambda *refs: _attn_kernel(nblk, *refs),
        grid=(batch, nblk),
        in_specs=[
            pl.BlockSpec(memory_space=pltpu.SMEM),
            pl.BlockSpec((3 * BLK, BLK), const),
            pl.BlockSpec((ATTN_HEADS // 2, 2 * BLK), const),
            pl.BlockSpec((1, D_ATTN), const),
            pl.BlockSpec((BLK, D_ATTN), cur),
            pl.BlockSpec((BLK, kw), prev), pl.BlockSpec((BLK, kw), cur), pl.BlockSpec((BLK, kw), nxt),
            pl.BlockSpec((BLK, vw), prev), pl.BlockSpec((BLK, vw), cur), pl.BlockSpec((BLK, vw), nxt),
            pl.BlockSpec((BLK, D_ATTN), cur),
        ],
        out_specs=pl.BlockSpec((BLK, D_ATTN), cur),
        out_shape=jax.ShapeDtypeStruct((n, D_ATTN), BF16),
        scratch_shapes=[pltpu.VMEM((ATTN_HEADS, 3 * BLK, BLK), F32),
                        pltpu.VMEM((D_ATTN, BLK), F32)],
        compiler_params=pltpu.CompilerParams(dimension_semantics=("arbitrary", "arbitrary"),
                                             vmem_limit_bytes=VMEM_LIMIT),
        name="attn",
    )(rel_bias, bucket_t, sink_x, attn_norm_g, q, kd, kd, kd, v, v, v, ga)


def _outproj_kernel(x_ref, ys_ref, ya_ref, w_ref, g_ref, o_ref):
    acc = (jnp.dot(ys_ref[...], w_ref[0:D_SSM, :], preferred_element_type=F32)
           + jnp.dot(ya_ref[...], w_ref[D_SSM:D_MIX, :], preferred_element_type=F32))
    hres = x_ref[...] + acc
    ms = jnp.mean(hres * hres, axis=-1, keepdims=True)
    o_ref[...] = hres * lax.rsqrt(ms + EPS) * g_ref[...]


def _outproj(x2, y_ssd, y_attn, w_out, final_norm_g):
    n = x2.shape[0]
    row = lambda i: (i, 0)
    const = lambda i: (0, 0)
    return pl.pallas_call(
        _outproj_kernel,
        grid=(n // TM,),
        in_specs=[
            pl.BlockSpec((TM, D_MODEL), row),
            pl.BlockSpec((TM, D_SSM), row),
            pl.BlockSpec((TM, D_ATTN), row),
            pl.BlockSpec((D_MIX, D_MODEL), const, pipeline_mode=pl.Buffered(1)),
            pl.BlockSpec((1, D_MODEL), const),
        ],
        out_specs=pl.BlockSpec((TM, D_MODEL), row),
        out_shape=jax.ShapeDtypeStruct((n, D_MODEL), F32),
        compiler_params=pltpu.CompilerParams(dimension_semantics=("parallel",),
                                             vmem_limit_bytes=VMEM_LIMIT),
        name="outproj",
    )(x2, y_ssd, y_attn, w_out, final_norm_g)


def _arrange_w_in(w_in):
    sizes = [D_SSM, CONV_CH, 2 * SSM_HEADS, D_ATTN, KV_HEADS * ATTN_HEAD_DIM,
             KV_HEADS * ATTN_HEAD_DIM, D_ATTN]
    wz, wxbc, wdt, wq, wk, wv, wga = jnp.split(w_in, [int(s) for s in np.cumsum(sizes)[:-1]], axis=1)
    wk = wk.reshape(D_MODEL, KV_HEADS, 1, ATTN_HEAD_DIM)
    wkd = jnp.broadcast_to(wk, (D_MODEL, KV_HEADS, 2, ATTN_HEAD_DIM)).reshape(D_MODEL, -1)
    return jnp.concatenate([wz, wxbc, wq, wkd, wv, wga, wdt], axis=1).astype(BF16)


def kernel(x, norm_in_g, w_in, conv_w, conv_b, dt_bias, a_log, d_skip, ssd_norm_g, rel_bias, sink,
           attn_norm_g, w_out, final_norm_g):
    batch, seq, _ = x.shape
    assert w_out.shape[0] == 1 and seq % (Q * CHUNK) == 0 and seq % TM == 0
    x2 = x.reshape(batch * seq, D_MODEL)

    w_all = _arrange_w_in(w_in)
    e = jnp.asarray(np.repeat(np.eye(SSM_HEADS, dtype=np.float32), SSM_HEAD_DIM, axis=1), BF16)
    tril = jnp.asarray(np.tril(np.ones((CHUNK, CHUNK), np.float32)))
    a_log2 = a_log.reshape(1, 2 * SSM_HEADS).astype(F32)
    dskip_x = jnp.repeat(d_skip.astype(F32), SSM_HEAD_DIM).reshape(1, D_SSM)
    sink_x = jnp.repeat(sink.astype(F32), BLK).reshape(ATTN_HEADS // 2, 2 * BLK)

    z, xs, bm, cm, dt, q, kd, v, ga = _inproj(
        x2, norm_in_g.reshape(1, D_MODEL), w_all, conv_w, conv_b.reshape(1, CONV_CH),
        dt_bias.reshape(1, 2 * SSM_HEADS).astype(F32), seq)
    sf, sb = _states(a_log2, e, tril, xs, bm, dt, batch, seq)
    y_ssd = _ssd_out(a_log2, dskip_x, ssd_norm_g.reshape(1, D_SSM), e, tril, xs, bm, cm, dt, z, sf, sb)
    y_attn = _attn(rel_bias.astype(F32), _bucket_table_t(), sink_x,
                   attn_norm_g.reshape(1, D_ATTN), q, kd, v, ga, batch, seq)
    out = _outproj(x2, y_ssd, y_attn, w_out[0].astype(BF16), final_norm_g.reshape(1, D_MODEL))
    return out.reshape(batch, seq, D_MODEL)
```

```python
import math

import numpy as np
import jax
import jax.numpy as jnp
from jax import lax
from jax.experimental import pallas as pl
from jax.experimental.pallas import tpu as pltpu

F32 = jnp.float32
BF16 = jnp.bfloat16

D_MODEL = 1024
D_SSM = 1024
D_ATTN = 1024
D_MIX = D_SSM + D_ATTN
SSM_HEAD_DIM = 64
SSM_HEADS = D_SSM // SSM_HEAD_DIM
SSM_GROUPS = 2
GROUP_W = D_SSM // SSM_GROUPS
D_STATE = 128
D_CONV = 5
CHUNK = 128
CONV_CH = D_SSM + 2 * SSM_GROUPS * D_STATE
ATTN_HEAD_DIM = 64
ATTN_HEADS = D_ATTN // ATTN_HEAD_DIM
KV_HEADS = 4
WINDOW = 128
BLK = 128
NUM_BUCKETS = 32
MAX_DISTANCE = 128
MAX_EXACT = 8
EPS = 1e-6
NEG_INF = -1e30
SCALE = ATTN_HEAD_DIM ** -0.5
LOG2E = math.log2(math.e)

LANES = 128
HALO = 8
TM = 512
Q = 4
VMEM_LIMIT = 56 * 1024 * 1024

OFF_Z = 0
OFF_XBC = OFF_Z + D_SSM
OFF_Q = OFF_XBC + CONV_CH
OFF_KD = OFF_Q + D_ATTN
OFF_V = OFF_KD + 2 * KV_HEADS * ATTN_HEAD_DIM
OFF_GA = OFF_V + KV_HEADS * ATTN_HEAD_DIM
OFF_DT = OFF_GA + D_ATTN
W_COLS = OFF_DT + LANES
assert TM == Q * CHUNK and Q * 2 * SSM_HEADS == LANES


def _silu(v):
    return v * jax.nn.sigmoid(v)


def _softplus(v):
    return jnp.maximum(v, 0.0) + jnp.log1p(jnp.exp(-jnp.abs(v)))


def _inproj_kernel(tiles_per_seq, x_ref, xp_ref, xn_ref, g_ref, w_ref, cw_ref, cb_ref, dtb_ref,
                   z_ref, xs_ref, bm_ref, cm_ref, dt_ref, q_ref, kd_ref, v_ref, ga_ref, pad_ref):
    j = lax.rem(pl.program_id(0), tiles_per_seq)
    g = g_ref[...]

    def norm(xv):
        ms = jnp.mean(xv * xv, axis=-1, keepdims=True)
        return (xv * lax.rsqrt(ms + EPS) * g).astype(BF16)

    def proj(hv, off, n):
        return jnp.dot(hv, w_ref[:, off:off + n], preferred_element_type=F32)

    h = norm(x_ref[...])
    hh = norm(jnp.concatenate([xp_ref[...], xn_ref[...]], axis=0))

    z_ref[...] = proj(h, OFF_Z, D_SSM).astype(BF16)
    q_ref[...] = (proj(h, OFF_Q, D_ATTN) * (SCALE * LOG2E)).astype(BF16)
    kd_ref[...] = proj(h, OFF_KD, 2 * KV_HEADS * ATTN_HEAD_DIM).astype(BF16)
    v_ref[...] = proj(h, OFF_V, KV_HEADS * ATTN_HEAD_DIM).astype(BF16)
    ga_ref[...] = proj(h, OFF_GA, D_ATTN).astype(BF16)
    dt_lane = lax.broadcasted_iota(jnp.int32, (TM, LANES), 1) < 2 * SSM_HEADS
    dt = jnp.where(dt_lane, _softplus(proj(h, OFF_DT, LANES) + dtb_ref[...]), 0.0)
    dtc = dt[0:CHUNK]
    for c in range(1, Q):
        dtc = dtc + pltpu.roll(dt[c * CHUNK:(c + 1) * CHUNK], 2 * SSM_HEADS * c, axis=1)
    dt_ref[...] = dtc

    halo = proj(hh, OFF_XBC, CONV_CH)
    prev_ok = jnp.where(j > 0, 1.0, 0.0)
    next_ok = jnp.where(j < tiles_per_seq - 1, 1.0, 0.0)
    pad_ref[0:HALO, :] = halo[0:HALO] * prev_ok
    pad_ref[HALO:HALO + TM, :] = proj(h, OFF_XBC, CONV_CH)
    pad_ref[HALO + TM:2 * HALO + TM, :] = halo[HALO:2 * HALO] * next_ok

    base = HALO - (D_CONV - 1) // 2
    for cc in range(CONV_CH // LANES):
        sl = slice(cc * LANES, (cc + 1) * LANES)
        acc = cb_ref[:, sl] + cw_ref[0:1, sl] * pad_ref[base:base + TM, sl]
        for k in range(1, D_CONV):
            acc = acc + cw_ref[k:k + 1, sl] * pad_ref[base + k:base + k + TM, sl]
        y = _silu(acc).astype(BF16)
        if cc < D_SSM // LANES:
            xs_ref[:, sl] = y
        elif cc < (D_SSM + SSM_GROUPS * D_STATE) // LANES:
            o = cc * LANES - D_SSM
            bm_ref[:, o:o + LANES] = y
        else:
            o = cc * LANES - D_SSM - SSM_GROUPS * D_STATE
            cm_ref[:, o:o + LANES] = y


def _inproj(x2, norm_in_g, w_all, conv_w, conv_b, dt_bias, seq):
    n = x2.shape[0]
    tiles_per_seq = seq // TM
    nh = n // HALO
    row = lambda i: (i, 0)
    const = lambda i: (0, 0)
    out_cols = (D_SSM, D_SSM, SSM_GROUPS * D_STATE, SSM_GROUPS * D_STATE,
                D_ATTN, 2 * KV_HEADS * ATTN_HEAD_DIM, KV_HEADS * ATTN_HEAD_DIM, D_ATTN)
    out_specs = [pl.BlockSpec((TM, c), row) for c in out_cols]
    out_shape = [jax.ShapeDtypeStruct((n, c), BF16) for c in out_cols]
    out_specs.insert(4, pl.BlockSpec((CHUNK, LANES), lambda i: (0, i)))
    out_shape.insert(4, jax.ShapeDtypeStruct((CHUNK, n // TM * LANES), F32))
    return pl.pallas_call(
        lambda *refs: _inproj_kernel(tiles_per_seq, *refs),
        grid=(n // TM,),
        in_specs=[
            pl.BlockSpec((TM, D_MODEL), row),
            pl.BlockSpec((HALO, D_MODEL), lambda i: (jnp.maximum(i * (TM // HALO) - 1, 0), 0)),
            pl.BlockSpec((HALO, D_MODEL), lambda i: (jnp.minimum((i + 1) * (TM // HALO), nh - 1), 0)),
            pl.BlockSpec((1, D_MODEL), const),
            pl.BlockSpec((D_MODEL, W_COLS), const, pipeline_mode=pl.Buffered(1)),
            pl.BlockSpec((D_CONV, CONV_CH), const),
            pl.BlockSpec((1, CONV_CH), const),
            pl.BlockSpec((1, LANES), const),
        ],
        out_specs=out_specs,
        out_shape=out_shape,
        scratch_shapes=[pltpu.VMEM((TM + 2 * HALO, CONV_CH), F32)],
        compiler_params=pltpu.CompilerParams(dimension_semantics=("parallel",),
                                             vmem_limit_bytes=VMEM_LIMIT),
        name="inproj",
    )(x2, x2, x2, norm_in_g, w_all, conv_w, conv_b, dt_bias)


def _split_terms(v, n):
    terms, r = [], v
    for _ in range(n):
        t = r.astype(BF16)
        terms.append(t)
        r = r - t.astype(F32)
    return terms


def _chunk_cumsum(tril, a):
    t0, t1, t2 = _split_terms(a, 3)
    return (jnp.dot(tril, t0, preferred_element_type=F32) + jnp.dot(tril, t1, preferred_element_type=F32)
            + jnp.dot(tril, t2, preferred_element_type=F32))


def _hi_lo(v):
    return jnp.concatenate(_split_terms(v, 2), axis=1)


def _fwd_lanes():
    lane = lax.broadcasted_iota(jnp.int32, (CHUNK, LANES), 1)
    return jnp.bitwise_and(lane, 2 * SSM_HEADS - 1) < SSM_HEADS


def _states_kernel(alog_ref, e_ref, tril_ref, xsf_ref, bf_ref, dtf_ref, xsb_ref, bb_ref, dtb_ref,
                   sf_ref, sb_ref, cf_ref, cb_ref):
    @pl.when(pl.program_id(1) == 0)
    def _():
        cf_ref[...] = jnp.zeros_like(cf_ref)
        cb_ref[...] = jnp.zeros_like(cb_ref)

    a_neg = -jnp.exp(alog_ref[...])
    tril = tril_ref[...]
    fwd_lane = _fwd_lanes()

    def prep(dt_ref):
        dt = dt_ref[...]
        a = dt * a_neg
        cs = _chunk_cumsum(tril, a)
        tot = cs[CHUNK - 1:CHUNK, :]
        w = dt * jnp.exp(jnp.where(fwd_lane, tot - cs, cs - a))
        return _hi_lo(jnp.concatenate([w, jnp.broadcast_to(jnp.exp(tot), (8, LANES))], axis=0))

    def chunk(c, d, cat, xs_ref, b_ref, out_ref, carry_ref):
        rows = slice(c * CHUNK, (c + 1) * CHUNK)
        wx = jnp.dot(cat, e_ref[c, d], preferred_element_type=F32)
        xd = (xs_ref[rows, :].astype(F32) * wx[0:CHUNK]).astype(BF16)
        dec = wx[CHUNK:CHUNK + 1, :]
        for g in range(SSM_GROUPS):
            gr = slice(g * D_STATE, (g + 1) * D_STATE)
            gc = slice(g * GROUP_W, (g + 1) * GROUP_W)
            st = lax.dot_general(b_ref[rows, gr], xd[:, gc], (((0,), (0,)), ((), ())),
                                 preferred_element_type=F32)
            prev = carry_ref[gr, :]
            out_ref[c, gr, :] = prev.astype(BF16)
            carry_ref[gr, :] = prev * dec[:, gc] + st

    cat_f = prep(dtf_ref)
    cat_b = prep(dtb_ref)
    for c in range(Q):
        chunk(c, 0, cat_f, xsf_ref, bf_ref, sf_ref, cf_ref)
    for c in reversed(range(Q)):
        chunk(c, 1, cat_b, xsb_ref, bb_ref, sb_ref, cb_ref)


def _states(a_log, e, tril, xs, bm, dt, batch, seq):
    nstep = seq // (Q * CHUNK)
    nchunk = batch * seq // CHUNK
    fwd = lambda b, i: (b * nstep + i, 0)
    bwd = lambda b, i: (b * nstep + nstep - 1 - i, 0)
    const = lambda b, i: (0, 0)
    rows = Q * CHUNK
    st_shape = jax.ShapeDtypeStruct((nchunk, SSM_GROUPS * D_STATE, GROUP_W), BF16)
    return pl.pallas_call(
        _states_kernel,
        grid=(batch, nstep),
        in_specs=[
            pl.BlockSpec((1, LANES), const),
            pl.BlockSpec((Q, 2, 2 * LANES, D_SSM), lambda b, i: (0, 0, 0, 0), pipeline_mode=pl.Buffered(1)),
            pl.BlockSpec((CHUNK, CHUNK), const),
            pl.BlockSpec((rows, D_SSM), fwd),
            pl.BlockSpec((rows, SSM_GROUPS * D_STATE), fwd),
            pl.BlockSpec((CHUNK, LANES), lambda b, i: (0, b * nstep + i)),
            pl.BlockSpec((rows, D_SSM), bwd),
            pl.BlockSpec((rows, SSM_GROUPS * D_STATE), bwd),
            pl.BlockSpec((CHUNK, LANES), lambda b, i: (0, b * nstep + nstep - 1 - i)),
        ],
        out_specs=[
            pl.BlockSpec((Q, SSM_GROUPS * D_STATE, GROUP_W), lambda b, i: (b * nstep + i, 0, 0)),
            pl.BlockSpec((Q, SSM_GROUPS * D_STATE, GROUP_W), lambda b, i: (b * nstep + nstep - 1 - i, 0, 0)),
        ],
        out_shape=[st_shape, st_shape],
        scratch_shapes=[pltpu.VMEM((SSM_GROUPS * D_STATE, GROUP_W), F32),
                        pltpu.VMEM((SSM_GROUPS * D_STATE, GROUP_W), F32)],
        compiler_params=pltpu.CompilerParams(dimension_semantics=("arbitrary", "arbitrary"),
                                             vmem_limit_bytes=VMEM_LIMIT),
        name="ssd_states",
    )(a_log, e, tril, xs, bm, dt, xs, bm, dt)


def _ssd_out_kernel(alog_ref, dskip_ref, gn_ref, e_ref, tril_ref, xs_ref, bm_ref, cm_ref, dt_ref,
                    z_ref, sf_ref, sb_ref, y_ref):
    li = lax.broadcasted_iota(jnp.int32, (CHUNK, CHUNK), 0)
    si = lax.broadcasted_iota(jnp.int32, (CHUNK, CHUNK), 1)
    lower = li >= si
    strict_lower = li > si
    strict_upper = li < si
    left = lax.broadcasted_iota(jnp.int32, (CHUNK, LANES), 1) < SSM_HEAD_DIM

    dt = dt_ref[...]
    a = dt * -jnp.exp(alog_ref[...])
    cs = _chunk_cumsum(tril_ref[...], a)
    tot = cs[CHUNK - 1:CHUNK, :]
    colq = jnp.where(_fwd_lanes(), cs, tot - cs + a)
    rowq = colq.T
    dtr = dt.T
    cat = _hi_lo(jnp.exp(colq))

    for c in range(Q):
        rows = slice(c * CHUNK, (c + 1) * CHUNK)
        ef = jnp.dot(cat, e_ref[c, 0], preferred_element_type=F32)
        eb = jnp.dot(cat, e_ref[c, 1], preferred_element_type=F32)
        for g in range(SSM_GROUPS):
            gr = slice(g * D_STATE, (g + 1) * D_STATE)
            gc = slice(g * GROUP_W, (g + 1) * GROUP_W)
            cmg = cm_ref[rows, gr]
            cb = lax.dot_general(cmg, bm_ref[rows, gr], (((1,), (1,)), ((), ())),
                                 preferred_element_type=F32)
            y_off = (jnp.dot(cmg, sf_ref[c, gr, :], preferred_element_type=F32) * ef[:, gc]
                     + jnp.dot(cmg, sb_ref[c, gr, :], preferred_element_type=F32) * eb[:, gc])
            xg = xs_ref[rows, gc]
            yg = y_off + dskip_ref[:, gc] * xg.astype(F32)
            pairs = []
            for pp in range(GROUP_W // LANES):
                scores = []
                for hh in range(2):
                    h = c * 2 * SSM_HEADS + g * (SSM_HEADS // SSM_GROUPS) + pp * 2 + hh
                    hb = h + SSM_HEADS
                    arg = jnp.where(lower, colq[:, h:h + 1] - rowq[h:h + 1, :],
                                    colq[:, hb:hb + 1] - rowq[hb:hb + 1, :])
                    wrow = jnp.where(strict_lower, dtr[h:h + 1, :],
                                     jnp.where(strict_upper, dtr[hb:hb + 1, :],
                                               dtr[h:h + 1, :] + dtr[hb:hb + 1, :]))
                    scores.append((cb * jnp.exp(arg) * wrow).astype(BF16))
                xt = xg[:, pp * LANES:(pp + 1) * LANES]
                zero = jnp.zeros_like(xt)
                rhs = jnp.concatenate([jnp.where(left, xt, zero), jnp.where(left, zero, xt)], axis=0)
                pairs.append(jnp.dot(jnp.concatenate(scores, axis=1), rhs, preferred_element_type=F32))
            yg = yg + jnp.concatenate(pairs, axis=1)
            vg = yg * _silu(z_ref[rows, gc].astype(F32))
            ms = jnp.mean(vg * vg, axis=-1, keepdims=True)
            y_ref[rows, gc] = (vg * lax.rsqrt(ms + EPS) * gn_ref[:, gc]).astype(BF16)


def _ssd_out(a_log, dskip_x, ssd_norm_g, e, tril, xs, bm, cm, dt, z, sf, sb):
    n = xs.shape[0]
    rows = Q * CHUNK
    row = lambda i: (i, 0)
    const = lambda i: (0, 0)
    return pl.pallas_call(
        _ssd_out_kernel,
        grid=(n // rows,),
        in_specs=[
            pl.BlockSpec((1, LANES), const),
            pl.BlockSpec((1, D_SSM), const),
            pl.BlockSpec((1, D_SSM), const),
            pl.BlockSpec((Q, 2, 2 * LANES, D_SSM), lambda i: (0, 0, 0, 0), pipeline_mode=pl.Buffered(1)),
            pl.BlockSpec((CHUNK, CHUNK), const),
            pl.BlockSpec((rows, D_SSM), row),
            pl.BlockSpec((rows, SSM_GROUPS * D_STATE), row),
            pl.BlockSpec((rows, SSM_GROUPS * D_STATE), row),
            pl.BlockSpec((CHUNK, LANES), lambda i: (0, i)),
            pl.BlockSpec((rows, D_SSM), row),
            pl.BlockSpec((Q, SSM_GROUPS * D_STATE, GROUP_W), lambda i: (i, 0, 0)),
            pl.BlockSpec((Q, SSM_GROUPS * D_STATE, GROUP_W), lambda i: (i, 0, 0)),
        ],
        out_specs=pl.BlockSpec((rows, D_SSM), row),
        out_shape=jax.ShapeDtypeStruct((n, D_SSM), BF16),
        compiler_params=pltpu.CompilerParams(dimension_semantics=("parallel",),
                                             vmem_limit_bytes=VMEM_LIMIT),
        name="ssd_out",
    )(a_log, dskip_x, ssd_norm_g, e, tril, xs, bm, cm, dt, z, sf, sb)


def _bucket_table_t():
    rel = jnp.arange(3 * BLK)[:, None] - BLK - jnp.arange(BLK)[None, :]
    half = NUM_BUCKETS // 2
    ret = (rel > 0).astype(jnp.int32) * half
    n = jnp.abs(rel)
    nf = jnp.maximum(n, 1).astype(F32)
    large = MAX_EXACT + (jnp.log(nf / MAX_EXACT) / math.log(MAX_DISTANCE / MAX_EXACT)
                         * (half - MAX_EXACT)).astype(jnp.int32)
    large = jnp.minimum(large, half - 1)
    return (ret + jnp.where(n < MAX_EXACT, n, large)).astype(jnp.int32)


def _attn_kernel(nblk, rb_ref, bucket_ref, sink_ref, gn_ref, q_ref, kp_ref, kc_ref, kn_ref,
                 vp_ref, vc_ref, vn_ref, ga_ref, y_ref, tab_ref, lg_ref, m_ref, ot_ref):
    j = pl.program_id(1)

    @pl.when((pl.program_id(0) == 0) & (j == 0))
    def _init():
        lg_ref[...] = jnp.zeros_like(lg_ref)
        m_ref[...] = jnp.zeros_like(m_ref)
        bk = bucket_ref[...]
        t = lax.broadcasted_iota(jnp.int32, (3 * BLK, BLK), 0)
        qq = lax.broadcasted_iota(jnp.int32, (3 * BLK, BLK), 1)
        in_window = jnp.abs(t - BLK - qq) <= WINDOW

        def per_head(h, carry):
            acc = jnp.zeros((3 * BLK, BLK), F32)
            for b in range(NUM_BUCKETS):
                acc = jnp.where(bk == b, rb_ref[b, h], acc)
            tab = jnp.where(in_window, acc * LOG2E, NEG_INF)
            tab_ref[0, h] = tab
            tab_ref[1, h] = jnp.where(t >= BLK, tab, NEG_INF)
            tab_ref[2, h] = jnp.where(t < 2 * BLK, tab, NEG_INF)
            return carry

        lax.fori_loop(0, ATTN_HEADS, per_head, 0)

    left = lax.broadcasted_iota(jnp.int32, (BLK, LANES), 1) < ATTN_HEAD_DIM
    left3 = lax.broadcasted_iota(jnp.int32, (3 * BLK, LANES), 1) < ATTN_HEAD_DIM
    s2 = sink_ref[...] * LOG2E

    ja = jnp.minimum(j, nblk - 1)
    var = jnp.where(ja == 0, 1, jnp.where(ja == nblk - 1, 2, 0))
    sa = lax.rem(j, 2)
    kcat = jnp.concatenate([kp_ref[...], kc_ref[...], kn_ref[...]], axis=0)
    for p in range(ATTN_HEADS // 2):
        kd = kcat[:, (p // 2) * LANES:(p // 2 + 1) * LANES]
        qt = q_ref[:, p * LANES:(p + 1) * LANES]
        zero = jnp.zeros_like(qt)
        rhs_t = jnp.concatenate([jnp.where(left, qt, zero), jnp.where(left, zero, qt)], axis=0)
        lg = lax.dot_general(kd, rhs_t, (((1,), (1,)), ((), ())), preferred_element_type=F32)
        lg = lg + jnp.concatenate([tab_ref[var, 2 * p], tab_ref[var, 2 * p + 1]], axis=1)
        lg_ref[sa, p] = lg
        m_ref[sa, p] = jnp.maximum(jnp.max(lg, axis=0, keepdims=True), s2[p:p + 1, :])

    sb = 1 - sa
    vcat = jnp.concatenate([vp_ref[...], vc_ref[...], vn_ref[...]], axis=0)
    one = jnp.ones((3 * BLK, LANES), BF16)
    for kv in range(KV_HEADS):
        vt = vcat[:, (kv // 2) * LANES:(kv // 2 + 1) * LANES]
        vmod = jnp.where(left3, vt, one) if kv % 2 == 0 else jnp.where(left3, one, vt)
        r0 = (kv % 2) * ATTN_HEAD_DIM
        d0 = ATTN_HEAD_DIM - r0
        for gp in range(2):
            p = kv * 2 + gp
            m = m_ref[sb, p]
            pe = jnp.exp2(lg_ref[sb, p] - m).astype(BF16)
            o2 = lax.dot_general(vmod, pe, (((0,), (0,)), ((), ())), preferred_element_type=F32)
            den = o2[d0:d0 + 1, :] + jnp.exp2(s2[p:p + 1, :] - m)
            o = o2[r0:r0 + ATTN_HEAD_DIM, :] / den
            ot_ref[2 * p * ATTN_HEAD_DIM:(2 * p + 1) * ATTN_HEAD_DIM, :] = o[:, 0:BLK]
            ot_ref[(2 * p + 1) * ATTN_HEAD_DIM:(2 * p + 2) * ATTN_HEAD_DIM, :] = o[:, BLK:2 * BLK]

    o = ot_ref[...].T
    v = o * _silu(ga_ref[...].astype(F32))
    ms = jnp.mean(v * v, axis=-1, keepdims=True)
    y_ref[...] = (v * lax.rsqrt(ms + EPS) * gn_ref[...]).astype(BF16)


def _attn(rel_bias, bucket_t, sink_x, attn_norm_g, q, kd, v, ga, batch, seq):
    n = q.shape[0]
    nblk = seq // BLK
    blk_a = lambda j: jnp.minimum(j, nblk - 1)
    blk_b = lambda j: jnp.maximum(j - 1, 0)
    a_cur = lambda b, j: (b * nblk + blk_a(j), 0)
    a_prev = lambda b, j: (b * nblk + jnp.maximum(blk_a(j) - 1, 0), 0)
    a_next = lambda b, j: (b * nblk + jnp.minimum(blk_a(j) + 1, nblk - 1), 0)
    b_cur = lambda b, j: (b * nblk + blk_b(j), 0)
    b_prev = lambda b, j: (b * nblk + jnp.maximum(blk_b(j) - 1, 0), 0)
    b_next = lambda b, j: (b * nblk + jnp.minimum(blk_b(j) + 1, nblk - 1), 0)
    const = lambda b, j: (0, 0)
    kw = 2 * KV_HEADS * ATTN_HEAD_DIM
    vw = KV_HEADS * ATTN_HEAD_DIM
    return pl.pallas_call(
        lambda *refs: _attn_kernel(nblk, *refs),
        grid=(batch, nblk + 1),
        in_specs=[
            pl.BlockSpec(memory_space=pltpu.SMEM),
            pl.BlockSpec((3 * BLK, BLK), const),
            pl.BlockSpec((ATTN_HEADS // 2, 2 * BLK), const),
            pl.BlockSpec((1, D_ATTN), const),
            pl.BlockSpec((BLK, D_ATTN), a_cur),
            pl.BlockSpec((BLK, kw), a_prev), pl.BlockSpec((BLK, kw), a_cur), pl.BlockSpec((BLK, kw), a_next),
            pl.BlockSpec((BLK, vw), b_prev), pl.BlockSpec((BLK, vw), b_cur), pl.BlockSpec((BLK, vw), b_next),
            pl.BlockSpec((BLK, D_ATTN), b_cur),
        ],
        out_specs=pl.BlockSpec((BLK, D_ATTN), b_cur),
        out_shape=jax.ShapeDtypeStruct((n, D_ATTN), BF16),
        scratch_shapes=[pltpu.VMEM((3, ATTN_HEADS, 3 * BLK, BLK), F32),
                        pltpu.VMEM((2, ATTN_HEADS // 2, 3 * BLK, 2 * BLK), F32),
                        pltpu.VMEM((2, ATTN_HEADS // 2, 1, 2 * BLK), F32),
                        pltpu.VMEM((D_ATTN, BLK), F32)],
        compiler_params=pltpu.CompilerParams(dimension_semantics=("arbitrary", "arbitrary"),
                                             vmem_limit_bytes=VMEM_LIMIT),
        name="attn",
    )(rel_bias, bucket_t, sink_x, attn_norm_g, q, kd, kd, kd, v, v, v, ga)


def _outproj_kernel(x_ref, ys_ref, ya_ref, w_ref, g_ref, o_ref):
    acc = (jnp.dot(ys_ref[...], w_ref[0:D_SSM, :], preferred_element_type=F32)
           + jnp.dot(ya_ref[...], w_ref[D_SSM:D_MIX, :], preferred_element_type=F32))
    hres = x_ref[...] + acc
    ms = jnp.mean(hres * hres, axis=-1, keepdims=True)
    o_ref[...] = hres * lax.rsqrt(ms + EPS) * g_ref[...]


def _outproj(x2, y_ssd, y_attn, w_out, final_norm_g):
    n = x2.shape[0]
    row = lambda i: (i, 0)
    const = lambda i: (0, 0)
    return pl.pallas_call(
        _outproj_kernel,
        grid=(n // TM,),
        in_specs=[
            pl.BlockSpec((TM, D_MODEL), row),
            pl.BlockSpec((TM, D_SSM), row),
            pl.BlockSpec((TM, D_ATTN), row),
            pl.BlockSpec((D_MIX, D_MODEL), const, pipeline_mode=pl.Buffered(1)),
            pl.BlockSpec((1, D_MODEL), const),
        ],
        out_specs=pl.BlockSpec((TM, D_MODEL), row),
        out_shape=jax.ShapeDtypeStruct((n, D_MODEL), F32),
        compiler_params=pltpu.CompilerParams(dimension_semantics=("parallel",),
                                             vmem_limit_bytes=VMEM_LIMIT),
        name="outproj",
    )(x2, y_ssd, y_attn, w_out, final_norm_g)


def _arrange_w_in(w_in):
    sizes = [D_SSM, CONV_CH, 2 * SSM_HEADS, D_ATTN, KV_HEADS * ATTN_HEAD_DIM,
             KV_HEADS * ATTN_HEAD_DIM, D_ATTN]
    wz, wxbc, wdt, wq, wk, wv, wga = jnp.split(w_in, [int(s) for s in np.cumsum(sizes)[:-1]], axis=1)
    wk = wk.reshape(D_MODEL, KV_HEADS, 1, ATTN_HEAD_DIM)
    wkd = jnp.broadcast_to(wk, (D_MODEL, KV_HEADS, 2, ATTN_HEAD_DIM)).reshape(D_MODEL, -1)
    wdt = jnp.pad(wdt, ((0, 0), (0, LANES - 2 * SSM_HEADS)))
    return jnp.concatenate([wz, wxbc, wq, wkd, wv, wga, wdt], axis=1).astype(BF16)


def _expand_matrices():
    e = np.zeros((Q, 2, 2 * LANES, D_SSM), np.float32)
    for c in range(Q):
        for d in range(2):
            for h in range(SSM_HEADS):
                r = (c * 2 + d) * SSM_HEADS + h
                e[c, d, r, h * SSM_HEAD_DIM:(h + 1) * SSM_HEAD_DIM] = 1.0
                e[c, d, LANES + r, h * SSM_HEAD_DIM:(h + 1) * SSM_HEAD_DIM] = 1.0
    return e


def kernel(x, norm_in_g, w_in, conv_w, conv_b, dt_bias, a_log, d_skip, ssd_norm_g, rel_bias, sink,
           attn_norm_g, w_out, final_norm_g):
    batch, seq, _ = x.shape
    assert w_out.shape[0] == 1 and seq % (Q * CHUNK) == 0 and seq % TM == 0
    x2 = x.reshape(batch * seq, D_MODEL)

    w_all = _arrange_w_in(w_in)
    e = jnp.asarray(_expand_matrices(), BF16)
    tril = jnp.asarray(np.tril(np.ones((CHUNK, CHUNK), np.float32)), BF16)
    a_log2 = jnp.tile(a_log.reshape(1, 2 * SSM_HEADS).astype(F32), (1, Q))
    dt_bias_x = jnp.pad(dt_bias.reshape(1, 2 * SSM_HEADS).astype(F32), ((0, 0), (0, LANES - 2 * SSM_HEADS)))
    dskip_x = jnp.repeat(d_skip.astype(F32), SSM_HEAD_DIM).reshape(1, D_SSM)
    sink_x = jnp.repeat(sink.astype(F32), BLK).reshape(ATTN_HEADS // 2, 2 * BLK)

    z, xs, bm, cm, dt, q, kd, v, ga = _inproj(
        x2, norm_in_g.reshape(1, D_MODEL), w_all, conv_w, conv_b.reshape(1, CONV_CH),
        dt_bias_x, seq)
    sf, sb = _states(a_log2, e, tril, xs, bm, dt, batch, seq)
    y_ssd = _ssd_out(a_log2, dskip_x, ssd_norm_g.reshape(1, D_SSM), e, tril, xs, bm, cm, dt, z, sf, sb)
    y_attn = _attn(rel_bias.astype(F32), _bucket_table_t(), sink_x,
                   attn_norm_g.reshape(1, D_ATTN), q, kd, v, ga, batch, seq)
    out = _outproj(x2, y_ssd, y_attn, w_out[0].astype(BF16), final_norm_g.reshape(1, D_MODEL))
    return out.reshape(batch, seq, D_MODEL)
```

```python
import math

import numpy as np
import jax
import jax.numpy as jnp
from jax import lax
from jax.experimental import pallas as pl
from jax.experimental.pallas import tpu as pltpu

F32 = jnp.float32
BF16 = jnp.bfloat16

D_MODEL = 1024
D_SSM = 1024
D_ATTN = 1024
D_MIX = D_SSM + D_ATTN
SSM_HEAD_DIM = 64
SSM_HEADS = D_SSM // SSM_HEAD_DIM
SSM_GROUPS = 2
GROUP_W = D_SSM // SSM_GROUPS
D_STATE = 128
D_CONV = 5
CHUNK = 128
CONV_CH = D_SSM + 2 * SSM_GROUPS * D_STATE
ATTN_HEAD_DIM = 64
ATTN_HEADS = D_ATTN // ATTN_HEAD_DIM
KV_HEADS = 4
WINDOW = 128
BLK = 128
NUM_BUCKETS = 32
MAX_DISTANCE = 128
MAX_EXACT = 8
EPS = 1e-6
NEG_INF = -1e30
SCALE = ATTN_HEAD_DIM ** -0.5
LOG2E = math.log2(math.e)

LANES = 128
HALO = 8
TM = 512
CONV_ROWS = 64
Q = 4
VMEM_LIMIT = 56 * 1024 * 1024

OFF_Z = 0
OFF_XBC = OFF_Z + D_SSM
OFF_Q = OFF_XBC + CONV_CH
OFF_KD = OFF_Q + D_ATTN
OFF_V = OFF_KD + 2 * KV_HEADS * ATTN_HEAD_DIM
OFF_GA = OFF_V + KV_HEADS * ATTN_HEAD_DIM
OFF_DT = OFF_GA + D_ATTN
W_COLS = OFF_DT + LANES
assert TM == Q * CHUNK and Q * 2 * SSM_HEADS == LANES


def _silu(v):
    return v * jax.nn.sigmoid(v)


def _softplus(v):
    return jnp.maximum(v, 0.0) + jnp.log1p(jnp.exp(-jnp.abs(v)))


def _inproj_kernel(tiles_per_seq, x_ref, g_ref, w_ref, cw_ref, cb_ref, dtb_ref,
                   z_ref, xs_ref, bm_ref, cm_ref, dt_ref, q_ref, kd_ref, v_ref, ga_ref, pad_ref):
    i = pl.program_id(0)

    @pl.when(i == 0)
    def _():
        pad_ref[...] = jnp.zeros_like(pad_ref)

    j = lax.rem(i, tiles_per_seq)
    g = g_ref[...]

    def proj(hv, off, n):
        return jnp.dot(hv, w_ref[:, off:off + n], preferred_element_type=F32)

    xv = x_ref[...]
    ms = jnp.mean(xv * xv, axis=-1, keepdims=True)
    h = (xv * lax.rsqrt(ms + EPS) * g).astype(BF16)

    s_cur = lax.rem(i, 3)
    s_prev = lax.rem(i + 2, 3)
    s_next = lax.rem(i + 1, 3)
    xbc = proj(h, OFF_XBC, CONV_CH)
    pad_ref[s_cur, HALO:HALO + TM, :] = xbc
    pad_ref[s_prev, HALO + TM:2 * HALO + TM, :] = xbc[0:HALO] * jnp.where(j > 0, 1.0, 0.0)
    pad_ref[s_next, 0:HALO, :] = xbc[TM - HALO:TM] * jnp.where(j < tiles_per_seq - 1, 1.0, 0.0)

    z_ref[...] = proj(h, OFF_Z, D_SSM).astype(BF16)
    q_ref[...] = (proj(h, OFF_Q, D_ATTN) * (SCALE * LOG2E)).astype(BF16)
    kd_ref[...] = proj(h, OFF_KD, 2 * KV_HEADS * ATTN_HEAD_DIM).astype(BF16)
    v_ref[...] = proj(h, OFF_V, KV_HEADS * ATTN_HEAD_DIM).astype(BF16)
    ga_ref[...] = proj(h, OFF_GA, D_ATTN).astype(BF16)
    dt_lane = lax.broadcasted_iota(jnp.int32, (TM, LANES), 1) < 2 * SSM_HEADS
    dt = jnp.where(dt_lane, _softplus(proj(h, OFF_DT, LANES) + dtb_ref[...]), 0.0)
    dtc = dt[0:CHUNK]
    for c in range(1, Q):
        dtc = dtc + pltpu.roll(dt[c * CHUNK:(c + 1) * CHUNK], 2 * SSM_HEADS * c, axis=1)
    dt_ref[...] = dtc

    base = HALO - (D_CONV - 1) // 2
    for cc in range(CONV_CH // LANES):
        sl = slice(cc * LANES, (cc + 1) * LANES)
        if cc < D_SSM // LANES:
            dst, o = xs_ref, cc * LANES
        elif cc < (D_SSM + SSM_GROUPS * D_STATE) // LANES:
            dst, o = bm_ref, cc * LANES - D_SSM
        else:
            dst, o = cm_ref, cc * LANES - D_SSM - SSM_GROUPS * D_STATE
        for r0 in range(0, TM, CONV_ROWS):
            acc = cb_ref[:, sl] + cw_ref[0:1, sl] * pad_ref[s_prev, base + r0:base + r0 + CONV_ROWS, sl]
            for k in range(1, D_CONV):
                acc = acc + cw_ref[k:k + 1, sl] * pad_ref[s_prev, base + r0 + k:base + r0 + k + CONV_ROWS, sl]
            dst[r0:r0 + CONV_ROWS, o:o + LANES] = _silu(acc).astype(BF16)


def _inproj(x2, norm_in_g, w_all, conv_w, conv_b, dt_bias, seq):
    n = x2.shape[0]
    tiles_per_seq = seq // TM
    ntile = n // TM
    cur = lambda i: (jnp.minimum(i, ntile - 1), 0)
    lag = lambda i: (jnp.maximum(i - 1, 0), 0)
    const = lambda i: (0, 0)
    out_cols = (D_SSM, D_SSM, SSM_GROUPS * D_STATE, SSM_GROUPS * D_STATE,
                D_ATTN, 2 * KV_HEADS * ATTN_HEAD_DIM, KV_HEADS * ATTN_HEAD_DIM, D_ATTN)
    out_maps = (cur, lag, lag, lag, cur, cur, cur, cur)
    out_specs = [pl.BlockSpec((TM, c), m) for c, m in zip(out_cols, out_maps)]
    out_shape = [jax.ShapeDtypeStruct((n, c), BF16) for c in out_cols]
    out_specs.insert(4, pl.BlockSpec((CHUNK, LANES), lambda i: (0, jnp.minimum(i, ntile - 1))))
    out_shape.insert(4, jax.ShapeDtypeStruct((CHUNK, ntile * LANES), F32))
    return pl.pallas_call(
        lambda *refs: _inproj_kernel(tiles_per_seq, *refs),
        grid=(ntile + 1,),
        in_specs=[
            pl.BlockSpec((TM, D_MODEL), cur),
            pl.BlockSpec((1, D_MODEL), const),
            pl.BlockSpec((D_MODEL, W_COLS), const, pipeline_mode=pl.Buffered(1)),
            pl.BlockSpec((D_CONV, CONV_CH), const),
            pl.BlockSpec((1, CONV_CH), const),
            pl.BlockSpec((1, LANES), const),
        ],
        out_specs=out_specs,
        out_shape=out_shape,
        scratch_shapes=[pltpu.VMEM((3, TM + 2 * HALO, CONV_CH), F32)],
        compiler_params=pltpu.CompilerParams(dimension_semantics=("arbitrary",),
                                             vmem_limit_bytes=VMEM_LIMIT),
        name="inproj",
    )(x2, norm_in_g, w_all, conv_w, conv_b, dt_bias)


def _split_terms(v, n):
    terms, r = [], v
    for _ in range(n):
        t = r.astype(BF16)
        terms.append(t)
        r = r - t.astype(F32)
    return terms


def _chunk_cumsum(tril, a):
    t0, t1, t2 = _split_terms(a, 3)
    return (jnp.dot(tril, t0, preferred_element_type=F32) + jnp.dot(tril, t1, preferred_element_type=F32)
            + jnp.dot(tril, t2, preferred_element_type=F32))


def _hi_lo(v):
    return jnp.concatenate(_split_terms(v, 2), axis=1)


def _fwd_lanes():
    lane = lax.broadcasted_iota(jnp.int32, (CHUNK, LANES), 1)
    return jnp.bitwise_and(lane, 2 * SSM_HEADS - 1) < SSM_HEADS


def _states_kernel(alog_ref, e_ref, tril_ref, xsf_ref, bf_ref, dtf_ref, xsb_ref, bb_ref, dtb_ref,
                   sf_ref, sb_ref, cf_ref, cb_ref):
    @pl.when(pl.program_id(1) == 0)
    def _():
        cf_ref[...] = jnp.zeros_like(cf_ref)
        cb_ref[...] = jnp.zeros_like(cb_ref)

    a_neg = -jnp.exp(alog_ref[...])
    tril = tril_ref[...]
    fwd_lane = _fwd_lanes()

    def prep(dt_ref):
        dt = dt_ref[...]
        a = dt * a_neg
        cs = _chunk_cumsum(tril, a)
        tot = cs[CHUNK - 1:CHUNK, :]
        w = dt * jnp.exp(jnp.where(fwd_lane, tot - cs, cs - a))
        return _hi_lo(jnp.concatenate([w, jnp.broadcast_to(jnp.exp(tot), (8, LANES))], axis=0))

    def chunk(c, d, cat, xs_ref, b_ref, out_ref, carry_ref):
        rows = slice(c * CHUNK, (c + 1) * CHUNK)
        wx = jnp.dot(cat, e_ref[c, d], preferred_element_type=F32)
        xd = (xs_ref[rows, :].astype(F32) * wx[0:CHUNK]).astype(BF16)
        dec = wx[CHUNK:CHUNK + 1, :]
        for g in range(SSM_GROUPS):
            gr = slice(g * D_STATE, (g + 1) * D_STATE)
            gc = slice(g * GROUP_W, (g + 1) * GROUP_W)
            st = lax.dot_general(b_ref[rows, gr], xd[:, gc], (((0,), (0,)), ((), ())),
                                 preferred_element_type=F32)
            prev = carry_ref[gr, :]
            out_ref[c, gr, :] = prev.astype(BF16)
            carry_ref[gr, :] = prev * dec[:, gc] + st

    cat_f = prep(dtf_ref)
    cat_b = prep(dtb_ref)
    for c in range(Q):
        chunk(c, 0, cat_f, xsf_ref, bf_ref, sf_ref, cf_ref)
    for c in reversed(range(Q)):
        chunk(c, 1, cat_b, xsb_ref, bb_ref, sb_ref, cb_ref)


def _states(a_log, e, tril, xs, bm, dt, batch, seq):
    nstep = seq // (Q * CHUNK)
    nchunk = batch * seq // CHUNK
    fwd = lambda b, i: (b * nstep + i, 0)
    bwd = lambda b, i: (b * nstep + nstep - 1 - i, 0)
    const = lambda b, i: (0, 0)
    rows = Q * CHUNK
    st_shape = jax.ShapeDtypeStruct((nchunk, SSM_GROUPS * D_STATE, GROUP_W), BF16)
    return pl.pallas_call(
        _states_kernel,
        grid=(batch, nstep),
        in_specs=[
            pl.BlockSpec((1, LANES), const),
            pl.BlockSpec((Q, 2, 2 * LANES, D_SSM), lambda b, i: (0, 0, 0, 0), pipeline_mode=pl.Buffered(1)),
            pl.BlockSpec((CHUNK, CHUNK), const),
            pl.BlockSpec((rows, D_SSM), fwd),
            pl.BlockSpec((rows, SSM_GROUPS * D_STATE), fwd),
            pl.BlockSpec((CHUNK, LANES), lambda b, i: (0, b * nstep + i)),
            pl.BlockSpec((rows, D_SSM), bwd),
            pl.BlockSpec((rows, SSM_GROUPS * D_STATE), bwd),
            pl.BlockSpec((CHUNK, LANES), lambda b, i: (0, b * nstep + nstep - 1 - i)),
        ],
        out_specs=[
            pl.BlockSpec((Q, SSM_GROUPS * D_STATE, GROUP_W), lambda b, i: (b * nstep + i, 0, 0)),
            pl.BlockSpec((Q, SSM_GROUPS * D_STATE, GROUP_W), lambda b, i: (b * nstep + nstep - 1 - i, 0, 0)),
        ],
        out_shape=[st_shape, st_shape],
        scratch_shapes=[pltpu.VMEM((SSM_GROUPS * D_STATE, GROUP_W), F32),
                        pltpu.VMEM((SSM_GROUPS * D_STATE, GROUP_W), F32)],
        compiler_params=pltpu.CompilerParams(dimension_semantics=("arbitrary", "arbitrary"),
                                             vmem_limit_bytes=VMEM_LIMIT),
        name="ssd_states",
    )(a_log, e, tril, xs, bm, dt, xs, bm, dt)


def _ssd_out_kernel(alog_ref, dskip_ref, gn_ref, e_ref, tril_ref, xs_ref, bm_ref, cm_ref, dt_ref,
                    z_ref, sf_ref, sb_ref, y_ref):
    li = lax.broadcasted_iota(jnp.int32, (CHUNK, CHUNK), 0)
    si = lax.broadcasted_iota(jnp.int32, (CHUNK, CHUNK), 1)
    lower = li >= si
    strict_lower = li > si
    strict_upper = li < si
    left = lax.broadcasted_iota(jnp.int32, (CHUNK, LANES), 1) < SSM_HEAD_DIM

    dt = dt_ref[...]
    a = dt * -jnp.exp(alog_ref[...])
    cs = _chunk_cumsum(tril_ref[...], a)
    tot = cs[CHUNK - 1:CHUNK, :]
    colq = jnp.where(_fwd_lanes(), cs, tot - cs + a)
    rowq = colq.T
    dtr = dt.T
    cat = _hi_lo(jnp.exp(colq))

    for c in range(Q):
        rows = slice(c * CHUNK, (c + 1) * CHUNK)
        ef = jnp.dot(cat, e_ref[c, 0], preferred_element_type=F32)
        eb = jnp.dot(cat, e_ref[c, 1], preferred_element_type=F32)
        for g in range(SSM_GROUPS):
            gr = slice(g * D_STATE, (g + 1) * D_STATE)
            gc = slice(g * GROUP_W, (g + 1) * GROUP_W)
            cmg = cm_ref[rows, gr]
            cb = lax.dot_general(cmg, bm_ref[rows, gr], (((1,), (1,)), ((), ())),
                                 preferred_element_type=F32)
            y_off = (jnp.dot(cmg, sf_ref[c, gr, :], preferred_element_type=F32) * ef[:, gc]
                     + jnp.dot(cmg, sb_ref[c, gr, :], preferred_element_type=F32) * eb[:, gc])
            xg = xs_ref[rows, gc]
            yg = y_off + dskip_ref[:, gc] * xg.astype(F32)
            pairs = []
            for pp in range(GROUP_W // LANES):
                scores = []
                for hh in range(2):
                    h = c * 2 * SSM_HEADS + g * (SSM_HEADS // SSM_GROUPS) + pp * 2 + hh
                    hb = h + SSM_HEADS
                    arg = jnp.where(lower, colq[:, h:h + 1] - rowq[h:h + 1, :],
                                    colq[:, hb:hb + 1] - rowq[hb:hb + 1, :])
                    wrow = jnp.where(strict_lower, dtr[h:h + 1, :],
                                     jnp.where(strict_upper, dtr[hb:hb + 1, :],
                                               dtr[h:h + 1, :] + dtr[hb:hb + 1, :]))
                    scores.append((cb * jnp.exp(arg) * wrow).astype(BF16))
                xt = xg[:, pp * LANES:(pp + 1) * LANES]
                zero = jnp.zeros_like(xt)
                rhs = jnp.concatenate([jnp.where(left, xt, zero), jnp.where(left, zero, xt)], axis=0)
                pairs.append(jnp.dot(jnp.concatenate(scores, axis=1), rhs, preferred_element_type=F32))
            yg = yg + jnp.concatenate(pairs, axis=1)
            vg = yg * _silu(z_ref[rows, gc].astype(F32))
            ms = jnp.mean(vg * vg, axis=-1, keepdims=True)
            y_ref[rows, gc] = (vg * lax.rsqrt(ms + EPS) * gn_ref[:, gc]).astype(BF16)


def _ssd_out(a_log, dskip_x, ssd_norm_g, e, tril, xs, bm, cm, dt, z, sf, sb):
    n = xs.shape[0]
    rows = Q * CHUNK
    row = lambda i: (i, 0)
    const = lambda i: (0, 0)
    return pl.pallas_call(
        _ssd_out_kernel,
        grid=(n // rows,),
        in_specs=[
            pl.BlockSpec((1, LANES), const),
            pl.BlockSpec((1, D_SSM), const),
            pl.BlockSpec((1, D_SSM), const),
            pl.BlockSpec((Q, 2, 2 * LANES, D_SSM), lambda i: (0, 0, 0, 0), pipeline_mode=pl.Buffered(1)),
            pl.BlockSpec((CHUNK, CHUNK), const),
            pl.BlockSpec((rows, D_SSM), row),
            pl.BlockSpec((rows, SSM_GROUPS * D_STATE), row),
            pl.BlockSpec((rows, SSM_GROUPS * D_STATE), row),
            pl.BlockSpec((CHUNK, LANES), lambda i: (0, i)),
            pl.BlockSpec((rows, D_SSM), row),
            pl.BlockSpec((Q, SSM_GROUPS * D_STATE, GROUP_W), lambda i: (i, 0, 0)),
            pl.BlockSpec((Q, SSM_GROUPS * D_STATE, GROUP_W), lambda i: (i, 0, 0)),
        ],
        out_specs=pl.BlockSpec((rows, D_SSM), row),
        out_shape=jax.ShapeDtypeStruct((n, D_SSM), BF16),
        compiler_params=pltpu.CompilerParams(dimension_semantics=("parallel",),
                                             vmem_limit_bytes=VMEM_LIMIT),
        name="ssd_out",
    )(a_log, dskip_x, ssd_norm_g, e, tril, xs, bm, cm, dt, z, sf, sb)


def _bucket_table_t():
    rel = jnp.arange(3 * BLK)[:, None] - BLK - jnp.arange(BLK)[None, :]
    half = NUM_BUCKETS // 2
    ret = (rel > 0).astype(jnp.int32) * half
    n = jnp.abs(rel)
    nf = jnp.maximum(n, 1).astype(F32)
    large = MAX_EXACT + (jnp.log(nf / MAX_EXACT) / math.log(MAX_DISTANCE / MAX_EXACT)
                         * (half - MAX_EXACT)).astype(jnp.int32)
    large = jnp.minimum(large, half - 1)
    return (ret + jnp.where(n < MAX_EXACT, n, large)).astype(jnp.int32)


def _attn_kernel(nblk, rb_ref, bucket_ref, sink_ref, gn_ref, q_ref, kp_ref, kc_ref, kn_ref,
                 vp_ref, vc_ref, vn_ref, ga_ref, y_ref, tab_ref, lg_ref, m_ref, ot_ref):
    j = pl.program_id(1)

    @pl.when((pl.program_id(0) == 0) & (j == 0))
    def _init():
        lg_ref[...] = jnp.zeros_like(lg_ref)
        m_ref[...] = jnp.zeros_like(m_ref)
        bk = bucket_ref[...]
        t = lax.broadcasted_iota(jnp.int32, (3 * BLK, BLK), 0)
        qq = lax.broadcasted_iota(jnp.int32, (3 * BLK, BLK), 1)
        in_window = jnp.abs(t - BLK - qq) <= WINDOW

        def per_head(h, carry):
            acc = jnp.zeros((3 * BLK, BLK), F32)
            for b in range(NUM_BUCKETS):
                acc = jnp.where(bk == b, rb_ref[b, h], acc)
            tab = jnp.where(in_window, acc * LOG2E, NEG_INF)
            tab_ref[0, h] = tab
            tab_ref[1, h] = jnp.where(t >= BLK, tab, NEG_INF)
            tab_ref[2, h] = jnp.where(t < 2 * BLK, tab, NEG_INF)
            return carry

        lax.fori_loop(0, ATTN_HEADS, per_head, 0)

    left = lax.broadcasted_iota(jnp.int32, (BLK, LANES), 1) < ATTN_HEAD_DIM
    left3 = lax.broadcasted_iota(jnp.int32, (3 * BLK, LANES), 1) < ATTN_HEAD_DIM
    s2 = sink_ref[...] * LOG2E

    ja = jnp.minimum(j, nblk - 1)
    var = jnp.where(ja == 0, 1, jnp.where(ja == nblk - 1, 2, 0))
    sa = lax.rem(j, 2)
    kcat = jnp.concatenate([kp_ref[...], kc_ref[...], kn_ref[...]], axis=0)
    for p in range(ATTN_HEADS // 2):
        kd = kcat[:, (p // 2) * LANES:(p // 2 + 1) * LANES]
        qt = q_ref[:, p * LANES:(p + 1) * LANES]
        zero = jnp.zeros_like(qt)
        rhs_t = jnp.concatenate([jnp.where(left, qt, zero), jnp.where(left, zero, qt)], axis=0)
        lg = lax.dot_general(kd, rhs_t, (((1,), (1,)), ((), ())), preferred_element_type=F32)
        lg = lg + jnp.concatenate([tab_ref[var, 2 * p], tab_ref[var, 2 * p + 1]], axis=1)
        lg_ref[sa, p] = lg
        m_ref[sa, p] = jnp.maximum(jnp.max(lg, axis=0, keepdims=True), s2[p:p + 1, :])

    sb = 1 - sa
    vcat = jnp.concatenate([vp_ref[...], vc_ref[...], vn_ref[...]], axis=0)
    one = jnp.ones((3 * BLK, LANES), BF16)
    for kv in range(KV_HEADS):
        vt = vcat[:, (kv // 2) * LANES:(kv // 2 + 1) * LANES]
        vmod = jnp.where(left3, vt, one) if kv % 2 == 0 else jnp.where(left3, one, vt)
        r0 = (kv % 2) * ATTN_HEAD_DIM
        d0 = ATTN_HEAD_DIM - r0
        for gp in range(2):
            p = kv * 2 + gp
            m = m_ref[sb, p]
            pe = jnp.exp2(lg_ref[sb, p] - m).astype(BF16)
            o2 = lax.dot_general(vmod, pe, (((0,), (0,)), ((), ())), preferred_element_type=F32)
            den = o2[d0:d0 + 1, :] + jnp.exp2(s2[p:p + 1, :] - m)
            o = o2[r0:r0 + ATTN_HEAD_DIM, :] / den
            ot_ref[2 * p * ATTN_HEAD_DIM:(2 * p + 1) * ATTN_HEAD_DIM, :] = o[:, 0:BLK]
            ot_ref[(2 * p + 1) * ATTN_HEAD_DIM:(2 * p + 2) * ATTN_HEAD_DIM, :] = o[:, BLK:2 * BLK]

    o = ot_ref[...].T
    v = o * _silu(ga_ref[...].astype(F32))
    ms = jnp.mean(v * v, axis=-1, keepdims=True)
    y_ref[...] = (v * lax.rsqrt(ms + EPS) * gn_ref[...]).astype(BF16)


def _attn(rel_bias, bucket_t, sink_x, attn_norm_g, q, kd, v, ga, batch, seq):
    n = q.shape[0]
    nblk = seq // BLK
    blk_a = lambda j: jnp.minimum(j, nblk - 1)
    blk_b = lambda j: jnp.maximum(j - 1, 0)
    a_cur = lambda b, j: (b * nblk + blk_a(j), 0)
    a_prev = lambda b, j: (b * nblk + jnp.maximum(blk_a(j) - 1, 0), 0)
    a_next = lambda b, j: (b * nblk + jnp.minimum(blk_a(j) + 1, nblk - 1), 0)
    b_cur = lambda b, j: (b * nblk + blk_b(j), 0)
    b_prev = lambda b, j: (b * nblk + jnp.maximum(blk_b(j) - 1, 0), 0)
    b_next = lambda b, j: (b * nblk + jnp.minimum(blk_b(j) + 1, nblk - 1), 0)
    const = lambda b, j: (0, 0)
    kw = 2 * KV_HEADS * ATTN_HEAD_DIM
    vw = KV_HEADS * ATTN_HEAD_DIM
    return pl.pallas_call(
        lambda *refs: _attn_kernel(nblk, *refs),
        grid=(batch, nblk + 1),
        in_specs=[
            pl.BlockSpec(memory_space=pltpu.SMEM),
            pl.BlockSpec((3 * BLK, BLK), const),
            pl.BlockSpec((ATTN_HEADS // 2, 2 * BLK), const),
            pl.BlockSpec((1, D_ATTN), const),
            pl.BlockSpec((BLK, D_ATTN), a_cur),
            pl.BlockSpec((BLK, kw), a_prev), pl.BlockSpec((BLK, kw), a_cur), pl.BlockSpec((BLK, kw), a_next),
            pl.BlockSpec((BLK, vw), b_prev), pl.BlockSpec((BLK, vw), b_cur), pl.BlockSpec((BLK, vw), b_next),
            pl.BlockSpec((BLK, D_ATTN), b_cur),
        ],
        out_specs=pl.BlockSpec((BLK, D_ATTN), b_cur),
        out_shape=jax.ShapeDtypeStruct((n, D_ATTN), BF16),
        scratch_shapes=[pltpu.VMEM((3, ATTN_HEADS, 3 * BLK, BLK), F32),
                        pltpu.VMEM((2, ATTN_HEADS // 2, 3 * BLK, 2 * BLK), F32),
                        pltpu.VMEM((2, ATTN_HEADS // 2, 1, 2 * BLK), F32),
                        pltpu.VMEM((D_ATTN, BLK), F32)],
        compiler_params=pltpu.CompilerParams(dimension_semantics=("arbitrary", "arbitrary"),
                                             vmem_limit_bytes=VMEM_LIMIT),
        name="attn",
    )(rel_bias, bucket_t, sink_x, attn_norm_g, q, kd, kd, kd, v, v, v, ga)


def _outproj_kernel(x_ref, ys_ref, ya_ref, w_ref, g_ref, o_ref):
    acc = (jnp.dot(ys_ref[...], w_ref[0:D_SSM, :], preferred_element_type=F32)
           + jnp.dot(ya_ref[...], w_ref[D_SSM:D_MIX, :], preferred_element_type=F32))
    hres = x_ref[...] + acc
    ms = jnp.mean(hres * hres, axis=-1, keepdims=True)
    o_ref[...] = hres * lax.rsqrt(ms + EPS) * g_ref[...]


def _outproj(x2, y_ssd, y_attn, w_out, final_norm_g):
    n = x2.shape[0]
    row = lambda i: (i, 0)
    const = lambda i: (0, 0)
    return pl.pallas_call(
        _outproj_kernel,
        grid=(n // TM,),
        in_specs=[
            pl.BlockSpec((TM, D_MODEL), row),
            pl.BlockSpec((TM, D_SSM), row),
            pl.BlockSpec((TM, D_ATTN), row),
            pl.BlockSpec((D_MIX, D_MODEL), const, pipeline_mode=pl.Buffered(1)),
            pl.BlockSpec((1, D_MODEL), const),
        ],
        out_specs=pl.BlockSpec((TM, D_MODEL), row),
        out_shape=jax.ShapeDtypeStruct((n, D_MODEL), F32),
        compiler_params=pltpu.CompilerParams(dimension_semantics=("parallel",),
                                             vmem_limit_bytes=VMEM_LIMIT),
        name="outproj",
    )(x2, y_ssd, y_attn, w_out, final_norm_g)


def _arrange_w_in(w_in):
    sizes = [D_SSM, CONV_CH, 2 * SSM_HEADS, D_ATTN, KV_HEADS * ATTN_HEAD_DIM,
             KV_HEADS * ATTN_HEAD_DIM, D_ATTN]
    wz, wxbc, wdt, wq, wk, wv, wga = jnp.split(w_in, [int(s) for s in np.cumsum(sizes)[:-1]], axis=1)
    wk = wk.reshape(D_MODEL, KV_HEADS, 1, ATTN_HEAD_DIM)
    wkd = jnp.broadcast_to(wk, (D_MODEL, KV_HEADS, 2, ATTN_HEAD_DIM)).reshape(D_MODEL, -1)
    wdt = jnp.pad(wdt, ((0, 0), (0, LANES - 2 * SSM_HEADS)))
    return jnp.concatenate([wz, wxbc, wq, wkd, wv, wga, wdt], axis=1).astype(BF16)


def _expand_matrices():
    e = np.zeros((Q, 2, 2 * LANES, D_SSM), np.float32)
    for c in range(Q):
        for d in range(2):
            for h in range(SSM_HEADS):
                r = (c * 2 + d) * SSM_HEADS + h
                e[c, d, r, h * SSM_HEAD_DIM:(h + 1) * SSM_HEAD_DIM] = 1.0
                e[c, d, LANES + r, h * SSM_HEAD_DIM:(h + 1) * SSM_HEAD_DIM] = 1.0
    return e


def kernel(x, norm_in_g, w_in, conv_w, conv_b, dt_bias, a_log, d_skip, ssd_norm_g, rel_bias, sink,
           attn_norm_g, w_out, final_norm_g):
    batch, seq, _ = x.shape
    assert w_out.shape[0] == 1 and seq % (Q * CHUNK) == 0 and seq % TM == 0
    x2 = x.reshape(batch * seq, D_MODEL)

    w_all = _arrange_w_in(w_in)
    e = jnp.asarray(_expand_matrices(), BF16)
    tril = jnp.asarray(np.tril(np.ones((CHUNK, CHUNK), np.float32)), BF16)
    a_log2 = jnp.tile(a_log.reshape(1, 2 * SSM_HEADS).astype(F32), (1, Q))
    dt_bias_x = jnp.pad(dt_bias.reshape(1, 2 * SSM_HEADS).astype(F32), ((0, 0), (0, LANES - 2 * SSM_HEADS)))
    dskip_x = jnp.repeat(d_skip.astype(F32), SSM_HEAD_DIM).reshape(1, D_SSM)
    sink_x = jnp.repeat(sink.astype(F32), BLK).reshape(ATTN_HEADS // 2, 2 * BLK)

    z, xs, bm, cm, dt, q, kd, v, ga = _inproj(
        x2, norm_in_g.reshape(1, D_MODEL), w_all, conv_w, conv_b.reshape(1, CONV_CH),
        dt_bias_x, seq)
    sf, sb = _states(a_log2, e, tril, xs, bm, dt, batch, seq)
    y_ssd = _ssd_out(a_log2, dskip_x, ssd_norm_g.reshape(1, D_SSM), e, tril, xs, bm, cm, dt, z, sf, sb)
    y_attn = _attn(rel_bias.astype(F32), _bucket_table_t(), sink_x,
                   attn_norm_g.reshape(1, D_ATTN), q, kd, v, ga, batch, seq)
    out = _outproj(x2, y_ssd, y_attn, w_out[0].astype(BF16), final_norm_g.reshape(1, D_MODEL))
    return out.reshape(batch, seq, D_MODEL)
```

```python
import math

import numpy as np
import jax
import jax.numpy as jnp
from jax import lax
from jax.experimental import pallas as pl
from jax.experimental.pallas import tpu as pltpu

F32 = jnp.float32
BF16 = jnp.bfloat16

D_MODEL = 1024
D_SSM = 1024
D_ATTN = 1024
D_MIX = D_SSM + D_ATTN
SSM_HEAD_DIM = 64
SSM_HEADS = D_SSM // SSM_HEAD_DIM
SSM_GROUPS = 2
GROUP_W = D_SSM // SSM_GROUPS
D_STATE = 128
D_CONV = 5
CHUNK = 128
CONV_CH = D_SSM + 2 * SSM_GROUPS * D_STATE
ATTN_HEAD_DIM = 64
ATTN_HEADS = D_ATTN // ATTN_HEAD_DIM
KV_HEADS = 4
WINDOW = 128
BLK = 128
NUM_BUCKETS = 32
MAX_DISTANCE = 128
MAX_EXACT = 8
EPS = 1e-6
NEG_INF = -1e30
SCALE = ATTN_HEAD_DIM ** -0.5
LOG2E = math.log2(math.e)

LANES = 128
HALO = 8
TM = 512
CONV_ROWS = 64
Q = 4
VMEM_LIMIT = 56 * 1024 * 1024

OFF_Z = 0
OFF_XBC = OFF_Z + D_SSM
OFF_Q = OFF_XBC + CONV_CH
OFF_KD = OFF_Q + D_ATTN
OFF_V = OFF_KD + 2 * KV_HEADS * ATTN_HEAD_DIM
OFF_GA = OFF_V + KV_HEADS * ATTN_HEAD_DIM
OFF_DT = OFF_GA + D_ATTN
W_COLS = OFF_DT + LANES
assert TM == Q * CHUNK and Q * 2 * SSM_HEADS == LANES


def _silu(v):
    return v * jax.nn.sigmoid(v)


def _softplus(v):
    return jnp.maximum(v, 0.0) + jnp.log1p(jnp.exp(-jnp.abs(v)))


def _inproj_kernel(tiles_per_seq, x_ref, g_ref, w_ref, cw_ref, cb_ref, dtb_ref,
                   z_ref, xs_ref, bm_ref, cm_ref, dt_ref, q_ref, kd_ref, v_ref, ga_ref, pad_ref):
    i = pl.program_id(0)

    @pl.when(i == 0)
    def _():
        pad_ref[...] = jnp.zeros_like(pad_ref)

    j = lax.rem(i, tiles_per_seq)
    g = g_ref[...]

    def proj(hv, off, n):
        return jnp.dot(hv, w_ref[:, off:off + n], preferred_element_type=F32)

    xv = x_ref[...]
    ms = jnp.mean(xv * xv, axis=-1, keepdims=True)
    h = (xv * lax.rsqrt(ms + EPS) * g).astype(BF16)

    s_cur = lax.rem(i, 3)
    s_prev = lax.rem(i + 2, 3)
    s_next = lax.rem(i + 1, 3)
    xbc = proj(h, OFF_XBC, CONV_CH)
    pad_ref[s_cur, HALO:HALO + TM, :] = xbc
    pad_ref[s_prev, HALO + TM:2 * HALO + TM, :] = xbc[0:HALO] * jnp.where(j > 0, 1.0, 0.0)
    pad_ref[s_next, 0:HALO, :] = xbc[TM - HALO:TM] * jnp.where(j < tiles_per_seq - 1, 1.0, 0.0)

    z_ref[...] = proj(h, OFF_Z, D_SSM).astype(BF16)
    q_ref[...] = (proj(h, OFF_Q, D_ATTN) * (SCALE * LOG2E)).astype(BF16)
    kd_ref[...] = proj(h, OFF_KD, 2 * KV_HEADS * ATTN_HEAD_DIM).astype(BF16)
    v_ref[...] = proj(h, OFF_V, KV_HEADS * ATTN_HEAD_DIM).astype(BF16)
    ga_ref[...] = proj(h, OFF_GA, D_ATTN).astype(BF16)
    dt_lane = lax.broadcasted_iota(jnp.int32, (TM, LANES), 1) < 2 * SSM_HEADS
    dt = jnp.where(dt_lane, _softplus(proj(h, OFF_DT, LANES) + dtb_ref[...]), 0.0)
    dtc = dt[0:CHUNK]
    for c in range(1, Q):
        dtc = dtc + pltpu.roll(dt[c * CHUNK:(c + 1) * CHUNK], 2 * SSM_HEADS * c, axis=1)
    dt_ref[...] = dtc

    base = HALO - (D_CONV - 1) // 2
    for cc in range(CONV_CH // LANES):
        sl = slice(cc * LANES, (cc + 1) * LANES)
        if cc < D_SSM // LANES:
            dst, o = xs_ref, cc * LANES
        elif cc < (D_SSM + SSM_GROUPS * D_STATE) // LANES:
            dst, o = bm_ref, cc * LANES - D_SSM
        else:
            dst, o = cm_ref, cc * LANES - D_SSM - SSM_GROUPS * D_STATE
        for r0 in range(0, TM, CONV_ROWS):
            acc = cb_ref[:, sl] + cw_ref[0:1, sl] * pad_ref[s_prev, base + r0:base + r0 + CONV_ROWS, sl]
            for k in range(1, D_CONV):
                acc = acc + cw_ref[k:k + 1, sl] * pad_ref[s_prev, base + r0 + k:base + r0 + k + CONV_ROWS, sl]
            dst[r0:r0 + CONV_ROWS, o:o + LANES] = _silu(acc).astype(BF16)


def _inproj(x2, norm_in_g, w_all, conv_w, conv_b, dt_bias, seq):
    n = x2.shape[0]
    tiles_per_seq = seq // TM
    ntile = n // TM
    cur = lambda i: (jnp.minimum(i, ntile - 1), 0)
    lag = lambda i: (jnp.maximum(i - 1, 0), 0)
    const = lambda i: (0, 0)
    out_cols = (D_SSM, D_SSM, SSM_GROUPS * D_STATE, SSM_GROUPS * D_STATE,
                D_ATTN, 2 * KV_HEADS * ATTN_HEAD_DIM, KV_HEADS * ATTN_HEAD_DIM, D_ATTN)
    out_maps = (cur, lag, lag, lag, cur, cur, cur, cur)
    out_specs = [pl.BlockSpec((TM, c), m) for c, m in zip(out_cols, out_maps)]
    out_shape = [jax.ShapeDtypeStruct((n, c), BF16) for c in out_cols]
    out_specs.insert(4, pl.BlockSpec((CHUNK, LANES), lambda i: (0, jnp.minimum(i, ntile - 1))))
    out_shape.insert(4, jax.ShapeDtypeStruct((CHUNK, ntile * LANES), F32))
    return pl.pallas_call(
        lambda *refs: _inproj_kernel(tiles_per_seq, *refs),
        grid=(ntile + 1,),
        in_specs=[
            pl.BlockSpec((TM, D_MODEL), cur),
            pl.BlockSpec((1, D_MODEL), const),
            pl.BlockSpec((D_MODEL, W_COLS), const, pipeline_mode=pl.Buffered(1)),
            pl.BlockSpec((D_CONV, CONV_CH), const),
            pl.BlockSpec((1, CONV_CH), const),
            pl.BlockSpec((1, LANES), const),
        ],
        out_specs=out_specs,
        out_shape=out_shape,
        scratch_shapes=[pltpu.VMEM((3, TM + 2 * HALO, CONV_CH), F32)],
        compiler_params=pltpu.CompilerParams(dimension_semantics=("arbitrary",),
                                             vmem_limit_bytes=VMEM_LIMIT),
        name="inproj",
    )(x2, norm_in_g, w_all, conv_w, conv_b, dt_bias)


def _split_terms(v, n):
    terms, r = [], v
    for _ in range(n):
        t = r.astype(BF16)
        terms.append(t)
        r = r - t.astype(F32)
    return terms


def _chunk_cumsum(tril, a):
    t0, t1, t2 = _split_terms(a, 3)
    return (jnp.dot(tril, t0, preferred_element_type=F32) + jnp.dot(tril, t1, preferred_element_type=F32)
            + jnp.dot(tril, t2, preferred_element_type=F32))


def _hi_lo(v):
    return jnp.concatenate(_split_terms(v, 2), axis=1)


def _fwd_lanes():
    lane = lax.broadcasted_iota(jnp.int32, (CHUNK, LANES), 1)
    return jnp.bitwise_and(lane, 2 * SSM_HEADS - 1) < SSM_HEADS


def _states_kernel(alog_ref, e_ref, tril_ref, xsf_ref, bf_ref, dtf_ref, xsb_ref, bb_ref, dtb_ref,
                   sf_ref, sb_ref, cf_ref, cb_ref):
    @pl.when(pl.program_id(1) == 0)
    def _():
        cf_ref[...] = jnp.zeros_like(cf_ref)
        cb_ref[...] = jnp.zeros_like(cb_ref)

    a_neg = -jnp.exp(alog_ref[...])
    tril = tril_ref[...]
    fwd_lane = _fwd_lanes()

    def prep(dt_ref):
        dt = dt_ref[...]
        a = dt * a_neg
        cs = _chunk_cumsum(tril, a)
        tot = cs[CHUNK - 1:CHUNK, :]
        w = dt * jnp.exp(jnp.where(fwd_lane, tot - cs, cs - a))
        return _hi_lo(jnp.concatenate([w, jnp.broadcast_to(jnp.exp(tot), (8, LANES))], axis=0))

    def chunk(c, d, cat, xs_ref, b_ref, out_ref, carry_ref):
        rows = slice(c * CHUNK, (c + 1) * CHUNK)
        wx = jnp.dot(cat, e_ref[c, d], preferred_element_type=F32)
        xd = (xs_ref[rows, :].astype(F32) * wx[0:CHUNK]).astype(BF16)
        dec = wx[CHUNK:CHUNK + 1, :]
        for g in range(SSM_GROUPS):
            gr = slice(g * D_STATE, (g + 1) * D_STATE)
            gc = slice(g * GROUP_W, (g + 1) * GROUP_W)
            st = lax.dot_general(b_ref[rows, gr], xd[:, gc], (((0,), (0,)), ((), ())),
                                 preferred_element_type=F32)
            prev = carry_ref[gr, :]
            out_ref[c, gr, :] = prev.astype(BF16)
            carry_ref[gr, :] = prev * dec[:, gc] + st

    cat_f = prep(dtf_ref)
    cat_b = prep(dtb_ref)
    for c in range(Q):
        chunk(c, 0, cat_f, xsf_ref, bf_ref, sf_ref, cf_ref)
    for c in reversed(range(Q)):
        chunk(c, 1, cat_b, xsb_ref, bb_ref, sb_ref, cb_ref)


def _states(a_log, e, tril, xs, bm, dt, batch, seq):
    nstep = seq // (Q * CHUNK)
    nchunk = batch * seq // CHUNK
    fwd = lambda b, i: (b * nstep + i, 0)
    bwd = lambda b, i: (b * nstep + nstep - 1 - i, 0)
    const = lambda b, i: (0, 0)
    rows = Q * CHUNK
    st_shape = jax.ShapeDtypeStruct((nchunk, SSM_GROUPS * D_STATE, GROUP_W), BF16)
    return pl.pallas_call(
        _states_kernel,
        grid=(batch, nstep),
        in_specs=[
            pl.BlockSpec((1, LANES), const),
            pl.BlockSpec((Q, 2, 2 * LANES, D_SSM), lambda b, i: (0, 0, 0, 0), pipeline_mode=pl.Buffered(1)),
            pl.BlockSpec((CHUNK, CHUNK), const),
            pl.BlockSpec((rows, D_SSM), fwd),
            pl.BlockSpec((rows, SSM_GROUPS * D_STATE), fwd),
            pl.BlockSpec((CHUNK, LANES), lambda b, i: (0, b * nstep + i)),
            pl.BlockSpec((rows, D_SSM), bwd),
            pl.BlockSpec((rows, SSM_GROUPS * D_STATE), bwd),
            pl.BlockSpec((CHUNK, LANES), lambda b, i: (0, b * nstep + nstep - 1 - i)),
        ],
        out_specs=[
            pl.BlockSpec((Q, SSM_GROUPS * D_STATE, GROUP_W), lambda b, i: (b * nstep + i, 0, 0)),
            pl.BlockSpec((Q, SSM_GROUPS * D_STATE, GROUP_W), lambda b, i: (b * nstep + nstep - 1 - i, 0, 0)),
        ],
        out_shape=[st_shape, st_shape],
        scratch_shapes=[pltpu.VMEM((SSM_GROUPS * D_STATE, GROUP_W), F32),
                        pltpu.VMEM((SSM_GROUPS * D_STATE, GROUP_W), F32)],
        compiler_params=pltpu.CompilerParams(dimension_semantics=("arbitrary", "arbitrary"),
                                             vmem_limit_bytes=VMEM_LIMIT),
        name="ssd_states",
    )(a_log, e, tril, xs, bm, dt, xs, bm, dt)


def _ssd_out_kernel(alog_ref, dskip_ref, gn_ref, e_ref, tril_ref, xs_ref, bm_ref, cm_ref, dt_ref,
                    z_ref, sf_ref, sb_ref, y_ref):
    li = lax.broadcasted_iota(jnp.int32, (CHUNK, CHUNK), 0)
    si = lax.broadcasted_iota(jnp.int32, (CHUNK, CHUNK), 1)
    lower = li >= si
    strict_lower = li > si
    strict_upper = li < si
    left = lax.broadcasted_iota(jnp.int32, (CHUNK, LANES), 1) < SSM_HEAD_DIM

    dt = dt_ref[...]
    a = dt * -jnp.exp(alog_ref[...])
    cs = _chunk_cumsum(tril_ref[...], a)
    tot = cs[CHUNK - 1:CHUNK, :]
    colq = jnp.where(_fwd_lanes(), cs, tot - cs + a)
    rowq = colq.T
    dtr = dt.T
    cat = _hi_lo(jnp.exp(colq))

    for c in range(Q):
        rows = slice(c * CHUNK, (c + 1) * CHUNK)
        ef = jnp.dot(cat, e_ref[c, 0], preferred_element_type=F32)
        eb = jnp.dot(cat, e_ref[c, 1], preferred_element_type=F32)
        for g in range(SSM_GROUPS):
            gr = slice(g * D_STATE, (g + 1) * D_STATE)
            gc = slice(g * GROUP_W, (g + 1) * GROUP_W)
            cmg = cm_ref[rows, gr]
            cb = lax.dot_general(cmg, bm_ref[rows, gr], (((1,), (1,)), ((), ())),
                                 preferred_element_type=F32)
            y_off = (jnp.dot(cmg, sf_ref[c, gr, :], preferred_element_type=F32) * ef[:, gc]
                     + jnp.dot(cmg, sb_ref[c, gr, :], preferred_element_type=F32) * eb[:, gc])
            xg = xs_ref[rows, gc]
            yg = y_off + dskip_ref[:, gc] * xg.astype(F32)
            pairs = []
            for pp in range(GROUP_W // LANES):
                scores = []
                for hh in range(2):
                    h = c * 2 * SSM_HEADS + g * (SSM_HEADS // SSM_GROUPS) + pp * 2 + hh
                    hb = h + SSM_HEADS
                    arg = jnp.where(lower, colq[:, h:h + 1] - rowq[h:h + 1, :],
                                    colq[:, hb:hb + 1] - rowq[hb:hb + 1, :])
                    wrow = jnp.where(strict_lower, dtr[h:h + 1, :],
                                     jnp.where(strict_upper, dtr[hb:hb + 1, :],
                                               dtr[h:h + 1, :] + dtr[hb:hb + 1, :]))
                    scores.append((cb * jnp.exp(arg) * wrow).astype(BF16))
                xt = xg[:, pp * LANES:(pp + 1) * LANES]
                zero = jnp.zeros_like(xt)
                rhs = jnp.concatenate([jnp.where(left, xt, zero), jnp.where(left, zero, xt)], axis=0)
                pairs.append(jnp.dot(jnp.concatenate(scores, axis=1), rhs, preferred_element_type=F32))
            yg = yg + jnp.concatenate(pairs, axis=1)
            vg = yg * _silu(z_ref[rows, gc].astype(F32))
            ms = jnp.mean(vg * vg, axis=-1, keepdims=True)
            y_ref[rows, gc] = (vg * lax.rsqrt(ms + EPS) * gn_ref[:, gc]).astype(BF16)


def _ssd_out(a_log, dskip_x, ssd_norm_g, e, tril, xs, bm, cm, dt, z, sf, sb):
    n = xs.shape[0]
    rows = Q * CHUNK
    row = lambda i: (i, 0)
    const = lambda i: (0, 0)
    return pl.pallas_call(
        _ssd_out_kernel,
        grid=(n // rows,),
        in_specs=[
            pl.BlockSpec((1, LANES), const),
            pl.BlockSpec((1, D_SSM), const),
            pl.BlockSpec((1, D_SSM), const),
            pl.BlockSpec((Q, 2, 2 * LANES, D_SSM), lambda i: (0, 0, 0, 0), pipeline_mode=pl.Buffered(1)),
            pl.BlockSpec((CHUNK, CHUNK), const),
            pl.BlockSpec((rows, D_SSM), row),
            pl.BlockSpec((rows, SSM_GROUPS * D_STATE), row),
            pl.BlockSpec((rows, SSM_GROUPS * D_STATE), row),
            pl.BlockSpec((CHUNK, LANES), lambda i: (0, i)),
            pl.BlockSpec((rows, D_SSM), row),
            pl.BlockSpec((Q, SSM_GROUPS * D_STATE, GROUP_W), lambda i: (i, 0, 0)),
            pl.BlockSpec((Q, SSM_GROUPS * D_STATE, GROUP_W), lambda i: (i, 0, 0)),
        ],
        out_specs=pl.BlockSpec((rows, D_SSM), row),
        out_shape=jax.ShapeDtypeStruct((n, D_SSM), BF16),
        compiler_params=pltpu.CompilerParams(dimension_semantics=("parallel",),
                                             vmem_limit_bytes=VMEM_LIMIT),
        name="ssd_out",
    )(a_log, dskip_x, ssd_norm_g, e, tril, xs, bm, cm, dt, z, sf, sb)


def _bucket_table_t():
    rel = jnp.arange(3 * BLK)[:, None] - BLK - jnp.arange(BLK)[None, :]
    half = NUM_BUCKETS // 2
    ret = (rel > 0).astype(jnp.int32) * half
    n = jnp.abs(rel)
    nf = jnp.maximum(n, 1).astype(F32)
    large = MAX_EXACT + (jnp.log(nf / MAX_EXACT) / math.log(MAX_DISTANCE / MAX_EXACT)
                         * (half - MAX_EXACT)).astype(jnp.int32)
    large = jnp.minimum(large, half - 1)
    return (ret + jnp.where(n < MAX_EXACT, n, large)).astype(jnp.int32)


def _attn_kernel(nblk, ntot, rb_ref, bucket_ref, sink_ref, gn_ref, q_ref, kp_ref, kc_ref, kn_ref,
                 vp_ref, vc_ref, vn_ref, ga_ref, y_ref,
                 tab_ref, pe_ref, mk_ref, ot_ref, vs_ref):
    g = pl.program_id(0)

    @pl.when(g == 0)
    def _init():
        pe_ref[...] = jnp.zeros_like(pe_ref)
        mk_ref[...] = jnp.zeros_like(mk_ref)
        ot_ref[...] = jnp.zeros_like(ot_ref)
        bk = bucket_ref[...]
        t = lax.broadcasted_iota(jnp.int32, (3 * BLK, BLK), 0)
        qq = lax.broadcasted_iota(jnp.int32, (3 * BLK, BLK), 1)
        in_window = jnp.abs(t - BLK - qq) <= WINDOW

        def per_head(h, carry):
            acc = jnp.zeros((3 * BLK, BLK), F32)
            for b in range(NUM_BUCKETS):
                acc = jnp.where(bk == b, rb_ref[b, h], acc)
            tab = jnp.where(in_window, acc * LOG2E, NEG_INF)
            tab_ref[0, h] = tab
            tab_ref[1, h] = jnp.where(t >= BLK, tab, NEG_INF)
            tab_ref[2, h] = jnp.where(t < 2 * BLK, tab, NEG_INF)
            return carry

        lax.fori_loop(0, ATTN_HEADS, per_head, 0)

    left = lax.broadcasted_iota(jnp.int32, (BLK, LANES), 1) < ATTN_HEAD_DIM
    left3 = lax.broadcasted_iota(jnp.int32, (3 * BLK, LANES), 1) < ATTN_HEAD_DIM
    s2 = sink_ref[...] * LOG2E
    def run(even, odd):
        ia = lax.rem(jnp.minimum(g, ntot - 1), nblk)
        var = jnp.where(ia == 0, 1, jnp.where(ia == nblk - 1, 2, 0))
        kcat = jnp.concatenate([kp_ref[...], kc_ref[...], kn_ref[...]], axis=0)
        vcat = jnp.concatenate([vp_ref[...], vc_ref[...], vn_ref[...]], axis=0)
        one = jnp.ones((3 * BLK, LANES), BF16)

        def stage_a(p):
            kd = kcat[:, (p // 2) * LANES:(p // 2 + 1) * LANES]
            qt = q_ref[:, p * LANES:(p + 1) * LANES]
            zero = jnp.zeros_like(qt)
            rhs_t = jnp.concatenate([jnp.where(left, qt, zero), jnp.where(left, zero, qt)], axis=0)
            lg = lax.dot_general(kd, rhs_t, (((1,), (1,)), ((), ())), preferred_element_type=F32)
            lg = lg + jnp.concatenate([tab_ref[var, 2 * p], tab_ref[var, 2 * p + 1]], axis=1)
            for kb in range(3):
                blk = lg[kb * BLK:(kb + 1) * BLK]
                mk = jnp.max(blk, axis=0, keepdims=True)
                pe_ref[even, p, kb * BLK:(kb + 1) * BLK, :] = jnp.exp2(blk - mk).astype(BF16)
                mk_ref[even, p, kb:kb + 1, :] = mk

        def stage_b(p):
            kv = p // 2
            vt = vcat[:, (kv // 2) * LANES:(kv // 2 + 1) * LANES]
            vmod = jnp.where(left3, vt, one) if kv % 2 == 0 else jnp.where(left3, one, vt)
            r0 = (kv % 2) * ATTN_HEAD_DIM
            d0 = ATTN_HEAD_DIM - r0
            mk = mk_ref[odd, p]
            sp = s2[p:p + 1, :]
            m = jnp.maximum(jnp.max(mk[0:3], axis=0, keepdims=True), sp)
            o2 = None
            for kb in range(3):
                part = lax.dot_general(vmod[kb * BLK:(kb + 1) * BLK], pe_ref[odd, p, kb * BLK:(kb + 1) * BLK, :],
                                       (((0,), (0,)), ((), ())), preferred_element_type=F32)
                part = part * jnp.exp2(mk[kb:kb + 1] - m)
                o2 = part if o2 is None else o2 + part
            o = o2[r0:r0 + ATTN_HEAD_DIM, :] / (o2[d0:d0 + 1, :] + jnp.exp2(sp - m))
            ot_ref[odd, 2 * p * ATTN_HEAD_DIM:(2 * p + 1) * ATTN_HEAD_DIM, :] = o[:, 0:BLK]
            ot_ref[odd, (2 * p + 1) * ATTN_HEAD_DIM:(2 * p + 2) * ATTN_HEAD_DIM, :] = o[:, BLK:2 * BLK]

        def stage_c(p, ssq):
            cols = slice(p * LANES, (p + 1) * LANES)
            v = ot_ref[even, p * LANES:(p + 1) * LANES, :].T * _silu(ga_ref[:, cols].astype(F32))
            vs_ref[:, cols] = v
            return ssq + jnp.sum(v * v, axis=-1, keepdims=True)

        ssq = jnp.zeros((BLK, 1), F32)
        for p in range(ATTN_HEADS // 2):
            stage_a(p)
            stage_b(p)
            ssq = stage_c(p, ssq)
        y_ref[...] = (vs_ref[...] * lax.rsqrt(ssq * (1.0 / D_ATTN) + EPS) * gn_ref[...]).astype(BF16)

    @pl.when(lax.rem(g, 2) == 0)
    def _():
        run(0, 1)

    @pl.when(lax.rem(g, 2) == 1)
    def _():
        run(1, 0)


def _attn(rel_bias, bucket_t, sink_x, attn_norm_g, q, kd, v, ga, batch, seq):
    n = q.shape[0]
    nblk = seq // BLK
    ntot = batch * nblk

    def blk(lag):
        return lambda g: jnp.clip(g - lag, 0, ntot - 1)

    def cur(lag):
        return lambda g: (blk(lag)(g), 0)

    def prev(lag):
        return lambda g: (blk(lag)(g) - jnp.where(lax.rem(blk(lag)(g), nblk) > 0, 1, 0), 0)

    def nxt(lag):
        return lambda g: (blk(lag)(g) + jnp.where(lax.rem(blk(lag)(g), nblk) < nblk - 1, 1, 0), 0)

    const = lambda g: (0, 0)
    kw = 2 * KV_HEADS * ATTN_HEAD_DIM
    vw = KV_HEADS * ATTN_HEAD_DIM
    npair = ATTN_HEADS // 2
    return pl.pallas_call(
        lambda *refs: _attn_kernel(nblk, ntot, *refs),
        grid=(ntot + 2,),
        in_specs=[
            pl.BlockSpec(memory_space=pltpu.SMEM),
            pl.BlockSpec((3 * BLK, BLK), const),
            pl.BlockSpec((npair, 2 * BLK), const),
            pl.BlockSpec((1, D_ATTN), const),
            pl.BlockSpec((BLK, D_ATTN), cur(0)),
            pl.BlockSpec((BLK, kw), prev(0)), pl.BlockSpec((BLK, kw), cur(0)), pl.BlockSpec((BLK, kw), nxt(0)),
            pl.BlockSpec((BLK, vw), prev(1)), pl.BlockSpec((BLK, vw), cur(1)), pl.BlockSpec((BLK, vw), nxt(1)),
            pl.BlockSpec((BLK, D_ATTN), cur(2)),
        ],
        out_specs=pl.BlockSpec((BLK, D_ATTN), cur(2)),
        out_shape=jax.ShapeDtypeStruct((n, D_ATTN), BF16),
        scratch_shapes=[pltpu.VMEM((3, ATTN_HEADS, 3 * BLK, BLK), F32),
                        pltpu.VMEM((2, npair, 3 * BLK, 2 * BLK), BF16),
                        pltpu.VMEM((2, npair, 8, 2 * BLK), F32),
                        pltpu.VMEM((2, D_ATTN, BLK), F32),
                        pltpu.VMEM((BLK, D_ATTN), F32)],

        compiler_params=pltpu.CompilerParams(dimension_semantics=("arbitrary",),
                                             vmem_limit_bytes=VMEM_LIMIT),
        name="attn",
    )(rel_bias, bucket_t, sink_x, attn_norm_g, q, kd, kd, kd, v, v, v, ga)


def _outproj_kernel(x_ref, ys_ref, ya_ref, w_ref, g_ref, o_ref):
    acc = (jnp.dot(ys_ref[...], w_ref[0:D_SSM, :], preferred_element_type=F32)
           + jnp.dot(ya_ref[...], w_ref[D_SSM:D_MIX, :], preferred_element_type=F32))
    hres = x_ref[...] + acc
    ms = jnp.mean(hres * hres, axis=-1, keepdims=True)
    o_ref[...] = hres * lax.rsqrt(ms + EPS) * g_ref[...]


def _outproj(x2, y_ssd, y_attn, w_out, final_norm_g):
    n = x2.shape[0]
    row = lambda i: (i, 0)
    const = lambda i: (0, 0)
    return pl.pallas_call(
        _outproj_kernel,
        grid=(n // TM,),
        in_specs=[
            pl.BlockSpec((TM, D_MODEL), row),
            pl.BlockSpec((TM, D_SSM), row),
            pl.BlockSpec((TM, D_ATTN), row),
            pl.BlockSpec((D_MIX, D_MODEL), const, pipeline_mode=pl.Buffered(1)),
            pl.BlockSpec((1, D_MODEL), const),
        ],
        out_specs=pl.BlockSpec((TM, D_MODEL), row),
        out_shape=jax.ShapeDtypeStruct((n, D_MODEL), F32),
        compiler_params=pltpu.CompilerParams(dimension_semantics=("parallel",),
                                             vmem_limit_bytes=VMEM_LIMIT),
        name="outproj",
    )(x2, y_ssd, y_attn, w_out, final_norm_g)


def _arrange_w_in(w_in):
    sizes = [D_SSM, CONV_CH, 2 * SSM_HEADS, D_ATTN, KV_HEADS * ATTN_HEAD_DIM,
             KV_HEADS * ATTN_HEAD_DIM, D_ATTN]
    wz, wxbc, wdt, wq, wk, wv, wga = jnp.split(w_in, [int(s) for s in np.cumsum(sizes)[:-1]], axis=1)
    wk = wk.reshape(D_MODEL, KV_HEADS, 1, ATTN_HEAD_DIM)
    wkd = jnp.broadcast_to(wk, (D_MODEL, KV_HEADS, 2, ATTN_HEAD_DIM)).reshape(D_MODEL, -1)
    wdt = jnp.pad(wdt, ((0, 0), (0, LANES - 2 * SSM_HEADS)))
    return jnp.concatenate([wz, wxbc, wq, wkd, wv, wga, wdt], axis=1).astype(BF16)


def _expand_matrices():
    e = np.zeros((Q, 2, 2 * LANES, D_SSM), np.float32)
    for c in range(Q):
        for d in range(2):
            for h in range(SSM_HEADS):
                r = (c * 2 + d) * SSM_HEADS + h
                e[c, d, r, h * SSM_HEAD_DIM:(h + 1) * SSM_HEAD_DIM] = 1.0
                e[c, d, LANES + r, h * SSM_HEAD_DIM:(h + 1) * SSM_HEAD_DIM] = 1.0
    return e


def kernel(x, norm_in_g, w_in, conv_w, conv_b, dt_bias, a_log, d_skip, ssd_norm_g, rel_bias, sink,
           attn_norm_g, w_out, final_norm_g):
    batch, seq, _ = x.shape
    assert w_out.shape[0] == 1 and seq % (Q * CHUNK) == 0 and seq % TM == 0
    x2 = x.reshape(batch * seq, D_MODEL)

    w_all = _arrange_w_in(w_in)
    e = jnp.asarray(_expand_matrices(), BF16)
    tril = jnp.asarray(np.tril(np.ones((CHUNK, CHUNK), np.float32)), BF16)
    a_log2 = jnp.tile(a_log.reshape(1, 2 * SSM_HEADS).astype(F32), (1, Q))
    dt_bias_x = jnp.pad(dt_bias.reshape(1, 2 * SSM_HEADS).astype(F32), ((0, 0), (0, LANES - 2 * SSM_HEADS)))
    dskip_x = jnp.repeat(d_skip.astype(F32), SSM_HEAD_DIM).reshape(1, D_SSM)
    sink_x = jnp.repeat(sink.astype(F32), BLK).reshape(ATTN_HEADS // 2, 2 * BLK)

    z, xs, bm, cm, dt, q, kd, v, ga = _inproj(
        x2, norm_in_g.reshape(1, D_MODEL), w_all, conv_w, conv_b.reshape(1, CONV_CH),
        dt_bias_x, seq)
    sf, sb = _states(a_log2, e, tril, xs, bm, dt, batch, seq)
    y_ssd = _ssd_out(a_log2, dskip_x, ssd_norm_g.reshape(1, D_SSM), e, tril, xs, bm, cm, dt, z, sf, sb)
    y_attn = _attn(rel_bias.astype(F32), _bucket_table_t(), sink_x,
                   attn_norm_g.reshape(1, D_ATTN), q, kd, v, ga, batch, seq)
    out = _outproj(x2, y_ssd, y_attn, w_out[0].astype(BF16), final_norm_g.reshape(1, D_MODEL))
    return out.reshape(batch, seq, D_MODEL)
```

```python
import math

import numpy as np
import jax
import jax.numpy as jnp
from jax import lax
from jax.experimental import pallas as pl
from jax.experimental.pallas import tpu as pltpu

F32 = jnp.float32
BF16 = jnp.bfloat16

D_MODEL = 1024
D_SSM = 1024
D_ATTN = 1024
D_MIX = D_SSM + D_ATTN
SSM_HEAD_DIM = 64
SSM_HEADS = D_SSM // SSM_HEAD_DIM
SSM_GROUPS = 2
GROUP_W = D_SSM // SSM_GROUPS
D_STATE = 128
D_CONV = 5
CHUNK = 128
CONV_CH = D_SSM + 2 * SSM_GROUPS * D_STATE
ATTN_HEAD_DIM = 64
ATTN_HEADS = D_ATTN // ATTN_HEAD_DIM
KV_HEADS = 4
WINDOW = 128
BLK = 128
NUM_BUCKETS = 32
MAX_DISTANCE = 128
MAX_EXACT = 8
EPS = 1e-6
NEG_INF = -1e30
SCALE = ATTN_HEAD_DIM ** -0.5
LOG2E = math.log2(math.e)

LANES = 128
HALO = 8
TM = 512
CONV_ROWS = 64
Q = 4
VMEM_LIMIT = 56 * 1024 * 1024

OFF_Z = 0
OFF_XBC = OFF_Z + D_SSM
OFF_Q = OFF_XBC + CONV_CH
OFF_KD = OFF_Q + D_ATTN
OFF_V = OFF_KD + 2 * KV_HEADS * ATTN_HEAD_DIM
OFF_GA = OFF_V + KV_HEADS * ATTN_HEAD_DIM
OFF_DT = OFF_GA + D_ATTN
W_COLS = OFF_DT + LANES
assert TM == Q * CHUNK and Q * 2 * SSM_HEADS == LANES


def _silu(v):
    return v * jax.nn.sigmoid(v)


def _softplus(v):
    return jnp.maximum(v, 0.0) + jnp.log1p(jnp.exp(-jnp.abs(v)))


def _inproj_kernel(tiles_per_seq, x_ref, g_ref, w_ref, cw_ref, cb_ref, dtb_ref,
                   z_ref, xs_ref, bm_ref, cm_ref, dt_ref, q_ref, kd_ref, v_ref, ga_ref, pad_ref):
    i = pl.program_id(0)

    @pl.when(i == 0)
    def _():
        pad_ref[...] = jnp.zeros_like(pad_ref)

    j = lax.rem(i, tiles_per_seq)
    g = g_ref[...]

    def proj(hv, off, n):
        return jnp.dot(hv, w_ref[:, off:off + n], preferred_element_type=F32)

    xv = x_ref[...]
    ms = jnp.mean(xv * xv, axis=-1, keepdims=True)
    h = (xv * lax.rsqrt(ms + EPS) * g).astype(BF16)

    s_cur = lax.rem(i, 3)
    s_prev = lax.rem(i + 2, 3)
    s_next = lax.rem(i + 1, 3)
    xbc = proj(h, OFF_XBC, CONV_CH)
    pad_ref[s_cur, HALO:HALO + TM, :] = xbc
    pad_ref[s_prev, HALO + TM:2 * HALO + TM, :] = xbc[0:HALO] * jnp.where(j > 0, 1.0, 0.0)
    pad_ref[s_next, 0:HALO, :] = xbc[TM - HALO:TM] * jnp.where(j < tiles_per_seq - 1, 1.0, 0.0)

    z_ref[...] = _silu(proj(h, OFF_Z, D_SSM)).astype(BF16)
    q_ref[...] = (proj(h, OFF_Q, D_ATTN) * (SCALE * LOG2E)).astype(BF16)
    kd_ref[...] = proj(h, OFF_KD, 2 * KV_HEADS * ATTN_HEAD_DIM).astype(BF16)
    v_ref[...] = proj(h, OFF_V, KV_HEADS * ATTN_HEAD_DIM).astype(BF16)
    ga_ref[...] = _silu(proj(h, OFF_GA, D_ATTN)).astype(BF16)
    dt_lane = lax.broadcasted_iota(jnp.int32, (TM, LANES), 1) < 2 * SSM_HEADS
    dt = jnp.where(dt_lane, _softplus(proj(h, OFF_DT, LANES) + dtb_ref[...]), 0.0)
    dtc = dt[0:CHUNK]
    for c in range(1, Q):
        dtc = dtc + pltpu.roll(dt[c * CHUNK:(c + 1) * CHUNK], 2 * SSM_HEADS * c, axis=1)
    dt_ref[...] = dtc

    base = HALO - (D_CONV - 1) // 2
    for cc in range(CONV_CH // LANES):
        sl = slice(cc * LANES, (cc + 1) * LANES)
        if cc < D_SSM // LANES:
            dst, o = xs_ref, cc * LANES
        elif cc < (D_SSM + SSM_GROUPS * D_STATE) // LANES:
            dst, o = bm_ref, cc * LANES - D_SSM
        else:
            dst, o = cm_ref, cc * LANES - D_SSM - SSM_GROUPS * D_STATE
        for r0 in range(0, TM, CONV_ROWS):
            acc = cb_ref[:, sl] + cw_ref[0:1, sl] * pad_ref[s_prev, base + r0:base + r0 + CONV_ROWS, sl]
            for k in range(1, D_CONV):
                acc = acc + cw_ref[k:k + 1, sl] * pad_ref[s_prev, base + r0 + k:base + r0 + k + CONV_ROWS, sl]
            dst[r0:r0 + CONV_ROWS, o:o + LANES] = _silu(acc).astype(BF16)


def _inproj(x2, norm_in_g, w_all, conv_w, conv_b, dt_bias, seq):
    n = x2.shape[0]
    tiles_per_seq = seq // TM
    ntile = n // TM
    cur = lambda i: (jnp.minimum(i, ntile - 1), 0)
    lag = lambda i: (jnp.maximum(i - 1, 0), 0)
    const = lambda i: (0, 0)
    out_cols = (D_SSM, D_SSM, SSM_GROUPS * D_STATE, SSM_GROUPS * D_STATE,
                D_ATTN, 2 * KV_HEADS * ATTN_HEAD_DIM, KV_HEADS * ATTN_HEAD_DIM, D_ATTN)
    out_maps = (cur, lag, lag, lag, cur, cur, cur, cur)
    out_specs = [pl.BlockSpec((TM, c), m) for c, m in zip(out_cols, out_maps)]
    out_shape = [jax.ShapeDtypeStruct((n, c), BF16) for c in out_cols]
    out_specs.insert(4, pl.BlockSpec((CHUNK, LANES), lambda i: (0, jnp.minimum(i, ntile - 1))))
    out_shape.insert(4, jax.ShapeDtypeStruct((CHUNK, ntile * LANES), F32))
    return pl.pallas_call(
        lambda *refs: _inproj_kernel(tiles_per_seq, *refs),
        grid=(ntile + 1,),
        in_specs=[
            pl.BlockSpec((TM, D_MODEL), cur),
            pl.BlockSpec((1, D_MODEL), const),
            pl.BlockSpec((D_MODEL, W_COLS), const, pipeline_mode=pl.Buffered(1)),
            pl.BlockSpec((D_CONV, CONV_CH), const),
            pl.BlockSpec((1, CONV_CH), const),
            pl.BlockSpec((1, LANES), const),
        ],
        out_specs=out_specs,
        out_shape=out_shape,
        scratch_shapes=[pltpu.VMEM((3, TM + 2 * HALO, CONV_CH), F32)],
        compiler_params=pltpu.CompilerParams(dimension_semantics=("arbitrary",),
                                             vmem_limit_bytes=VMEM_LIMIT),
        name="inproj",
    )(x2, norm_in_g, w_all, conv_w, conv_b, dt_bias)


def _split_terms(v, n):
    terms, r = [], v
    for _ in range(n):
        t = r.astype(BF16)
        terms.append(t)
        r = r - t.astype(F32)
    return terms


def _chunk_cumsum(tril, a):
    t0, t1, t2 = _split_terms(a, 3)
    return (jnp.dot(tril, t0, preferred_element_type=F32) + jnp.dot(tril, t1, preferred_element_type=F32)
            + jnp.dot(tril, t2, preferred_element_type=F32))


def _hi_lo(v):
    return jnp.concatenate(_split_terms(v, 2), axis=1)


def _fwd_lanes():
    lane = lax.broadcasted_iota(jnp.int32, (CHUNK, LANES), 1)
    return jnp.bitwise_and(lane, 2 * SSM_HEADS - 1) < SSM_HEADS


def _states_kernel(alog_ref, e_ref, tril_ref, xsf_ref, bf_ref, dtf_ref, xsb_ref, bb_ref, dtb_ref,
                   sf_ref, sb_ref, cf_ref, cb_ref):
    @pl.when(pl.program_id(1) == 0)
    def _():
        cf_ref[...] = jnp.zeros_like(cf_ref)
        cb_ref[...] = jnp.zeros_like(cb_ref)

    a_neg = -jnp.exp(alog_ref[...])
    tril = tril_ref[...]
    fwd_lane = _fwd_lanes()

    def prep(dt_ref):
        dt = dt_ref[...]
        a = dt * a_neg
        cs = _chunk_cumsum(tril, a)
        tot = cs[CHUNK - 1:CHUNK, :]
        w = dt * jnp.exp(jnp.where(fwd_lane, tot - cs, cs - a))
        return _hi_lo(jnp.concatenate([w, jnp.broadcast_to(jnp.exp(tot), (8, LANES))], axis=0))

    def chunk(c, d, cat, xs_ref, b_ref, out_ref, carry_ref):
        rows = slice(c * CHUNK, (c + 1) * CHUNK)
        wx = jnp.dot(cat, e_ref[c, d], preferred_element_type=F32)
        xd = (xs_ref[rows, :].astype(F32) * wx[0:CHUNK]).astype(BF16)
        dec = wx[CHUNK:CHUNK + 1, :]
        for g in range(SSM_GROUPS):
            gr = slice(g * D_STATE, (g + 1) * D_STATE)
            gc = slice(g * GROUP_W, (g + 1) * GROUP_W)
            st = lax.dot_general(b_ref[rows, gr], xd[:, gc], (((0,), (0,)), ((), ())),
                                 preferred_element_type=F32)
            prev = carry_ref[gr, :]
            out_ref[c, gr, :] = prev.astype(BF16)
            carry_ref[gr, :] = prev * dec[:, gc] + st

    cat_f = prep(dtf_ref)
    cat_b = prep(dtb_ref)
    for c in range(Q):
        chunk(c, 0, cat_f, xsf_ref, bf_ref, sf_ref, cf_ref)
    for c in reversed(range(Q)):
        chunk(c, 1, cat_b, xsb_ref, bb_ref, sb_ref, cb_ref)


def _states(a_log, e, tril, xs, bm, dt, batch, seq):
    nstep = seq // (Q * CHUNK)
    nchunk = batch * seq // CHUNK
    fwd = lambda b, i: (b * nstep + i, 0)
    bwd = lambda b, i: (b * nstep + nstep - 1 - i, 0)
    const = lambda b, i: (0, 0)
    rows = Q * CHUNK
    st_shape = jax.ShapeDtypeStruct((nchunk, SSM_GROUPS * D_STATE, GROUP_W), BF16)
    return pl.pallas_call(
        _states_kernel,
        grid=(batch, nstep),
        in_specs=[
            pl.BlockSpec((1, LANES), const),
            pl.BlockSpec((Q, 2, 2 * LANES, D_SSM), lambda b, i: (0, 0, 0, 0), pipeline_mode=pl.Buffered(1)),
            pl.BlockSpec((CHUNK, CHUNK), const),
            pl.BlockSpec((rows, D_SSM), fwd),
            pl.BlockSpec((rows, SSM_GROUPS * D_STATE), fwd),
            pl.BlockSpec((CHUNK, LANES), lambda b, i: (0, b * nstep + i)),
            pl.BlockSpec((rows, D_SSM), bwd),
            pl.BlockSpec((rows, SSM_GROUPS * D_STATE), bwd),
            pl.BlockSpec((CHUNK, LANES), lambda b, i: (0, b * nstep + nstep - 1 - i)),
        ],
        out_specs=[
            pl.BlockSpec((Q, SSM_GROUPS * D_STATE, GROUP_W), lambda b, i: (b * nstep + i, 0, 0)),
            pl.BlockSpec((Q, SSM_GROUPS * D_STATE, GROUP_W), lambda b, i: (b * nstep + nstep - 1 - i, 0, 0)),
        ],
        out_shape=[st_shape, st_shape],
        scratch_shapes=[pltpu.VMEM((SSM_GROUPS * D_STATE, GROUP_W), F32),
                        pltpu.VMEM((SSM_GROUPS * D_STATE, GROUP_W), F32)],
        compiler_params=pltpu.CompilerParams(dimension_semantics=("arbitrary", "arbitrary"),
                                             vmem_limit_bytes=VMEM_LIMIT),
        name="ssd_states",
    )(a_log, e, tril, xs, bm, dt, xs, bm, dt)


def _ssd_out_kernel(alog_ref, dskip_ref, gn_ref, e_ref, selb_ref, tril_ref, xs_ref, bm_ref, cm_ref, dt_ref,
                    z_ref, sf_ref, sb_ref, y_ref):
    li = lax.broadcasted_iota(jnp.int32, (CHUNK, CHUNK), 0)
    si = lax.broadcasted_iota(jnp.int32, (CHUNK, CHUNK), 1)
    lower = li >= si
    eye = li == si
    left = lax.broadcasted_iota(jnp.int32, (CHUNK, LANES), 1) < SSM_HEAD_DIM

    dt = dt_ref[...]
    a = dt * (-LOG2E * jnp.exp(alog_ref[...]))
    cs = _chunk_cumsum(tril_ref[...], a)
    tot = cs[CHUNK - 1:CHUNK, :]
    colq = jnp.where(_fwd_lanes(), cs, tot - cs + a)
    rowq = (colq - jnp.log2(dt)).T
    dtr = dt.T.astype(BF16)
    cat = _hi_lo(jnp.exp2(colq))
    catq = _hi_lo(colq)

    for c in range(Q):
        rows = slice(c * CHUNK, (c + 1) * CHUNK)
        ef = jnp.dot(cat, e_ref[c, 0], preferred_element_type=F32)
        eb = jnp.dot(cat, e_ref[c, 1], preferred_element_type=F32)
        colb = jnp.dot(catq, selb_ref[c], preferred_element_type=F32)
        for g in range(SSM_GROUPS):
            gr = slice(g * D_STATE, (g + 1) * D_STATE)
            gc = slice(g * GROUP_W, (g + 1) * GROUP_W)
            cmg = cm_ref[rows, gr]
            cb = lax.dot_general(cmg, bm_ref[rows, gr], (((1,), (1,)), ((), ())),
                                 preferred_element_type=F32)
            cbb = cb.astype(BF16)
            cbeye = jnp.where(eye, cb, 0.0).astype(BF16)
            y_off = (jnp.dot(cmg, sf_ref[c, gr, :], preferred_element_type=F32) * ef[:, gc]
                     + jnp.dot(cmg, sb_ref[c, gr, :], preferred_element_type=F32) * eb[:, gc])
            xg = xs_ref[rows, gc]
            yg = y_off + dskip_ref[:, gc] * xg.astype(F32)
            pairs = []
            for pp in range(GROUP_W // LANES):
                scores = []
                for hh in range(2):
                    hd = g * (SSM_HEADS // SSM_GROUPS) + pp * 2 + hh
                    h = c * 2 * SSM_HEADS + hd
                    hb = h + SSM_HEADS
                    arg = jnp.where(lower, colq[:, h:h + 1] - rowq[h:h + 1, :],
                                    colb[:, hd * CHUNK:(hd + 1) * CHUNK] - rowq[hb:hb + 1, :])
                    scores.append(cbb * jnp.exp2(arg).astype(BF16) + cbeye * dtr[hb:hb + 1, :])
                xt = xg[:, pp * LANES:(pp + 1) * LANES]
                zero = jnp.zeros_like(xt)
                rhs = jnp.concatenate([jnp.where(left, xt, zero), jnp.where(left, zero, xt)], axis=0)
                pairs.append(jnp.dot(jnp.concatenate(scores, axis=1), rhs, preferred_element_type=F32))
            yg = yg + jnp.concatenate(pairs, axis=1)
            vg = yg * z_ref[rows, gc].astype(F32)
            ms = jnp.mean(vg * vg, axis=-1, keepdims=True)
            y_ref[rows, gc] = (vg * lax.rsqrt(ms + EPS) * gn_ref[:, gc]).astype(BF16)


def _ssd_out(a_log, dskip_x, ssd_norm_g, e, selb, tril, xs, bm, cm, dt, z, sf, sb):
    n = xs.shape[0]
    rows = Q * CHUNK
    row = lambda i: (i, 0)
    const = lambda i: (0, 0)
    return pl.pallas_call(
        _ssd_out_kernel,
        grid=(n // rows,),
        in_specs=[
            pl.BlockSpec((1, LANES), const),
            pl.BlockSpec((1, D_SSM), const),
            pl.BlockSpec((1, D_SSM), const),
            pl.BlockSpec((Q, 2, 2 * LANES, D_SSM), lambda i: (0, 0, 0, 0), pipeline_mode=pl.Buffered(1)),
            pl.BlockSpec((Q, 2 * LANES, SSM_HEADS * CHUNK), lambda i: (0, 0, 0), pipeline_mode=pl.Buffered(1)),
            pl.BlockSpec((CHUNK, CHUNK), const),
            pl.BlockSpec((rows, D_SSM), row),
            pl.BlockSpec((rows, SSM_GROUPS * D_STATE), row),
            pl.BlockSpec((rows, SSM_GROUPS * D_STATE), row),
            pl.BlockSpec((CHUNK, LANES), lambda i: (0, i)),
            pl.BlockSpec((rows, D_SSM), row),
            pl.BlockSpec((Q, SSM_GROUPS * D_STATE, GROUP_W), lambda i: (i, 0, 0)),
            pl.BlockSpec((Q, SSM_GROUPS * D_STATE, GROUP_W), lambda i: (i, 0, 0)),
        ],
        out_specs=pl.BlockSpec((rows, D_SSM), row),
        out_shape=jax.ShapeDtypeStruct((n, D_SSM), BF16),
        compiler_params=pltpu.CompilerParams(dimension_semantics=("parallel",),
                                             vmem_limit_bytes=VMEM_LIMIT),
        name="ssd_out",
    )(a_log, dskip_x, ssd_norm_g, e, selb, tril, xs, bm, cm, dt, z, sf, sb)


def _bucket_table_t():
    rel = jnp.arange(3 * BLK)[:, None] - BLK - jnp.arange(BLK)[None, :]
    half = NUM_BUCKETS // 2
    ret = (rel > 0).astype(jnp.int32) * half
    n = jnp.abs(rel)
    nf = jnp.maximum(n, 1).astype(F32)
    large = MAX_EXACT + (jnp.log(nf / MAX_EXACT) / math.log(MAX_DISTANCE / MAX_EXACT)
                         * (half - MAX_EXACT)).astype(jnp.int32)
    large = jnp.minimum(large, half - 1)
    return (ret + jnp.where(n < MAX_EXACT, n, large)).astype(jnp.int32)


def _attn_kernel(nblk, ntot, rb_ref, bucket_ref, sink_ref, gn_ref, q_ref, kp_ref, kc_ref, kn_ref,
                 vp_ref, vc_ref, vn_ref, ga_ref, y_ref,
                 tab_ref, pe_ref, mk_ref, ot_ref, vs_ref):
    g = pl.program_id(0)

    @pl.when(g == 0)
    def _init():
        pe_ref[...] = jnp.zeros_like(pe_ref)
        mk_ref[...] = jnp.zeros_like(mk_ref)
        ot_ref[...] = jnp.zeros_like(ot_ref)
        bk = bucket_ref[...]
        t = lax.broadcasted_iota(jnp.int32, (3 * BLK, BLK), 0)
        qq = lax.broadcasted_iota(jnp.int32, (3 * BLK, BLK), 1)
        in_window = jnp.abs(t - BLK - qq) <= WINDOW

        def per_head(h, carry):
            acc = jnp.zeros((3 * BLK, BLK), F32)
            for b in range(NUM_BUCKETS):
                acc = jnp.where(bk == b, rb_ref[b, h], acc)
            tab = jnp.where(in_window, acc * LOG2E, NEG_INF)
            tab_ref[0, h] = tab
            tab_ref[1, h] = jnp.where(t >= BLK, tab, NEG_INF)
            tab_ref[2, h] = jnp.where(t < 2 * BLK, tab, NEG_INF)
            return carry

        lax.fori_loop(0, ATTN_HEADS, per_head, 0)

    left = lax.broadcasted_iota(jnp.int32, (BLK, LANES), 1) < ATTN_HEAD_DIM
    left3 = lax.broadcasted_iota(jnp.int32, (3 * BLK, LANES), 1) < ATTN_HEAD_DIM
    s2 = sink_ref[...] * LOG2E
    def run(even, odd):
        ia = lax.rem(jnp.minimum(g, ntot - 1), nblk)
        var = jnp.where(ia == 0, 1, jnp.where(ia == nblk - 1, 2, 0))
        kcat = jnp.concatenate([kp_ref[...], kc_ref[...], kn_ref[...]], axis=0)
        vcat = jnp.concatenate([vp_ref[...], vc_ref[...], vn_ref[...]], axis=0)
        one = jnp.ones((3 * BLK, LANES), BF16)

        def stage_a(p):
            kd = kcat[:, (p // 2) * LANES:(p // 2 + 1) * LANES]
            qt = q_ref[:, p * LANES:(p + 1) * LANES]
            zero = jnp.zeros_like(qt)
            rhs_t = jnp.concatenate([jnp.where(left, qt, zero), jnp.where(left, zero, qt)], axis=0)
            lg = lax.dot_general(kd, rhs_t, (((1,), (1,)), ((), ())), preferred_element_type=F32)
            lg = lg + jnp.concatenate([tab_ref[var, 2 * p], tab_ref[var, 2 * p + 1]], axis=1)
            for kb in range(3):
                blk = lg[kb * BLK:(kb + 1) * BLK]
                mk = jnp.max(blk, axis=0, keepdims=True)
                pe_ref[even, p, kb * BLK:(kb + 1) * BLK, :] = jnp.exp2(blk - mk).astype(BF16)
                mk_ref[even, p, kb:kb + 1, :] = mk

        def stage_b(p):
            kv = p // 2
            vt = vcat[:, (kv // 2) * LANES:(kv // 2 + 1) * LANES]
            vmod = jnp.where(left3, vt, one) if kv % 2 == 0 else jnp.where(left3, one, vt)
            r0 = (kv % 2) * ATTN_HEAD_DIM
            d0 = ATTN_HEAD_DIM - r0
            mk = mk_ref[odd, p]
            sp = s2[p:p + 1, :]
            m = jnp.maximum(jnp.max(mk[0:3], axis=0, keepdims=True), sp)
            o2 = None
            for kb in range(3):
                part = lax.dot_general(vmod[kb * BLK:(kb + 1) * BLK], pe_ref[odd, p, kb * BLK:(kb + 1) * BLK, :],
                                       (((0,), (0,)), ((), ())), preferred_element_type=F32)
                part = part * jnp.exp2(mk[kb:kb + 1] - m)
                o2 = part if o2 is None else o2 + part
            o = o2[r0:r0 + ATTN_HEAD_DIM, :] / (o2[d0:d0 + 1, :] + jnp.exp2(sp - m))
            ot_ref[odd, 2 * p * ATTN_HEAD_DIM:(2 * p + 1) * ATTN_HEAD_DIM, :] = o[:, 0:BLK]
            ot_ref[odd, (2 * p + 1) * ATTN_HEAD_DIM:(2 * p + 2) * ATTN_HEAD_DIM, :] = o[:, BLK:2 * BLK]

        def stage_c(p, ssq):
            cols = slice(p * LANES, (p + 1) * LANES)
            v = ot_ref[even, p * LANES:(p + 1) * LANES, :].T * ga_ref[:, cols].astype(F32)
            vs_ref[:, cols] = v
            return ssq + jnp.sum(v * v, axis=-1, keepdims=True)

        ssq = jnp.zeros((BLK, 1), F32)
        for p in range(ATTN_HEADS // 2):
            stage_a(p)
            stage_b(p)
            ssq = stage_c(p, ssq)
        y_ref[...] = (vs_ref[...] * lax.rsqrt(ssq * (1.0 / D_ATTN) + EPS) * gn_ref[...]).astype(BF16)

    @pl.when(lax.rem(g, 2) == 0)
    def _():
        run(0, 1)

    @pl.when(lax.rem(g, 2) == 1)
    def _():
        run(1, 0)


def _attn(rel_bias, bucket_t, sink_x, attn_norm_g, q, kd, v, ga, batch, seq):
    n = q.shape[0]
    nblk = seq // BLK
    ntot = batch * nblk

    def blk(lag):
        return lambda g: jnp.clip(g - lag, 0, ntot - 1)

    def cur(lag):
        return lambda g: (blk(lag)(g), 0)

    def prev(lag):
        return lambda g: (blk(lag)(g) - jnp.where(lax.rem(blk(lag)(g), nblk) > 0, 1, 0), 0)

    def nxt(lag):
        return lambda g: (blk(lag)(g) + jnp.where(lax.rem(blk(lag)(g), nblk) < nblk - 1, 1, 0), 0)

    const = lambda g: (0, 0)
    kw = 2 * KV_HEADS * ATTN_HEAD_DIM
    vw = KV_HEADS * ATTN_HEAD_DIM
    npair = ATTN_HEADS // 2
    return pl.pallas_call(
        lambda *refs: _attn_kernel(nblk, ntot, *refs),
        grid=(ntot + 2,),
        in_specs=[
            pl.BlockSpec(memory_space=pltpu.SMEM),
            pl.BlockSpec((3 * BLK, BLK), const),
            pl.BlockSpec((npair, 2 * BLK), const),
            pl.BlockSpec((1, D_ATTN), const),
            pl.BlockSpec((BLK, D_ATTN), cur(0)),
            pl.BlockSpec((BLK, kw), prev(0)), pl.BlockSpec((BLK, kw), cur(0)), pl.BlockSpec((BLK, kw), nxt(0)),
            pl.BlockSpec((BLK, vw), prev(1)), pl.BlockSpec((BLK, vw), cur(1)), pl.BlockSpec((BLK, vw), nxt(1)),
            pl.BlockSpec((BLK, D_ATTN), cur(2)),
        ],
        out_specs=pl.BlockSpec((BLK, D_ATTN), cur(2)),
        out_shape=jax.ShapeDtypeStruct((n, D_ATTN), BF16),
        scratch_shapes=[pltpu.VMEM((3, ATTN_HEADS, 3 * BLK, BLK), F32),
                        pltpu.VMEM((2, npair, 3 * BLK, 2 * BLK), BF16),
                        pltpu.VMEM((2, npair, 8, 2 * BLK), F32),
                        pltpu.VMEM((2, D_ATTN, BLK), F32),
                        pltpu.VMEM((BLK, D_ATTN), F32)],

        compiler_params=pltpu.CompilerParams(dimension_semantics=("arbitrary",),
                                             vmem_limit_bytes=VMEM_LIMIT),
        name="attn",
    )(rel_bias, bucket_t, sink_x, attn_norm_g, q, kd, kd, kd, v, v, v, ga)


def _outproj_kernel(x_ref, ys_ref, ya_ref, w_ref, g_ref, o_ref):
    acc = (jnp.dot(ys_ref[...], w_ref[0:D_SSM, :], preferred_element_type=F32)
           + jnp.dot(ya_ref[...], w_ref[D_SSM:D_MIX, :], preferred_element_type=F32))
    hres = x_ref[...] + acc
    ms = jnp.mean(hres * hres, axis=-1, keepdims=True)
    o_ref[...] = hres * lax.rsqrt(ms + EPS) * g_ref[...]


def _outproj(x2, y_ssd, y_attn, w_out, final_norm_g):
    n = x2.shape[0]
    row = lambda i: (i, 0)
    const = lambda i: (0, 0)
    return pl.pallas_call(
        _outproj_kernel,
        grid=(n // TM,),
        in_specs=[
            pl.BlockSpec((TM, D_MODEL), row),
            pl.BlockSpec((TM, D_SSM), row),
            pl.BlockSpec((TM, D_ATTN), row),
            pl.BlockSpec((D_MIX, D_MODEL), const, pipeline_mode=pl.Buffered(1)),
            pl.BlockSpec((1, D_MODEL), const),
        ],
        out_specs=pl.BlockSpec((TM, D_MODEL), row),
        out_shape=jax.ShapeDtypeStruct((n, D_MODEL), F32),
        compiler_params=pltpu.CompilerParams(dimension_semantics=("parallel",),
                                             vmem_limit_bytes=VMEM_LIMIT),
        name="outproj",
    )(x2, y_ssd, y_attn, w_out, final_norm_g)


def _arrange_w_in(w_in):
    sizes = [D_SSM, CONV_CH, 2 * SSM_HEADS, D_ATTN, KV_HEADS * ATTN_HEAD_DIM,
             KV_HEADS * ATTN_HEAD_DIM, D_ATTN]
    wz, wxbc, wdt, wq, wk, wv, wga = jnp.split(w_in, [int(s) for s in np.cumsum(sizes)[:-1]], axis=1)
    wk = wk.reshape(D_MODEL, KV_HEADS, 1, ATTN_HEAD_DIM)
    wkd = jnp.broadcast_to(wk, (D_MODEL, KV_HEADS, 2, ATTN_HEAD_DIM)).reshape(D_MODEL, -1)
    wdt = jnp.pad(wdt, ((0, 0), (0, LANES - 2 * SSM_HEADS)))
    return jnp.concatenate([wz, wxbc, wq, wkd, wv, wga, wdt], axis=1).astype(BF16)


def _expand_matrices():
    e = np.zeros((Q, 2, 2 * LANES, D_SSM), np.float32)
    for c in range(Q):
        for d in range(2):
            for h in range(SSM_HEADS):
                r = (c * 2 + d) * SSM_HEADS + h
                e[c, d, r, h * SSM_HEAD_DIM:(h + 1) * SSM_HEAD_DIM] = 1.0
                e[c, d, LANES + r, h * SSM_HEAD_DIM:(h + 1) * SSM_HEAD_DIM] = 1.0
    return e


def _bcast_matrices():
    m = np.zeros((Q, 2 * LANES, SSM_HEADS * CHUNK), np.float32)
    for c in range(Q):
        for h in range(SSM_HEADS):
            r = (c * 2 + 1) * SSM_HEADS + h
            m[c, r, h * CHUNK:(h + 1) * CHUNK] = 1.0
            m[c, LANES + r, h * CHUNK:(h + 1) * CHUNK] = 1.0
    return m


def kernel(x, norm_in_g, w_in, conv_w, conv_b, dt_bias, a_log, d_skip, ssd_norm_g, rel_bias, sink,
           attn_norm_g, w_out, final_norm_g):
    batch, seq, _ = x.shape
    assert w_out.shape[0] == 1 and seq % (Q * CHUNK) == 0 and seq % TM == 0
    x2 = x.reshape(batch * seq, D_MODEL)

    w_all = _arrange_w_in(w_in)
    e = jnp.asarray(_expand_matrices(), BF16)
    tril = jnp.asarray(np.tril(np.ones((CHUNK, CHUNK), np.float32)), BF16)
    a_log2 = jnp.tile(a_log.reshape(1, 2 * SSM_HEADS).astype(F32), (1, Q))
    dt_bias_x = jnp.pad(dt_bias.reshape(1, 2 * SSM_HEADS).astype(F32), ((0, 0), (0, LANES - 2 * SSM_HEADS)))
    dskip_x = jnp.repeat(d_skip.astype(F32), SSM_HEAD_DIM).reshape(1, D_SSM)
    sink_x = jnp.repeat(sink.astype(F32), BLK).reshape(ATTN_HEADS // 2, 2 * BLK)

    z, xs, bm, cm, dt, q, kd, v, ga = _inproj(
        x2, norm_in_g.reshape(1, D_MODEL), w_all, conv_w, conv_b.reshape(1, CONV_CH),
        dt_bias_x, seq)
    sf, sb = _states(a_log2, e, tril, xs, bm, dt, batch, seq)
    y_ssd = _ssd_out(a_log2, dskip_x, ssd_norm_g.reshape(1, D_SSM), e, jnp.asarray(_bcast_matrices(), BF16),
                     tril, xs, bm, cm, dt, z, sf, sb)
    y_attn = _attn(rel_bias.astype(F32), _bucket_table_t(), sink_x,
                   attn_norm_g.reshape(1, D_ATTN), q, kd, v, ga, batch, seq)
    out = _outproj(x2, y_ssd, y_attn, w_out[0].astype(BF16), final_norm_g.reshape(1, D_MODEL))
    return out.reshape(batch, seq, D_MODEL)
```

```python
import math

import numpy as np
import jax
import jax.numpy as jnp
from jax import lax
from jax.experimental import pallas as pl
from jax.experimental.pallas import tpu as pltpu

F32 = jnp.float32
BF16 = jnp.bfloat16

D_MODEL = 1024
D_SSM = 1024
D_ATTN = 1024
D_MIX = D_SSM + D_ATTN
SSM_HEAD_DIM = 64
SSM_HEADS = D_SSM // SSM_HEAD_DIM
SSM_GROUPS = 2
GROUP_W = D_SSM // SSM_GROUPS
D_STATE = 128
D_CONV = 5
CHUNK = 128
CONV_CH = D_SSM + 2 * SSM_GROUPS * D_STATE
ATTN_HEAD_DIM = 64
ATTN_HEADS = D_ATTN // ATTN_HEAD_DIM
KV_HEADS = 4
WINDOW = 128
BLK = 128
NUM_BUCKETS = 32
MAX_DISTANCE = 128
MAX_EXACT = 8
EPS = 1e-6
NEG_INF = -1e30
SCALE = ATTN_HEAD_DIM ** -0.5
LOG2E = math.log2(math.e)

LANES = 128
HALO = 8
TM = 512
CONV_ROWS = 64
Q = 4
VMEM_LIMIT = 56 * 1024 * 1024

OFF_Z = 0
OFF_XBC = OFF_Z + D_SSM
OFF_Q = OFF_XBC + CONV_CH
OFF_KD = OFF_Q + D_ATTN
OFF_V = OFF_KD + 2 * KV_HEADS * ATTN_HEAD_DIM
OFF_GA = OFF_V + KV_HEADS * ATTN_HEAD_DIM
OFF_DT = OFF_GA + D_ATTN
W_COLS = OFF_DT + LANES
assert TM == Q * CHUNK and Q * 2 * SSM_HEADS == LANES


def _silu(v):
    return v * jax.nn.sigmoid(v)


def _softplus(v):
    return jnp.maximum(v, 0.0) + jnp.log1p(jnp.exp(-jnp.abs(v)))


def _inproj_kernel(tiles_per_seq, x_ref, g_ref, w_ref, cw_ref, cb_ref, dtb_ref,
                   z_ref, xs_ref, bm_ref, cm_ref, dt_ref, q_ref, kd_ref, v_ref, ga_ref, pad_ref):
    i = pl.program_id(0)

    @pl.when(i == 0)
    def _():
        pad_ref[...] = jnp.zeros_like(pad_ref)

    j = lax.rem(i, tiles_per_seq)
    g = g_ref[...]

    def proj(hv, off, n):
        return jnp.dot(hv, w_ref[:, off:off + n], preferred_element_type=F32)

    xv = x_ref[...]
    ms = jnp.mean(xv * xv, axis=-1, keepdims=True)
    h = (xv * lax.rsqrt(ms + EPS) * g).astype(BF16)

    s_cur = lax.rem(i, 3)
    s_prev = lax.rem(i + 2, 3)
    s_next = lax.rem(i + 1, 3)
    xbc = proj(h, OFF_XBC, CONV_CH)
    pad_ref[s_cur, HALO:HALO + TM, :] = xbc
    pad_ref[s_prev, HALO + TM:2 * HALO + TM, :] = xbc[0:HALO] * jnp.where(j > 0, 1.0, 0.0)
    pad_ref[s_next, 0:HALO, :] = xbc[TM - HALO:TM] * jnp.where(j < tiles_per_seq - 1, 1.0, 0.0)

    z_ref[...] = _silu(proj(h, OFF_Z, D_SSM)).astype(BF16)
    q_ref[...] = (proj(h, OFF_Q, D_ATTN) * (SCALE * LOG2E)).astype(BF16)
    kd_ref[...] = proj(h, OFF_KD, 2 * KV_HEADS * ATTN_HEAD_DIM).astype(BF16)
    v_ref[...] = proj(h, OFF_V, KV_HEADS * ATTN_HEAD_DIM).astype(BF16)
    ga_ref[...] = _silu(proj(h, OFF_GA, D_ATTN)).astype(BF16)
    dt_lane = lax.broadcasted_iota(jnp.int32, (TM, LANES), 1) < 2 * SSM_HEADS
    dt = jnp.where(dt_lane, _softplus(proj(h, OFF_DT, LANES) + dtb_ref[...]), 0.0)
    dtc = dt[0:CHUNK]
    for c in range(1, Q):
        dtc = dtc + pltpu.roll(dt[c * CHUNK:(c + 1) * CHUNK], 2 * SSM_HEADS * c, axis=1)
    dt_ref[...] = dtc

    base = HALO - (D_CONV - 1) // 2
    for cc in range(CONV_CH // LANES):
        sl = slice(cc * LANES, (cc + 1) * LANES)
        if cc < D_SSM // LANES:
            dst, o = xs_ref, cc * LANES
        elif cc < (D_SSM + SSM_GROUPS * D_STATE) // LANES:
            dst, o = bm_ref, cc * LANES - D_SSM
        else:
            dst, o = cm_ref, cc * LANES - D_SSM - SSM_GROUPS * D_STATE
        for r0 in range(0, TM, CONV_ROWS):
            acc = cb_ref[:, sl] + cw_ref[0:1, sl] * pad_ref[s_prev, base + r0:base + r0 + CONV_ROWS, sl]
            for k in range(1, D_CONV):
                acc = acc + cw_ref[k:k + 1, sl] * pad_ref[s_prev, base + r0 + k:base + r0 + k + CONV_ROWS, sl]
            dst[r0:r0 + CONV_ROWS, o:o + LANES] = _silu(acc).astype(BF16)


def _inproj(x2, norm_in_g, w_all, conv_w, conv_b, dt_bias, seq):
    n = x2.shape[0]
    tiles_per_seq = seq // TM
    ntile = n // TM
    cur = lambda i: (jnp.minimum(i, ntile - 1), 0)
    lag = lambda i: (jnp.maximum(i - 1, 0), 0)
    const = lambda i: (0, 0)
    out_cols = (D_SSM, D_SSM, SSM_GROUPS * D_STATE, SSM_GROUPS * D_STATE,
                D_ATTN, 2 * KV_HEADS * ATTN_HEAD_DIM, KV_HEADS * ATTN_HEAD_DIM, D_ATTN)
    out_maps = (cur, lag, lag, lag, cur, cur, cur, cur)
    out_specs = [pl.BlockSpec((TM, c), m) for c, m in zip(out_cols, out_maps)]
    out_shape = [jax.ShapeDtypeStruct((n, c), BF16) for c in out_cols]
    out_specs.insert(4, pl.BlockSpec((CHUNK, LANES), lambda i: (0, jnp.minimum(i, ntile - 1))))
    out_shape.insert(4, jax.ShapeDtypeStruct((CHUNK, ntile * LANES), F32))
    return pl.pallas_call(
        lambda *refs: _inproj_kernel(tiles_per_seq, *refs),
        grid=(ntile + 1,),
        in_specs=[
            pl.BlockSpec((TM, D_MODEL), cur),
            pl.BlockSpec((1, D_MODEL), const),
            pl.BlockSpec((D_MODEL, W_COLS), const, pipeline_mode=pl.Buffered(1)),
            pl.BlockSpec((D_CONV, CONV_CH), const),
            pl.BlockSpec((1, CONV_CH), const),
            pl.BlockSpec((1, LANES), const),
        ],
        out_specs=out_specs,
        out_shape=out_shape,
        scratch_shapes=[pltpu.VMEM((3, TM + 2 * HALO, CONV_CH), F32)],
        compiler_params=pltpu.CompilerParams(dimension_semantics=("arbitrary",),
                                             vmem_limit_bytes=VMEM_LIMIT),
        name="inproj",
    )(x2, norm_in_g, w_all, conv_w, conv_b, dt_bias)


def _split_terms(v, n):
    terms, r = [], v
    for _ in range(n):
        t = r.astype(BF16)
        terms.append(t)
        r = r - t.astype(F32)
    return terms


def _chunk_cumsum(tril, a):
    t0, t1, t2 = _split_terms(a, 3)
    return (jnp.dot(tril, t0, preferred_element_type=F32) + jnp.dot(tril, t1, preferred_element_type=F32)
            + jnp.dot(tril, t2, preferred_element_type=F32))


def _hi_lo(v):
    return jnp.concatenate(_split_terms(v, 2), axis=1)


def _fwd_lanes():
    lane = lax.broadcasted_iota(jnp.int32, (CHUNK, LANES), 1)
    return jnp.bitwise_and(lane, 2 * SSM_HEADS - 1) < SSM_HEADS


def _states_kernel(alog_ref, e_ref, tril_ref, xsf_ref, bf_ref, dtf_ref, xsb_ref, bb_ref, dtb_ref,
                   sf_ref, sb_ref, cf_ref, cb_ref):
    @pl.when(pl.program_id(1) == 0)
    def _():
        cf_ref[...] = jnp.zeros_like(cf_ref)
        cb_ref[...] = jnp.zeros_like(cb_ref)

    a_neg = -jnp.exp(alog_ref[...])
    tril = tril_ref[...]
    fwd_lane = _fwd_lanes()

    def prep(dt_ref):
        dt = dt_ref[...]
        a = dt * a_neg
        cs = _chunk_cumsum(tril, a)
        tot = cs[CHUNK - 1:CHUNK, :]
        w = dt * jnp.exp(jnp.where(fwd_lane, tot - cs, cs - a))
        return _hi_lo(jnp.concatenate([w, jnp.broadcast_to(jnp.exp(tot), (8, LANES))], axis=0))

    def chunk(c, d, cat, xs_ref, b_ref, out_ref, carry_ref):
        rows = slice(c * CHUNK, (c + 1) * CHUNK)
        wx = jnp.dot(cat, e_ref[c, d], preferred_element_type=F32)
        xd = (xs_ref[rows, :].astype(F32) * wx[0:CHUNK]).astype(BF16)
        dec = wx[CHUNK:CHUNK + 1, :]
        for g in range(SSM_GROUPS):
            gr = slice(g * D_STATE, (g + 1) * D_STATE)
            gc = slice(g * GROUP_W, (g + 1) * GROUP_W)
            st = lax.dot_general(b_ref[rows, gr], xd[:, gc], (((0,), (0,)), ((), ())),
                                 preferred_element_type=F32)
            prev = carry_ref[gr, :]
            out_ref[c, gr, :] = prev.astype(BF16)
            carry_ref[gr, :] = prev * dec[:, gc] + st

    cat_f = prep(dtf_ref)
    cat_b = prep(dtb_ref)
    for c in range(Q):
        chunk(c, 0, cat_f, xsf_ref, bf_ref, sf_ref, cf_ref)
    for c in reversed(range(Q)):
        chunk(c, 1, cat_b, xsb_ref, bb_ref, sb_ref, cb_ref)


def _states(a_log, e, tril, xs, bm, dt, batch, seq):
    nstep = seq // (Q * CHUNK)
    nchunk = batch * seq // CHUNK
    fwd = lambda b, i: (b * nstep + i, 0)
    bwd = lambda b, i: (b * nstep + nstep - 1 - i, 0)
    const = lambda b, i: (0, 0)
    rows = Q * CHUNK
    st_shape = jax.ShapeDtypeStruct((nchunk, SSM_GROUPS * D_STATE, GROUP_W), BF16)
    return pl.pallas_call(
        _states_kernel,
        grid=(batch, nstep),
        in_specs=[
            pl.BlockSpec((1, LANES), const),
            pl.BlockSpec((Q, 2, 2 * LANES, D_SSM), lambda b, i: (0, 0, 0, 0), pipeline_mode=pl.Buffered(1)),
            pl.BlockSpec((CHUNK, CHUNK), const),
            pl.BlockSpec((rows, D_SSM), fwd),
            pl.BlockSpec((rows, SSM_GROUPS * D_STATE), fwd),
            pl.BlockSpec((CHUNK, LANES), lambda b, i: (0, b * nstep + i)),
            pl.BlockSpec((rows, D_SSM), bwd),
            pl.BlockSpec((rows, SSM_GROUPS * D_STATE), bwd),
            pl.BlockSpec((CHUNK, LANES), lambda b, i: (0, b * nstep + nstep - 1 - i)),
        ],
        out_specs=[
            pl.BlockSpec((Q, SSM_GROUPS * D_STATE, GROUP_W), lambda b, i: (b * nstep + i, 0, 0)),
            pl.BlockSpec((Q, SSM_GROUPS * D_STATE, GROUP_W), lambda b, i: (b * nstep + nstep - 1 - i, 0, 0)),
        ],
        out_shape=[st_shape, st_shape],
        scratch_shapes=[pltpu.VMEM((SSM_GROUPS * D_STATE, GROUP_W), F32),
                        pltpu.VMEM((SSM_GROUPS * D_STATE, GROUP_W), F32)],
        compiler_params=pltpu.CompilerParams(dimension_semantics=("arbitrary", "arbitrary"),
                                             vmem_limit_bytes=VMEM_LIMIT),
        name="ssd_states",
    )(a_log, e, tril, xs, bm, dt, xs, bm, dt)


def _ssd_out_kernel(alog_ref, dskip_ref, gn_ref, e_ref, selb_ref, tril_ref, xs_ref, bm_ref, cm_ref, dt_ref,
                    z_ref, sf_ref, sb_ref, ya_ref, x_ref, w_ref, fg_ref, o_ref, ys_ref):
    li = lax.broadcasted_iota(jnp.int32, (CHUNK, CHUNK), 0)
    si = lax.broadcasted_iota(jnp.int32, (CHUNK, CHUNK), 1)
    lower = li >= si
    eye = li == si
    left = lax.broadcasted_iota(jnp.int32, (CHUNK, LANES), 1) < SSM_HEAD_DIM

    dt = dt_ref[...]
    a = dt * (-LOG2E * jnp.exp(alog_ref[...]))
    cs = _chunk_cumsum(tril_ref[...], a)
    tot = cs[CHUNK - 1:CHUNK, :]
    colq = jnp.where(_fwd_lanes(), cs, tot - cs + a)
    rowq = (colq - jnp.log2(dt)).T
    dtr = dt.T.astype(BF16)
    cat = _hi_lo(jnp.exp2(colq))
    catq = _hi_lo(colq)

    for c in range(Q):
        rows = slice(c * CHUNK, (c + 1) * CHUNK)
        ef = jnp.dot(cat, e_ref[c, 0], preferred_element_type=F32)
        eb = jnp.dot(cat, e_ref[c, 1], preferred_element_type=F32)
        colb = jnp.dot(catq, selb_ref[c], preferred_element_type=F32)
        for g in range(SSM_GROUPS):
            gr = slice(g * D_STATE, (g + 1) * D_STATE)
            gc = slice(g * GROUP_W, (g + 1) * GROUP_W)
            cmg = cm_ref[rows, gr]
            cb = lax.dot_general(cmg, bm_ref[rows, gr], (((1,), (1,)), ((), ())),
                                 preferred_element_type=F32)
            cbb = cb.astype(BF16)
            cbeye = jnp.where(eye, cb, 0.0).astype(BF16)
            y_off = (jnp.dot(cmg, sf_ref[c, gr, :], preferred_element_type=F32) * ef[:, gc]
                     + jnp.dot(cmg, sb_ref[c, gr, :], preferred_element_type=F32) * eb[:, gc])
            xg = xs_ref[rows, gc]
            yg = y_off + dskip_ref[:, gc] * xg.astype(F32)
            pairs = []
            for pp in range(GROUP_W // LANES):
                scores = []
                for hh in range(2):
                    hd = g * (SSM_HEADS // SSM_GROUPS) + pp * 2 + hh
                    h = c * 2 * SSM_HEADS + hd
                    hb = h + SSM_HEADS
                    arg = jnp.where(lower, colq[:, h:h + 1] - rowq[h:h + 1, :],
                                    colb[:, hd * CHUNK:(hd + 1) * CHUNK] - rowq[hb:hb + 1, :])
                    scores.append(cbb * jnp.exp2(arg).astype(BF16) + cbeye * dtr[hb:hb + 1, :])
                xt = xg[:, pp * LANES:(pp + 1) * LANES]
                zero = jnp.zeros_like(xt)
                rhs = jnp.concatenate([jnp.where(left, xt, zero), jnp.where(left, zero, xt)], axis=0)
                pairs.append(jnp.dot(jnp.concatenate(scores, axis=1), rhs, preferred_element_type=F32))
            yg = yg + jnp.concatenate(pairs, axis=1)
            vg = yg * z_ref[rows, gc].astype(F32)
            ms = jnp.mean(vg * vg, axis=-1, keepdims=True)
            ys_ref[rows, gc] = (vg * lax.rsqrt(ms + EPS) * gn_ref[:, gc]).astype(BF16)

        acc = (jnp.dot(ys_ref[rows, :], w_ref[0:D_SSM, :], preferred_element_type=F32)
               + jnp.dot(ya_ref[rows, :], w_ref[D_SSM:D_MIX, :], preferred_element_type=F32))
        hres = x_ref[rows, :] + acc
        ms = jnp.mean(hres * hres, axis=-1, keepdims=True)
        o_ref[rows, :] = hres * lax.rsqrt(ms + EPS) * fg_ref[...]


def _ssd_out(a_log, dskip_x, ssd_norm_g, e, selb, tril, xs, bm, cm, dt, z, sf, sb, y_attn, x2, w_out, final_norm_g):
    n = xs.shape[0]
    rows = Q * CHUNK
    row = lambda i: (i, 0)
    const = lambda i: (0, 0)
    return pl.pallas_call(
        _ssd_out_kernel,
        grid=(n // rows,),
        in_specs=[
            pl.BlockSpec((1, LANES), const),
            pl.BlockSpec((1, D_SSM), const),
            pl.BlockSpec((1, D_SSM), const),
            pl.BlockSpec((Q, 2, 2 * LANES, D_SSM), lambda i: (0, 0, 0, 0), pipeline_mode=pl.Buffered(1)),
            pl.BlockSpec((Q, 2 * LANES, SSM_HEADS * CHUNK), lambda i: (0, 0, 0), pipeline_mode=pl.Buffered(1)),
            pl.BlockSpec((CHUNK, CHUNK), const),
            pl.BlockSpec((rows, D_SSM), row),
            pl.BlockSpec((rows, SSM_GROUPS * D_STATE), row),
            pl.BlockSpec((rows, SSM_GROUPS * D_STATE), row),
            pl.BlockSpec((CHUNK, LANES), lambda i: (0, i)),
            pl.BlockSpec((rows, D_SSM), row),
            pl.BlockSpec((Q, SSM_GROUPS * D_STATE, GROUP_W), lambda i: (i, 0, 0)),
            pl.BlockSpec((Q, SSM_GROUPS * D_STATE, GROUP_W), lambda i: (i, 0, 0)),
            pl.BlockSpec((rows, D_ATTN), row),
            pl.BlockSpec((rows, D_MODEL), row),
            pl.BlockSpec((D_MIX, D_MODEL), const, pipeline_mode=pl.Buffered(1)),
            pl.BlockSpec((1, D_MODEL), const),
        ],
        out_specs=pl.BlockSpec((rows, D_MODEL), row),
        out_shape=jax.ShapeDtypeStruct((n, D_MODEL), F32),
        scratch_shapes=[pltpu.VMEM((rows, D_SSM), BF16)],
        compiler_params=pltpu.CompilerParams(dimension_semantics=("parallel",),
                                             vmem_limit_bytes=VMEM_LIMIT),
        name="ssd_out",
    )(a_log, dskip_x, ssd_norm_g, e, selb, tril, xs, bm, cm, dt, z, sf, sb, y_attn, x2, w_out, final_norm_g)


def _bucket_table_t():
    rel = jnp.arange(3 * BLK)[:, None] - BLK - jnp.arange(BLK)[None, :]
    half = NUM_BUCKETS // 2
    ret = (rel > 0).astype(jnp.int32) * half
    n = jnp.abs(rel)
    nf = jnp.maximum(n, 1).astype(F32)
    large = MAX_EXACT + (jnp.log(nf / MAX_EXACT) / math.log(MAX_DISTANCE / MAX_EXACT)
                         * (half - MAX_EXACT)).astype(jnp.int32)
    large = jnp.minimum(large, half - 1)
    return (ret + jnp.where(n < MAX_EXACT, n, large)).astype(jnp.int32)


def _attn_kernel(nblk, ntot, rb_ref, bucket_ref, sink_ref, gn_ref, q_ref, kp_ref, kc_ref, kn_ref,
                 vp_ref, vc_ref, vn_ref, ga_ref, y_ref,
                 tab_ref, pe_ref, mk_ref, ot_ref, vs_ref):
    g = pl.program_id(0)

    @pl.when(g == 0)
    def _init():
        pe_ref[...] = jnp.zeros_like(pe_ref)
        mk_ref[...] = jnp.zeros_like(mk_ref)
        ot_ref[...] = jnp.zeros_like(ot_ref)
        bk = bucket_ref[...]
        t = lax.broadcasted_iota(jnp.int32, (3 * BLK, BLK), 0)
        qq = lax.broadcasted_iota(jnp.int32, (3 * BLK, BLK), 1)
        in_window = jnp.abs(t - BLK - qq) <= WINDOW

        def per_head(h, carry):
            acc = jnp.zeros((3 * BLK, BLK), F32)
            for b in range(NUM_BUCKETS):
                acc = jnp.where(bk == b, rb_ref[b, h], acc)
            tab = jnp.where(in_window, acc * LOG2E, NEG_INF)
            tab_ref[0, h] = tab
            tab_ref[1, h] = jnp.where(t >= BLK, tab, NEG_INF)
            tab_ref[2, h] = jnp.where(t < 2 * BLK, tab, NEG_INF)
            return carry

        lax.fori_loop(0, ATTN_HEADS, per_head, 0)

    left = lax.broadcasted_iota(jnp.int32, (BLK, LANES), 1) < ATTN_HEAD_DIM
    left3 = lax.broadcasted_iota(jnp.int32, (3 * BLK, LANES), 1) < ATTN_HEAD_DIM
    s2 = sink_ref[...] * LOG2E
    def run(even, odd):
        ia = lax.rem(jnp.minimum(g, ntot - 1), nblk)
        var = jnp.where(ia == 0, 1, jnp.where(ia == nblk - 1, 2, 0))
        kcat = jnp.concatenate([kp_ref[...], kc_ref[...], kn_ref[...]], axis=0)
        vcat = jnp.concatenate([vp_ref[...], vc_ref[...], vn_ref[...]], axis=0)
        one = jnp.ones((3 * BLK, LANES), BF16)

        def stage_a(p):
            kd = kcat[:, (p // 2) * LANES:(p // 2 + 1) * LANES]
            qt = q_ref[:, p * LANES:(p + 1) * LANES]
            zero = jnp.zeros_like(qt)
            rhs_t = jnp.concatenate([jnp.where(left, qt, zero), jnp.where(left, zero, qt)], axis=0)
            lg = lax.dot_general(kd, rhs_t, (((1,), (1,)), ((), ())), preferred_element_type=F32)
            lg = lg + jnp.concatenate([tab_ref[var, 2 * p], tab_ref[var, 2 * p + 1]], axis=1)
            for kb in range(3):
                blk = lg[kb * BLK:(kb + 1) * BLK]
                mk = jnp.max(blk, axis=0, keepdims=True)
                pe_ref[even, p, kb * BLK:(kb + 1) * BLK, :] = jnp.exp2(blk - mk).astype(BF16)
                mk_ref[even, p, kb:kb + 1, :] = mk

        def stage_b(p):
            kv = p // 2
            vt = vcat[:, (kv // 2) * LANES:(kv // 2 + 1) * LANES]
            vmod = jnp.where(left3, vt, one) if kv % 2 == 0 else jnp.where(left3, one, vt)
            r0 = (kv % 2) * ATTN_HEAD_DIM
            d0 = ATTN_HEAD_DIM - r0
            mk = mk_ref[odd, p]
            sp = s2[p:p + 1, :]
            m = jnp.maximum(jnp.max(mk[0:3], axis=0, keepdims=True), sp)
            o2 = None
            for kb in range(3):
                part = lax.dot_general(vmod[kb * BLK:(kb + 1) * BLK], pe_ref[odd, p, kb * BLK:(kb + 1) * BLK, :],
                                       (((0,), (0,)), ((), ())), preferred_element_type=F32)
                part = part * jnp.exp2(mk[kb:kb + 1] - m)
                o2 = part if o2 is None else o2 + part
            o = o2[r0:r0 + ATTN_HEAD_DIM, :] / (o2[d0:d0 + 1, :] + jnp.exp2(sp - m))
            ot_ref[odd, 2 * p * ATTN_HEAD_DIM:(2 * p + 1) * ATTN_HEAD_DIM, :] = o[:, 0:BLK]
            ot_ref[odd, (2 * p + 1) * ATTN_HEAD_DIM:(2 * p + 2) * ATTN_HEAD_DIM, :] = o[:, BLK:2 * BLK]

        def stage_c(p, ssq):
            cols = slice(p * LANES, (p + 1) * LANES)
            v = ot_ref[even, p * LANES:(p + 1) * LANES, :].T * ga_ref[:, cols].astype(F32)
            vs_ref[:, cols] = v
            return ssq + jnp.sum(v * v, axis=-1, keepdims=True)

        ssq = jnp.zeros((BLK, 1), F32)
        for p in range(ATTN_HEADS // 2):
            stage_a(p)
            stage_b(p)
            ssq = stage_c(p, ssq)
        y_ref[...] = (vs_ref[...] * lax.rsqrt(ssq * (1.0 / D_ATTN) + EPS) * gn_ref[...]).astype(BF16)

    @pl.when(lax.rem(g, 2) == 0)
    def _():
        run(0, 1)

    @pl.when(lax.rem(g, 2) == 1)
    def _():
        run(1, 0)


def _attn(rel_bias, bucket_t, sink_x, attn_norm_g, q, kd, v, ga, batch, seq):
    n = q.shape[0]
    nblk = seq // BLK
    ntot = batch * nblk

    def blk(lag):
        return lambda g: jnp.clip(g - lag, 0, ntot - 1)

    def cur(lag):
        return lambda g: (blk(lag)(g), 0)

    def prev(lag):
        return lambda g: (blk(lag)(g) - jnp.where(lax.rem(blk(lag)(g), nblk) > 0, 1, 0), 0)

    def nxt(lag):
        return lambda g: (blk(lag)(g) + jnp.where(lax.rem(blk(lag)(g), nblk) < nblk - 1, 1, 0), 0)

    const = lambda g: (0, 0)
    kw = 2 * KV_HEADS * ATTN_HEAD_DIM
    vw = KV_HEADS * ATTN_HEAD_DIM
    npair = ATTN_HEADS // 2
    return pl.pallas_call(
        lambda *refs: _attn_kernel(nblk, ntot, *refs),
        grid=(ntot + 2,),
        in_specs=[
            pl.BlockSpec(memory_space=pltpu.SMEM),
            pl.BlockSpec((3 * BLK, BLK), const),
            pl.BlockSpec((npair, 2 * BLK), const),
            pl.BlockSpec((1, D_ATTN), const),
            pl.BlockSpec((BLK, D_ATTN), cur(0)),
            pl.BlockSpec((BLK, kw), prev(0)), pl.BlockSpec((BLK, kw), cur(0)), pl.BlockSpec((BLK, kw), nxt(0)),
            pl.BlockSpec((BLK, vw), prev(1)), pl.BlockSpec((BLK, vw), cur(1)), pl.BlockSpec((BLK, vw), nxt(1)),
            pl.BlockSpec((BLK, D_ATTN), cur(2)),
        ],
        out_specs=pl.BlockSpec((BLK, D_ATTN), cur(2)),
        out_shape=jax.ShapeDtypeStruct((n, D_ATTN), BF16),
        scratch_shapes=[pltpu.VMEM((3, ATTN_HEADS, 3 * BLK, BLK), F32),
                        pltpu.VMEM((2, npair, 3 * BLK, 2 * BLK), BF16),
                        pltpu.VMEM((2, npair, 8, 2 * BLK), F32),
                        pltpu.VMEM((2, D_ATTN, BLK), F32),
                        pltpu.VMEM((BLK, D_ATTN), F32)],

        compiler_params=pltpu.CompilerParams(dimension_semantics=("arbitrary",),
                                             vmem_limit_bytes=VMEM_LIMIT),
        name="attn",
    )(rel_bias, bucket_t, sink_x, attn_norm_g, q, kd, kd, kd, v, v, v, ga)


def _arrange_w_in(w_in):
    sizes = [D_SSM, CONV_CH, 2 * SSM_HEADS, D_ATTN, KV_HEADS * ATTN_HEAD_DIM,
             KV_HEADS * ATTN_HEAD_DIM, D_ATTN]
    wz, wxbc, wdt, wq, wk, wv, wga = jnp.split(w_in, [int(s) for s in np.cumsum(sizes)[:-1]], axis=1)
    wk = wk.reshape(D_MODEL, KV_HEADS, 1, ATTN_HEAD_DIM)
    wkd = jnp.broadcast_to(wk, (D_MODEL, KV_HEADS, 2, ATTN_HEAD_DIM)).reshape(D_MODEL, -1)
    wdt = jnp.pad(wdt, ((0, 0), (0, LANES - 2 * SSM_HEADS)))
    return jnp.concatenate([wz, wxbc, wq, wkd, wv, wga, wdt], axis=1).astype(BF16)


def _expand_matrices():
    e = np.zeros((Q, 2, 2 * LANES, D_SSM), np.float32)
    for c in range(Q):
        for d in range(2):
            for h in range(SSM_HEADS):
                r = (c * 2 + d) * SSM_HEADS + h
                e[c, d, r, h * SSM_HEAD_DIM:(h + 1) * SSM_HEAD_DIM] = 1.0
                e[c, d, LANES + r, h * SSM_HEAD_DIM:(h + 1) * SSM_HEAD_DIM] = 1.0
    return e


def _bcast_matrices():
    m = np.zeros((Q, 2 * LANES, SSM_HEADS * CHUNK), np.float32)
    for c in range(Q):
        for h in range(SSM_HEADS):
            r = (c * 2 + 1) * SSM_HEADS + h
            m[c, r, h * CHUNK:(h + 1) * CHUNK] = 1.0
            m[c, LANES + r, h * CHUNK:(h + 1) * CHUNK] = 1.0
    return m


def kernel(x, norm_in_g, w_in, conv_w, conv_b, dt_bias, a_log, d_skip, ssd_norm_g, rel_bias, sink,
           attn_norm_g, w_out, final_norm_g):
    batch, seq, _ = x.shape
    assert w_out.shape[0] == 1 and seq % (Q * CHUNK) == 0 and seq % TM == 0
    x2 = x.reshape(batch * seq, D_MODEL)

    w_all = _arrange_w_in(w_in)
    e = jnp.asarray(_expand_matrices(), BF16)
    tril = jnp.asarray(np.tril(np.ones((CHUNK, CHUNK), np.float32)), BF16)
    a_log2 = jnp.tile(a_log.reshape(1, 2 * SSM_HEADS).astype(F32), (1, Q))
    dt_bias_x = jnp.pad(dt_bias.reshape(1, 2 * SSM_HEADS).astype(F32), ((0, 0), (0, LANES - 2 * SSM_HEADS)))
    dskip_x = jnp.repeat(d_skip.astype(F32), SSM_HEAD_DIM).reshape(1, D_SSM)
    sink_x = jnp.repeat(sink.astype(F32), BLK).reshape(ATTN_HEADS // 2, 2 * BLK)

    z, xs, bm, cm, dt, q, kd, v, ga = _inproj(
        x2, norm_in_g.reshape(1, D_MODEL), w_all, conv_w, conv_b.reshape(1, CONV_CH),
        dt_bias_x, seq)
    sf, sb = _states(a_log2, e, tril, xs, bm, dt, batch, seq)
    y_attn = _attn(rel_bias.astype(F32), _bucket_table_t(), sink_x,
                   attn_norm_g.reshape(1, D_ATTN), q, kd, v, ga, batch, seq)
    out = _ssd_out(a_log2, dskip_x, ssd_norm_g.reshape(1, D_SSM), e, jnp.asarray(_bcast_matrices(), BF16),
                   tril, xs, bm, cm, dt, z, sf, sb, y_attn, x2, w_out[0].astype(BF16),
                   final_norm_g.reshape(1, D_MODEL))
    return out.reshape(batch, seq, D_MODEL)
```

```python
import math

import numpy as np
import jax
import jax.numpy as jnp
from jax import lax
from jax.experimental import pallas as pl
from jax.experimental.pallas import tpu as pltpu

F32 = jnp.float32
BF16 = jnp.bfloat16

D_MODEL = 1024
D_SSM = 1024
D_ATTN = 1024
D_MIX = D_SSM + D_ATTN
SSM_HEAD_DIM = 64
SSM_HEADS = D_SSM // SSM_HEAD_DIM
SSM_GROUPS = 2
GROUP_W = D_SSM // SSM_GROUPS
D_STATE = 128
D_CONV = 5
CHUNK = 128
CONV_CH = D_SSM + 2 * SSM_GROUPS * D_STATE
ATTN_HEAD_DIM = 64
ATTN_HEADS = D_ATTN // ATTN_HEAD_DIM
KV_HEADS = 4
WINDOW = 128
BLK = 128
NUM_BUCKETS = 32
MAX_DISTANCE = 128
MAX_EXACT = 8
EPS = 1e-6
NEG_INF = -1e30
SCALE = ATTN_HEAD_DIM ** -0.5
LOG2E = math.log2(math.e)

LANES = 128
HALO = 8
TM = 512
CONV_ROWS = 64
Q = 4
VMEM_LIMIT = 56 * 1024 * 1024

OFF_Z = 0
OFF_XBC = OFF_Z + D_SSM
OFF_Q = OFF_XBC + CONV_CH
OFF_KD = OFF_Q + D_ATTN
OFF_V = OFF_KD + 2 * KV_HEADS * ATTN_HEAD_DIM
OFF_GA = OFF_V + KV_HEADS * ATTN_HEAD_DIM
OFF_DT = OFF_GA + D_ATTN
W_COLS = OFF_DT + LANES
assert TM == Q * CHUNK and Q * 2 * SSM_HEADS == LANES


def _silu(v):
    return v * jax.nn.sigmoid(v)


def _softplus(v):
    return jnp.maximum(v, 0.0) + jnp.log1p(jnp.exp(-jnp.abs(v)))


def _inproj_kernel(tiles_per_seq, x_ref, g_ref, w_ref, cw_ref, cb_ref, dtb_ref,
                   z_ref, xs_ref, bm_ref, cm_ref, dt_ref, q_ref, kd_ref, v_ref, ga_ref, pad_ref):
    i = pl.program_id(0)

    @pl.when(i == 0)
    def _():
        pad_ref[...] = jnp.zeros_like(pad_ref)

    j = lax.rem(i, tiles_per_seq)
    g = g_ref[...]

    def proj(hv, off, n):
        return jnp.dot(hv, w_ref[:, off:off + n], preferred_element_type=F32)

    xv = x_ref[...]
    ms = jnp.mean(xv * xv, axis=-1, keepdims=True)
    h = (xv * lax.rsqrt(ms + EPS) * g).astype(BF16)

    s_cur = lax.rem(i, 3)
    s_prev = lax.rem(i + 2, 3)
    s_next = lax.rem(i + 1, 3)
    xbc = proj(h, OFF_XBC, CONV_CH)
    pad_ref[s_cur, HALO:HALO + TM, :] = xbc
    pad_ref[s_prev, HALO + TM:2 * HALO + TM, :] = xbc[0:HALO] * jnp.where(j > 0, 1.0, 0.0)
    pad_ref[s_next, 0:HALO, :] = xbc[TM - HALO:TM] * jnp.where(j < tiles_per_seq - 1, 1.0, 0.0)

    z_ref[...] = _silu(proj(h, OFF_Z, D_SSM)).astype(BF16)
    q_ref[...] = (proj(h, OFF_Q, D_ATTN) * (SCALE * LOG2E)).astype(BF16)
    kd_ref[...] = proj(h, OFF_KD, 2 * KV_HEADS * ATTN_HEAD_DIM).astype(BF16)
    v_ref[...] = proj(h, OFF_V, KV_HEADS * ATTN_HEAD_DIM).astype(BF16)
    ga_ref[...] = _silu(proj(h, OFF_GA, D_ATTN)).astype(BF16)
    dt_lane = lax.broadcasted_iota(jnp.int32, (TM, LANES), 1) < 2 * SSM_HEADS
    dt = jnp.where(dt_lane, _softplus(proj(h, OFF_DT, LANES) + dtb_ref[...]), 0.0)
    dtc = dt[0:CHUNK]
    for c in range(1, Q):
        dtc = dtc + pltpu.roll(dt[c * CHUNK:(c + 1) * CHUNK], 2 * SSM_HEADS * c, axis=1)
    dt_ref[...] = dtc

    base = HALO - (D_CONV - 1) // 2
    for cc in range(CONV_CH // LANES):
        sl = slice(cc * LANES, (cc + 1) * LANES)
        if cc < D_SSM // LANES:
            dst, o = xs_ref, cc * LANES
        elif cc < (D_SSM + SSM_GROUPS * D_STATE) // LANES:
            dst, o = bm_ref, cc * LANES - D_SSM
        else:
            dst, o = cm_ref, cc * LANES - D_SSM - SSM_GROUPS * D_STATE
        for r0 in range(0, TM, CONV_ROWS):
            acc = cb_ref[:, sl] + cw_ref[0:1, sl] * pad_ref[s_prev, base + r0:base + r0 + CONV_ROWS, sl]
            for k in range(1, D_CONV):
                acc = acc + cw_ref[k:k + 1, sl] * pad_ref[s_prev, base + r0 + k:base + r0 + k + CONV_ROWS, sl]
            dst[r0:r0 + CONV_ROWS, o:o + LANES] = _silu(acc).astype(BF16)


def _inproj(x2, norm_in_g, w_all, conv_w, conv_b, dt_bias, seq):
    n = x2.shape[0]
    tiles_per_seq = seq // TM
    ntile = n // TM
    cur = lambda i: (jnp.minimum(i, ntile - 1), 0)
    lag = lambda i: (jnp.maximum(i - 1, 0), 0)
    const = lambda i: (0, 0)
    out_cols = (D_SSM, D_SSM, SSM_GROUPS * D_STATE, SSM_GROUPS * D_STATE,
                D_ATTN, 2 * KV_HEADS * ATTN_HEAD_DIM, KV_HEADS * ATTN_HEAD_DIM, D_ATTN)
    out_maps = (cur, lag, lag, lag, cur, cur, cur, cur)
    out_specs = [pl.BlockSpec((TM, c), m) for c, m in zip(out_cols, out_maps)]
    out_shape = [jax.ShapeDtypeStruct((n, c), BF16) for c in out_cols]
    out_specs.insert(4, pl.BlockSpec((CHUNK, LANES), lambda i: (0, jnp.minimum(i, ntile - 1))))
    out_shape.insert(4, jax.ShapeDtypeStruct((CHUNK, ntile * LANES), F32))
    return pl.pallas_call(
        lambda *refs: _inproj_kernel(tiles_per_seq, *refs),
        grid=(ntile + 1,),
        in_specs=[
            pl.BlockSpec((TM, D_MODEL), cur),
            pl.BlockSpec((1, D_MODEL), const),
            pl.BlockSpec((D_MODEL, W_COLS), const, pipeline_mode=pl.Buffered(1)),
            pl.BlockSpec((D_CONV, CONV_CH), const),
            pl.BlockSpec((1, CONV_CH), const),
            pl.BlockSpec((1, LANES), const),
        ],
        out_specs=out_specs,
        out_shape=out_shape,
        scratch_shapes=[pltpu.VMEM((3, TM + 2 * HALO, CONV_CH), F32)],
        compiler_params=pltpu.CompilerParams(dimension_semantics=("arbitrary",),
                                             vmem_limit_bytes=VMEM_LIMIT),
        name="inproj",
    )(x2, norm_in_g, w_all, conv_w, conv_b, dt_bias)


def _split_terms(v, n):
    terms, r = [], v
    for _ in range(n):
        t = r.astype(BF16)
        terms.append(t)
        r = r - t.astype(F32)
    return terms


def _chunk_cumsum(tril, a):
    t0, t1, t2 = _split_terms(a, 3)
    return (jnp.dot(tril, t0, preferred_element_type=F32) + jnp.dot(tril, t1, preferred_element_type=F32)
            + jnp.dot(tril, t2, preferred_element_type=F32))


def _hi_lo(v):
    return jnp.concatenate(_split_terms(v, 2), axis=1)


def _fwd_lanes():
    lane = lax.broadcasted_iota(jnp.int32, (CHUNK, LANES), 1)
    return jnp.bitwise_and(lane, 2 * SSM_HEADS - 1) < SSM_HEADS


def _states_kernel(alog_ref, e_ref, tril_ref, xsf_ref, bf_ref, dtf_ref, xsb_ref, bb_ref, dtb_ref,
                   sf_ref, sb_ref, cf_ref, cb_ref):
    @pl.when(pl.program_id(1) == 0)
    def _():
        cf_ref[...] = jnp.zeros_like(cf_ref)
        cb_ref[...] = jnp.zeros_like(cb_ref)

    a_neg = -jnp.exp(alog_ref[...])
    tril = tril_ref[...]
    fwd_lane = _fwd_lanes()

    def prep(dt_ref):
        dt = dt_ref[...]
        a = dt * a_neg
        cs = _chunk_cumsum(tril, a)
        tot = cs[CHUNK - 1:CHUNK, :]
        w = dt * jnp.exp(jnp.where(fwd_lane, tot - cs, cs - a))
        return _hi_lo(jnp.concatenate([w, jnp.broadcast_to(jnp.exp(tot), (8, LANES))], axis=0))

    def chunk(c, d, cat, xs_ref, b_ref, out_ref, carry_ref):
        rows = slice(c * CHUNK, (c + 1) * CHUNK)
        wx = jnp.dot(cat, e_ref[c, d], preferred_element_type=F32)
        xd = (xs_ref[rows, :].astype(F32) * wx[0:CHUNK]).astype(BF16)
        dec = wx[CHUNK:CHUNK + 1, :]
        for g in range(SSM_GROUPS):
            gr = slice(g * D_STATE, (g + 1) * D_STATE)
            gc = slice(g * GROUP_W, (g + 1) * GROUP_W)
            st = lax.dot_general(b_ref[rows, gr], xd[:, gc], (((0,), (0,)), ((), ())),
                                 preferred_element_type=F32)
            prev = carry_ref[gr, :]
            out_ref[c, gr, :] = prev.astype(BF16)
            carry_ref[gr, :] = prev * dec[:, gc] + st

    cat_f = prep(dtf_ref)
    cat_b = prep(dtb_ref)
    for c in range(Q):
        chunk(c, 0, cat_f, xsf_ref, bf_ref, sf_ref, cf_ref)
    for c in reversed(range(Q)):
        chunk(c, 1, cat_b, xsb_ref, bb_ref, sb_ref, cb_ref)


def _states(a_log, e, tril, xs, bm, dt, batch, seq):
    nstep = seq // (Q * CHUNK)
    nchunk = batch * seq // CHUNK
    fwd = lambda b, i: (b * nstep + i, 0)
    bwd = lambda b, i: (b * nstep + nstep - 1 - i, 0)
    const = lambda b, i: (0, 0)
    rows = Q * CHUNK
    st_shape = jax.ShapeDtypeStruct((nchunk, SSM_GROUPS * D_STATE, GROUP_W), BF16)
    return pl.pallas_call(
        _states_kernel,
        grid=(batch, nstep),
        in_specs=[
            pl.BlockSpec((1, LANES), const),
            pl.BlockSpec((Q, 2, 2 * LANES, D_SSM), lambda b, i: (0, 0, 0, 0), pipeline_mode=pl.Buffered(1)),
            pl.BlockSpec((CHUNK, CHUNK), const),
            pl.BlockSpec((rows, D_SSM), fwd),
            pl.BlockSpec((rows, SSM_GROUPS * D_STATE), fwd),
            pl.BlockSpec((CHUNK, LANES), lambda b, i: (0, b * nstep + i)),
            pl.BlockSpec((rows, D_SSM), bwd),
            pl.BlockSpec((rows, SSM_GROUPS * D_STATE), bwd),
            pl.BlockSpec((CHUNK, LANES), lambda b, i: (0, b * nstep + nstep - 1 - i)),
        ],
        out_specs=[
            pl.BlockSpec((Q, SSM_GROUPS * D_STATE, GROUP_W), lambda b, i: (b * nstep + i, 0, 0)),
            pl.BlockSpec((Q, SSM_GROUPS * D_STATE, GROUP_W), lambda b, i: (b * nstep + nstep - 1 - i, 0, 0)),
        ],
        out_shape=[st_shape, st_shape],
        scratch_shapes=[pltpu.VMEM((SSM_GROUPS * D_STATE, GROUP_W), F32),
                        pltpu.VMEM((SSM_GROUPS * D_STATE, GROUP_W), F32)],
        compiler_params=pltpu.CompilerParams(dimension_semantics=("arbitrary", "arbitrary"),
                                             vmem_limit_bytes=VMEM_LIMIT),
        name="ssd_states",
    )(a_log, e, tril, xs, bm, dt, xs, bm, dt)


def _ssd_out_kernel(alog_ref, dskip_ref, gn_ref, selb_ref, tril_ref, xs_ref, bm_ref, cm_ref, dt_ref,
                    z_ref, sf_ref, sb_ref, ya_ref, x_ref, w_ref, fg_ref, o_ref, ys_ref):
    li = lax.broadcasted_iota(jnp.int32, (CHUNK, CHUNK), 0)
    si = lax.broadcasted_iota(jnp.int32, (CHUNK, CHUNK), 1)
    lower = li >= si
    eye = li == si
    left = lax.broadcasted_iota(jnp.int32, (CHUNK, LANES), 1) < SSM_HEAD_DIM

    dt = dt_ref[...]
    a = dt * (-LOG2E * jnp.exp(alog_ref[...]))
    cs = _chunk_cumsum(tril_ref[...], a)
    tot = cs[CHUNK - 1:CHUNK, :]
    colq = jnp.where(_fwd_lanes(), cs, tot - cs + a)
    rowq = (colq - jnp.log2(dt)).T
    dtr = dt.T.astype(BF16)
    catq = _hi_lo(colq)

    for c in range(Q):
        rows = slice(c * CHUNK, (c + 1) * CHUNK)
        colb = jnp.dot(catq, selb_ref[c], preferred_element_type=F32)
        for g in range(SSM_GROUPS):
            gr = slice(g * D_STATE, (g + 1) * D_STATE)
            gc = slice(g * GROUP_W, (g + 1) * GROUP_W)
            cmg = cm_ref[rows, gr]
            cb = lax.dot_general(cmg, bm_ref[rows, gr], (((1,), (1,)), ((), ())),
                                 preferred_element_type=F32)
            cbb = cb.astype(BF16)
            cbeye = jnp.where(eye, cb, 0.0).astype(BF16)
            csf = jnp.dot(cmg, sf_ref[c, gr, :], preferred_element_type=F32)
            csb = jnp.dot(cmg, sb_ref[c, gr, :], preferred_element_type=F32)
            xg = xs_ref[rows, gc]
            pairs = []
            for pp in range(GROUP_W // LANES):
                pc = slice(pp * LANES, (pp + 1) * LANES)
                scores, colf_bc, colb_bc = [], [], []
                for hh in range(2):
                    hd = g * (SSM_HEADS // SSM_GROUPS) + pp * 2 + hh
                    h = c * 2 * SSM_HEADS + hd
                    hb = h + SSM_HEADS
                    colf_bc.append(jnp.broadcast_to(colq[:, h:h + 1], (CHUNK, CHUNK)))
                    colb_bc.append(colb[:, hd * CHUNK:(hd + 1) * CHUNK])
                    arg = jnp.where(lower, colf_bc[hh] - rowq[h:h + 1, :], colb_bc[hh] - rowq[hb:hb + 1, :])
                    scores.append(cbb * jnp.exp2(arg).astype(BF16) + cbeye * dtr[hb:hb + 1, :])
                xt = xg[:, pc]
                zero = jnp.zeros_like(xt)
                rhs = jnp.concatenate([jnp.where(left, xt, zero), jnp.where(left, zero, xt)], axis=0)
                y_diag = jnp.dot(jnp.concatenate(scores, axis=1), rhs, preferred_element_type=F32)
                y_off = (csf[:, pc] * jnp.exp2(jnp.where(left, colf_bc[0], colf_bc[1]))
                         + csb[:, pc] * jnp.exp2(jnp.where(left, colb_bc[0], colb_bc[1])))
                pairs.append(y_diag + y_off)
            yg = jnp.concatenate(pairs, axis=1) + dskip_ref[:, gc] * xg.astype(F32)
            vg = yg * z_ref[rows, gc].astype(F32)
            ms = jnp.mean(vg * vg, axis=-1, keepdims=True)
            ys_ref[rows, gc] = (vg * lax.rsqrt(ms + EPS) * gn_ref[:, gc]).astype(BF16)

        acc = (jnp.dot(ys_ref[rows, :], w_ref[0:D_SSM, :], preferred_element_type=F32)
               + jnp.dot(ya_ref[rows, :], w_ref[D_SSM:D_MIX, :], preferred_element_type=F32))
        hres = x_ref[rows, :] + acc
        ms = jnp.mean(hres * hres, axis=-1, keepdims=True)
        o_ref[rows, :] = hres * lax.rsqrt(ms + EPS) * fg_ref[...]


def _ssd_out(a_log, dskip_x, ssd_norm_g, selb, tril, xs, bm, cm, dt, z, sf, sb, y_attn, x2, w_out, final_norm_g):
    n = xs.shape[0]
    rows = Q * CHUNK
    row = lambda i: (i, 0)
    const = lambda i: (0, 0)
    return pl.pallas_call(
        _ssd_out_kernel,
        grid=(n // rows,),
        in_specs=[
            pl.BlockSpec((1, LANES), const),
            pl.BlockSpec((1, D_SSM), const),
            pl.BlockSpec((1, D_SSM), const),
            pl.BlockSpec((Q, 2 * LANES, SSM_HEADS * CHUNK), lambda i: (0, 0, 0), pipeline_mode=pl.Buffered(1)),
            pl.BlockSpec((CHUNK, CHUNK), const),
            pl.BlockSpec((rows, D_SSM), row),
            pl.BlockSpec((rows, SSM_GROUPS * D_STATE), row),
            pl.BlockSpec((rows, SSM_GROUPS * D_STATE), row),
            pl.BlockSpec((CHUNK, LANES), lambda i: (0, i)),
            pl.BlockSpec((rows, D_SSM), row),
            pl.BlockSpec((Q, SSM_GROUPS * D_STATE, GROUP_W), lambda i: (i, 0, 0)),
            pl.BlockSpec((Q, SSM_GROUPS * D_STATE, GROUP_W), lambda i: (i, 0, 0)),
            pl.BlockSpec((rows, D_ATTN), row),
            pl.BlockSpec((rows, D_MODEL), row),
            pl.BlockSpec((D_MIX, D_MODEL), const, pipeline_mode=pl.Buffered(1)),
            pl.BlockSpec((1, D_MODEL), const),
        ],
        out_specs=pl.BlockSpec((rows, D_MODEL), row),
        out_shape=jax.ShapeDtypeStruct((n, D_MODEL), F32),
        scratch_shapes=[pltpu.VMEM((rows, D_SSM), BF16)],
        compiler_params=pltpu.CompilerParams(dimension_semantics=("parallel",),
                                             vmem_limit_bytes=VMEM_LIMIT),
        name="ssd_out",
    )(a_log, dskip_x, ssd_norm_g, selb, tril, xs, bm, cm, dt, z, sf, sb, y_attn, x2, w_out, final_norm_g)


def _bucket_table_t():
    rel = jnp.arange(3 * BLK)[:, None] - BLK - jnp.arange(BLK)[None, :]
    half = NUM_BUCKETS // 2
    ret = (rel > 0).astype(jnp.int32) * half
    n = jnp.abs(rel)
    nf = jnp.maximum(n, 1).astype(F32)
    large = MAX_EXACT + (jnp.log(nf / MAX_EXACT) / math.log(MAX_DISTANCE / MAX_EXACT)
                         * (half - MAX_EXACT)).astype(jnp.int32)
    large = jnp.minimum(large, half - 1)
    return (ret + jnp.where(n < MAX_EXACT, n, large)).astype(jnp.int32)


def _attn_kernel(nblk, ntot, rb_ref, bucket_ref, sink_ref, gn_ref, q_ref, kp_ref, kc_ref, kn_ref,
                 vp_ref, vc_ref, vn_ref, ga_ref, y_ref,
                 tab_ref, pe_ref, mk_ref, ot_ref, vs_ref):
    g = pl.program_id(0)

    @pl.when(g == 0)
    def _init():
        pe_ref[...] = jnp.zeros_like(pe_ref)
        mk_ref[...] = jnp.zeros_like(mk_ref)
        ot_ref[...] = jnp.zeros_like(ot_ref)
        bk = bucket_ref[...]
        t = lax.broadcasted_iota(jnp.int32, (3 * BLK, BLK), 0)
        qq = lax.broadcasted_iota(jnp.int32, (3 * BLK, BLK), 1)
        in_window = jnp.abs(t - BLK - qq) <= WINDOW

        def per_head(h, carry):
            acc = jnp.zeros((3 * BLK, BLK), F32)
            for b in range(NUM_BUCKETS):
                acc = jnp.where(bk == b, rb_ref[b, h], acc)
            tab = jnp.where(in_window, acc * LOG2E, NEG_INF)
            tab_ref[0, h] = tab
            tab_ref[1, h] = jnp.where(t >= BLK, tab, NEG_INF)
            tab_ref[2, h] = jnp.where(t < 2 * BLK, tab, NEG_INF)
            return carry

        lax.fori_loop(0, ATTN_HEADS, per_head, 0)

    left = lax.broadcasted_iota(jnp.int32, (BLK, LANES), 1) < ATTN_HEAD_DIM
    left3 = lax.broadcasted_iota(jnp.int32, (3 * BLK, LANES), 1) < ATTN_HEAD_DIM
    s2 = sink_ref[...] * LOG2E
    def run(even, odd):
        ia = lax.rem(jnp.minimum(g, ntot - 1), nblk)
        var = jnp.where(ia == 0, 1, jnp.where(ia == nblk - 1, 2, 0))
        kcat = jnp.concatenate([kp_ref[...], kc_ref[...], kn_ref[...]], axis=0)
        vcat = jnp.concatenate([vp_ref[...], vc_ref[...], vn_ref[...]], axis=0)
        one = jnp.ones((3 * BLK, LANES), BF16)

        def stage_a(p):
            kd = kcat[:, (p // 2) * LANES:(p // 2 + 1) * LANES]
            qt = q_ref[:, p * LANES:(p + 1) * LANES]
            zero = jnp.zeros_like(qt)
            rhs_t = jnp.concatenate([jnp.where(left, qt, zero), jnp.where(left, zero, qt)], axis=0)
            lg = lax.dot_general(kd, rhs_t, (((1,), (1,)), ((), ())), preferred_element_type=F32)
            lg = lg + jnp.concatenate([tab_ref[var, 2 * p], tab_ref[var, 2 * p + 1]], axis=1)
            for kb in range(3):
                blk = lg[kb * BLK:(kb + 1) * BLK]
                mk = jnp.max(blk, axis=0, keepdims=True)
                pe_ref[even, p, kb * BLK:(kb + 1) * BLK, :] = jnp.exp2(blk - mk).astype(BF16)
                mk_ref[even, p, kb:kb + 1, :] = mk

        def stage_b(p):
            kv = p // 2
            vt = vcat[:, (kv // 2) * LANES:(kv // 2 + 1) * LANES]
            vmod = jnp.where(left3, vt, one) if kv % 2 == 0 else jnp.where(left3, one, vt)
            r0 = (kv % 2) * ATTN_HEAD_DIM
            d0 = ATTN_HEAD_DIM - r0
            mk = mk_ref[odd, p]
            sp = s2[p:p + 1, :]
            m = jnp.maximum(jnp.max(mk[0:3], axis=0, keepdims=True), sp)
            resc = jnp.exp2(mk[0:3] - m).astype(BF16)
            pes = jnp.concatenate([pe_ref[odd, p, kb * BLK:(kb + 1) * BLK, :] * resc[kb:kb + 1]
                                   for kb in range(3)], axis=0)
            o2 = lax.dot_general(vmod, pes, (((0,), (0,)), ((), ())), preferred_element_type=F32)
            o = o2[r0:r0 + ATTN_HEAD_DIM, :] / (o2[d0:d0 + 1, :] + jnp.exp2(sp - m))
            ot_ref[odd, 2 * p * ATTN_HEAD_DIM:(2 * p + 1) * ATTN_HEAD_DIM, :] = o[:, 0:BLK]
            ot_ref[odd, (2 * p + 1) * ATTN_HEAD_DIM:(2 * p + 2) * ATTN_HEAD_DIM, :] = o[:, BLK:2 * BLK]

        def stage_c(p, ssq):
            cols = slice(p * LANES, (p + 1) * LANES)
            v = ot_ref[even, p * LANES:(p + 1) * LANES, :].T * ga_ref[:, cols].astype(F32)
            vs_ref[:, cols] = v
            return ssq + jnp.sum(v * v, axis=-1, keepdims=True)

        ssq = jnp.zeros((BLK, 1), F32)
        for p in range(ATTN_HEADS // 2):
            stage_a(p)
            stage_b(p)
            ssq = stage_c(p, ssq)
        y_ref[...] = (vs_ref[...] * lax.rsqrt(ssq * (1.0 / D_ATTN) + EPS) * gn_ref[...]).astype(BF16)

    @pl.when(lax.rem(g, 2) == 0)
    def _():
        run(0, 1)

    @pl.when(lax.rem(g, 2) == 1)
    def _():
        run(1, 0)


def _attn(rel_bias, bucket_t, sink_x, attn_norm_g, q, kd, v, ga, batch, seq):
    n = q.shape[0]
    nblk = seq // BLK
    ntot = batch * nblk

    def blk(lag):
        return lambda g: jnp.clip(g - lag, 0, ntot - 1)

    def cur(lag):
        return lambda g: (blk(lag)(g), 0)

    def prev(lag):
        return lambda g: (blk(lag)(g) - jnp.where(lax.rem(blk(lag)(g), nblk) > 0, 1, 0), 0)

    def nxt(lag):
        return lambda g: (blk(lag)(g) + jnp.where(lax.rem(blk(lag)(g), nblk) < nblk - 1, 1, 0), 0)

    const = lambda g: (0, 0)
    kw = 2 * KV_HEADS * ATTN_HEAD_DIM
    vw = KV_HEADS * ATTN_HEAD_DIM
    npair = ATTN_HEADS // 2
    return pl.pallas_call(
        lambda *refs: _attn_kernel(nblk, ntot, *refs),
        grid=(ntot + 2,),
        in_specs=[
            pl.BlockSpec(memory_space=pltpu.SMEM),
            pl.BlockSpec((3 * BLK, BLK), const),
            pl.BlockSpec((npair, 2 * BLK), const),
            pl.BlockSpec((1, D_ATTN), const),
            pl.BlockSpec((BLK, D_ATTN), cur(0)),
            pl.BlockSpec((BLK, kw), prev(0)), pl.BlockSpec((BLK, kw), cur(0)), pl.BlockSpec((BLK, kw), nxt(0)),
            pl.BlockSpec((BLK, vw), prev(1)), pl.BlockSpec((BLK, vw), cur(1)), pl.BlockSpec((BLK, vw), nxt(1)),
            pl.BlockSpec((BLK, D_ATTN), cur(2)),
        ],
        out_specs=pl.BlockSpec((BLK, D_ATTN), cur(2)),
        out_shape=jax.ShapeDtypeStruct((n, D_ATTN), BF16),
        scratch_shapes=[pltpu.VMEM((3, ATTN_HEADS, 3 * BLK, BLK), F32),
                        pltpu.VMEM((2, npair, 3 * BLK, 2 * BLK), BF16),
                        pltpu.VMEM((2, npair, 8, 2 * BLK), F32),
                        pltpu.VMEM((2, D_ATTN, BLK), F32),
                        pltpu.VMEM((BLK, D_ATTN), F32)],

        compiler_params=pltpu.CompilerParams(dimension_semantics=("arbitrary",),
                                             vmem_limit_bytes=VMEM_LIMIT),
        name="attn",
    )(rel_bias, bucket_t, sink_x, attn_norm_g, q, kd, kd, kd, v, v, v, ga)


def _arrange_w_in(w_in):
    sizes = [D_SSM, CONV_CH, 2 * SSM_HEADS, D_ATTN, KV_HEADS * ATTN_HEAD_DIM,
             KV_HEADS * ATTN_HEAD_DIM, D_ATTN]
    wz, wxbc, wdt, wq, wk, wv, wga = jnp.split(w_in, [int(s) for s in np.cumsum(sizes)[:-1]], axis=1)
    wk = wk.reshape(D_MODEL, KV_HEADS, 1, ATTN_HEAD_DIM)
    wkd = jnp.broadcast_to(wk, (D_MODEL, KV_HEADS, 2, ATTN_HEAD_DIM)).reshape(D_MODEL, -1)
    wdt = jnp.pad(wdt, ((0, 0), (0, LANES - 2 * SSM_HEADS)))
    return jnp.concatenate([wz, wxbc, wq, wkd, wv, wga, wdt], axis=1).astype(BF16)


def _expand_matrices():
    e = np.zeros((Q, 2, 2 * LANES, D_SSM), np.float32)
    for c in range(Q):
        for d in range(2):
            for h in range(SSM_HEADS):
                r = (c * 2 + d) * SSM_HEADS + h
                e[c, d, r, h * SSM_HEAD_DIM:(h + 1) * SSM_HEAD_DIM] = 1.0
                e[c, d, LANES + r, h * SSM_HEAD_DIM:(h + 1) * SSM_HEAD_DIM] = 1.0
    return e


def _bcast_matrices():
    m = np.zeros((Q, 2 * LANES, SSM_HEADS * CHUNK), np.float32)
    for c in range(Q):
        for h in range(SSM_HEADS):
            r = (c * 2 + 1) * SSM_HEADS + h
            m[c, r, h * CHUNK:(h + 1) * CHUNK] = 1.0
            m[c, LANES + r, h * CHUNK:(h + 1) * CHUNK] = 1.0
    return m


def kernel(x, norm_in_g, w_in, conv_w, conv_b, dt_bias, a_log, d_skip, ssd_norm_g, rel_bias, sink,
           attn_norm_g, w_out, final_norm_g):
    batch, seq, _ = x.shape
    assert w_out.shape[0] == 1 and seq % (Q * CHUNK) == 0 and seq % TM == 0
    x2 = x.reshape(batch * seq, D_MODEL)

    w_all = _arrange_w_in(w_in)
    e = jnp.asarray(_expand_matrices(), BF16)
    tril = jnp.asarray(np.tril(np.ones((CHUNK, CHUNK), np.float32)), BF16)
    a_log2 = jnp.tile(a_log.reshape(1, 2 * SSM_HEADS).astype(F32), (1, Q))
    dt_bias_x = jnp.pad(dt_bias.reshape(1, 2 * SSM_HEADS).astype(F32), ((0, 0), (0, LANES - 2 * SSM_HEADS)))
    dskip_x = jnp.repeat(d_skip.astype(F32), SSM_HEAD_DIM).reshape(1, D_SSM)
    sink_x = jnp.repeat(sink.astype(F32), BLK).reshape(ATTN_HEADS // 2, 2 * BLK)

    z, xs, bm, cm, dt, q, kd, v, ga = _inproj(
        x2, norm_in_g.reshape(1, D_MODEL), w_all, conv_w, conv_b.reshape(1, CONV_CH),
        dt_bias_x, seq)
    sf, sb = _states(a_log2, e, tril, xs, bm, dt, batch, seq)
    y_attn = _attn(rel_bias.astype(F32), _bucket_table_t(), sink_x,
                   attn_norm_g.reshape(1, D_ATTN), q, kd, v, ga, batch, seq)
    out = _ssd_out(a_log2, dskip_x, ssd_norm_g.reshape(1, D_SSM), jnp.asarray(_bcast_matrices(), BF16),
                   tril, xs, bm, cm, dt, z, sf, sb, y_attn, x2, w_out[0].astype(BF16),
                   final_norm_g.reshape(1, D_MODEL))
    return out.reshape(batch, seq, D_MODEL)
```

```python
import math

import numpy as np
import jax
import jax.numpy as jnp
from jax import lax
from jax.experimental import pallas as pl
from jax.experimental.pallas import tpu as pltpu

F32 = jnp.float32
BF16 = jnp.bfloat16

D_MODEL = 1024
D_SSM = 1024
D_ATTN = 1024
D_MIX = D_SSM + D_ATTN
SSM_HEAD_DIM = 64
SSM_HEADS = D_SSM // SSM_HEAD_DIM
SSM_GROUPS = 2
GROUP_W = D_SSM // SSM_GROUPS
D_STATE = 128
D_CONV = 5
CHUNK = 128
CONV_CH = D_SSM + 2 * SSM_GROUPS * D_STATE
ATTN_HEAD_DIM = 64
ATTN_HEADS = D_ATTN // ATTN_HEAD_DIM
KV_HEADS = 4
WINDOW = 128
BLK = 128
NUM_BUCKETS = 32
MAX_DISTANCE = 128
MAX_EXACT = 8
EPS = 1e-6
NEG_INF = -1e30
SCALE = ATTN_HEAD_DIM ** -0.5
LOG2E = math.log2(math.e)

LANES = 128
HALO = 8
TM = 512
CONV_ROWS = 64
ATTN_NB = 4
Q = 4
VMEM_LIMIT = 56 * 1024 * 1024

OFF_Z = 0
OFF_XBC = OFF_Z + D_SSM
OFF_Q = OFF_XBC + CONV_CH
OFF_KD = OFF_Q + D_ATTN
OFF_V = OFF_KD + 2 * KV_HEADS * ATTN_HEAD_DIM
OFF_GA = OFF_V + KV_HEADS * ATTN_HEAD_DIM
OFF_DT = OFF_GA + D_ATTN
W_COLS = OFF_DT + LANES
assert TM == Q * CHUNK and Q * 2 * SSM_HEADS == LANES


def _silu(v):
    return v * jax.nn.sigmoid(v)


def _softplus(v):
    return jnp.maximum(v, 0.0) + jnp.log1p(jnp.exp(-jnp.abs(v)))


def _inproj_kernel(tiles_per_seq, x_ref, g_ref, w_ref, cw_ref, cb_ref, dtb_ref,
                   z_ref, xs_ref, bm_ref, cm_ref, dt_ref, q_ref, kd_ref, v_ref, ga_ref, pad_ref):
    i = pl.program_id(0)

    @pl.when(i == 0)
    def _():
        pad_ref[...] = jnp.zeros_like(pad_ref)

    j = lax.rem(i, tiles_per_seq)
    g = g_ref[...]

    def proj(hv, off, n):
        return jnp.dot(hv, w_ref[:, off:off + n], preferred_element_type=F32)

    xv = x_ref[...]
    ms = jnp.mean(xv * xv, axis=-1, keepdims=True)
    h = (xv * lax.rsqrt(ms + EPS) * g).astype(BF16)

    s_cur = lax.rem(i, 3)
    s_prev = lax.rem(i + 2, 3)
    s_next = lax.rem(i + 1, 3)
    xbc = proj(h, OFF_XBC, CONV_CH)
    pad_ref[s_cur, HALO:HALO + TM, :] = xbc
    pad_ref[s_prev, HALO + TM:2 * HALO + TM, :] = xbc[0:HALO] * jnp.where(j > 0, 1.0, 0.0)
    pad_ref[s_next, 0:HALO, :] = xbc[TM - HALO:TM] * jnp.where(j < tiles_per_seq - 1, 1.0, 0.0)

    z_ref[...] = _silu(proj(h, OFF_Z, D_SSM)).astype(BF16)
    q_ref[...] = (proj(h, OFF_Q, D_ATTN) * (SCALE * LOG2E)).astype(BF16)
    kd_ref[...] = proj(h, OFF_KD, 2 * KV_HEADS * ATTN_HEAD_DIM).astype(BF16)
    v_ref[...] = proj(h, OFF_V, KV_HEADS * ATTN_HEAD_DIM).astype(BF16)
    ga_ref[...] = _silu(proj(h, OFF_GA, D_ATTN)).astype(BF16)
    dt_lane = lax.broadcasted_iota(jnp.int32, (TM, LANES), 1) < 2 * SSM_HEADS
    dt = jnp.where(dt_lane, _softplus(proj(h, OFF_DT, LANES) + dtb_ref[...]), 0.0)
    dtc = dt[0:CHUNK]
    for c in range(1, Q):
        dtc = dtc + pltpu.roll(dt[c * CHUNK:(c + 1) * CHUNK], 2 * SSM_HEADS * c, axis=1)
    dt_ref[...] = dtc

    base = HALO - (D_CONV - 1) // 2
    for cc in range(CONV_CH // LANES):
        sl = slice(cc * LANES, (cc + 1) * LANES)
        if cc < D_SSM // LANES:
            dst, o = xs_ref, cc * LANES
        elif cc < (D_SSM + SSM_GROUPS * D_STATE) // LANES:
            dst, o = bm_ref, cc * LANES - D_SSM
        else:
            dst, o = cm_ref, cc * LANES - D_SSM - SSM_GROUPS * D_STATE
        for r0 in range(0, TM, CONV_ROWS):
            acc = cb_ref[:, sl] + cw_ref[0:1, sl] * pad_ref[s_prev, base + r0:base + r0 + CONV_ROWS, sl]
            for k in range(1, D_CONV):
                acc = acc + cw_ref[k:k + 1, sl] * pad_ref[s_prev, base + r0 + k:base + r0 + k + CONV_ROWS, sl]
            dst[r0:r0 + CONV_ROWS, o:o + LANES] = _silu(acc).astype(BF16)


def _inproj(x2, norm_in_g, w_all, conv_w, conv_b, dt_bias, seq):
    n = x2.shape[0]
    tiles_per_seq = seq // TM
    ntile = n // TM
    cur = lambda i: (jnp.minimum(i, ntile - 1), 0)
    lag = lambda i: (jnp.maximum(i - 1, 0), 0)
    const = lambda i: (0, 0)
    out_cols = (D_SSM, D_SSM, SSM_GROUPS * D_STATE, SSM_GROUPS * D_STATE,
                D_ATTN, 2 * KV_HEADS * ATTN_HEAD_DIM, KV_HEADS * ATTN_HEAD_DIM, D_ATTN)
    out_maps = (cur, lag, lag, lag, cur, cur, cur, cur)
    out_specs = [pl.BlockSpec((TM, c), m) for c, m in zip(out_cols, out_maps)]
    out_shape = [jax.ShapeDtypeStruct((n, c), BF16) for c in out_cols]
    out_specs.insert(4, pl.BlockSpec((CHUNK, LANES), lambda i: (0, jnp.minimum(i, ntile - 1))))
    out_shape.insert(4, jax.ShapeDtypeStruct((CHUNK, ntile * LANES), F32))
    return pl.pallas_call(
        lambda *refs: _inproj_kernel(tiles_per_seq, *refs),
        grid=(ntile + 1,),
        in_specs=[
            pl.BlockSpec((TM, D_MODEL), cur),
            pl.BlockSpec((1, D_MODEL), const),
            pl.BlockSpec((D_MODEL, W_COLS), const, pipeline_mode=pl.Buffered(1)),
            pl.BlockSpec((D_CONV, CONV_CH), const),
            pl.BlockSpec((1, CONV_CH), const),
            pl.BlockSpec((1, LANES), const),
        ],
        out_specs=out_specs,
        out_shape=out_shape,
        scratch_shapes=[pltpu.VMEM((3, TM + 2 * HALO, CONV_CH), F32)],
        compiler_params=pltpu.CompilerParams(dimension_semantics=("arbitrary",),
                                             vmem_limit_bytes=VMEM_LIMIT),
        name="inproj",
    )(x2, norm_in_g, w_all, conv_w, conv_b, dt_bias)


def _split_terms(v, n):
    terms, r = [], v
    for _ in range(n):
        t = r.astype(BF16)
        terms.append(t)
        r = r - t.astype(F32)
    return terms


def _chunk_cumsum(tril, a):
    t0, t1, t2 = _split_terms(a, 3)
    return (jnp.dot(tril, t0, preferred_element_type=F32) + jnp.dot(tril, t1, preferred_element_type=F32)
            + jnp.dot(tril, t2, preferred_element_type=F32))


def _hi_lo(v):
    return jnp.concatenate(_split_terms(v, 2), axis=1)


def _fwd_lanes():
    lane = lax.broadcasted_iota(jnp.int32, (CHUNK, LANES), 1)
    return jnp.bitwise_and(lane, 2 * SSM_HEADS - 1) < SSM_HEADS


def _states_kernel(alog_ref, e_ref, tril_ref, xsf_ref, bf_ref, dtf_ref, xsb_ref, bb_ref, dtb_ref,
                   sf_ref, sb_ref, cf_ref, cb_ref):
    @pl.when(pl.program_id(1) == 0)
    def _():
        cf_ref[...] = jnp.zeros_like(cf_ref)
        cb_ref[...] = jnp.zeros_like(cb_ref)

    a_neg = -jnp.exp(alog_ref[...])
    tril = tril_ref[...]
    fwd_lane = _fwd_lanes()

    def prep(dt_ref):
        dt = dt_ref[...]
        a = dt * a_neg
        cs = _chunk_cumsum(tril, a)
        tot = cs[CHUNK - 1:CHUNK, :]
        w = dt * jnp.exp(jnp.where(fwd_lane, tot - cs, cs - a))
        return _hi_lo(jnp.concatenate([w, jnp.broadcast_to(jnp.exp(tot), (8, LANES))], axis=0))

    def chunk(c, d, cat, xs_ref, b_ref, out_ref, carry_ref):
        rows = slice(c * CHUNK, (c + 1) * CHUNK)
        wx = jnp.dot(cat, e_ref[c, d], preferred_element_type=F32)
        xd = (xs_ref[rows, :].astype(F32) * wx[0:CHUNK]).astype(BF16)
        dec = wx[CHUNK:CHUNK + 1, :]
        for g in range(SSM_GROUPS):
            gr = slice(g * D_STATE, (g + 1) * D_STATE)
            gc = slice(g * GROUP_W, (g + 1) * GROUP_W)
            st = lax.dot_general(b_ref[rows, gr], xd[:, gc], (((0,), (0,)), ((), ())),
                                 preferred_element_type=F32)
            prev = carry_ref[gr, :]
            out_ref[c, gr, :] = prev.astype(BF16)
            carry_ref[gr, :] = prev * dec[:, gc] + st

    cat_f = prep(dtf_ref)
    cat_b = prep(dtb_ref)
    for c in range(Q):
        chunk(c, 0, cat_f, xsf_ref, bf_ref, sf_ref, cf_ref)
    for c in reversed(range(Q)):
        chunk(c, 1, cat_b, xsb_ref, bb_ref, sb_ref, cb_ref)


def _states(a_log, e, tril, xs, bm, dt, batch, seq):
    nstep = seq // (Q * CHUNK)
    nchunk = batch * seq // CHUNK
    fwd = lambda b, i: (b * nstep + i, 0)
    bwd = lambda b, i: (b * nstep + nstep - 1 - i, 0)
    const = lambda b, i: (0, 0)
    rows = Q * CHUNK
    st_shape = jax.ShapeDtypeStruct((nchunk, SSM_GROUPS * D_STATE, GROUP_W), BF16)
    return pl.pallas_call(
        _states_kernel,
        grid=(batch, nstep),
        in_specs=[
            pl.BlockSpec((1, LANES), const),
            pl.BlockSpec((Q, 2, 2 * LANES, D_SSM), lambda b, i: (0, 0, 0, 0), pipeline_mode=pl.Buffered(1)),
            pl.BlockSpec((CHUNK, CHUNK), const),
            pl.BlockSpec((rows, D_SSM), fwd),
            pl.BlockSpec((rows, SSM_GROUPS * D_STATE), fwd),
            pl.BlockSpec((CHUNK, LANES), lambda b, i: (0, b * nstep + i)),
            pl.BlockSpec((rows, D_SSM), bwd),
            pl.BlockSpec((rows, SSM_GROUPS * D_STATE), bwd),
            pl.BlockSpec((CHUNK, LANES), lambda b, i: (0, b * nstep + nstep - 1 - i)),
        ],
        out_specs=[
            pl.BlockSpec((Q, SSM_GROUPS * D_STATE, GROUP_W), lambda b, i: (b * nstep + i, 0, 0)),
            pl.BlockSpec((Q, SSM_GROUPS * D_STATE, GROUP_W), lambda b, i: (b * nstep + nstep - 1 - i, 0, 0)),
        ],
        out_shape=[st_shape, st_shape],
        scratch_shapes=[pltpu.VMEM((SSM_GROUPS * D_STATE, GROUP_W), F32),
                        pltpu.VMEM((SSM_GROUPS * D_STATE, GROUP_W), F32)],
        compiler_params=pltpu.CompilerParams(dimension_semantics=("arbitrary", "arbitrary"),
                                             vmem_limit_bytes=VMEM_LIMIT),
        name="ssd_states",
    )(a_log, e, tril, xs, bm, dt, xs, bm, dt)


def _ssd_out_kernel(alog_ref, dskip_ref, gn_ref, selb_ref, tril_ref, xs_ref, bm_ref, cm_ref, dt_ref,
                    z_ref, sf_ref, sb_ref, ya_ref, x_ref, w_ref, fg_ref, o_ref, ys_ref):
    li = lax.broadcasted_iota(jnp.int32, (CHUNK, CHUNK), 0)
    si = lax.broadcasted_iota(jnp.int32, (CHUNK, CHUNK), 1)
    lower = li >= si
    eye = li == si
    left = lax.broadcasted_iota(jnp.int32, (CHUNK, LANES), 1) < SSM_HEAD_DIM

    dt = dt_ref[...]
    a = dt * (-LOG2E * jnp.exp(alog_ref[...]))
    cs = _chunk_cumsum(tril_ref[...], a)
    tot = cs[CHUNK - 1:CHUNK, :]
    colq = jnp.where(_fwd_lanes(), cs, tot - cs + a)
    rowq = (colq - jnp.log2(dt)).T
    dtr = dt.T.astype(BF16)
    catq = _hi_lo(colq)

    for c in range(Q):
        rows = slice(c * CHUNK, (c + 1) * CHUNK)
        colb = jnp.dot(catq, selb_ref[c], preferred_element_type=F32)
        for g in range(SSM_GROUPS):
            gr = slice(g * D_STATE, (g + 1) * D_STATE)
            gc = slice(g * GROUP_W, (g + 1) * GROUP_W)
            cmg = cm_ref[rows, gr]
            cb = lax.dot_general(cmg, bm_ref[rows, gr], (((1,), (1,)), ((), ())),
                                 preferred_element_type=F32)
            cbb = cb.astype(BF16)
            cbeye = jnp.where(eye, cb, 0.0).astype(BF16)
            csf = jnp.dot(cmg, sf_ref[c, gr, :], preferred_element_type=F32)
            csb = jnp.dot(cmg, sb_ref[c, gr, :], preferred_element_type=F32)
            xg = xs_ref[rows, gc]
            pairs = []
            for pp in range(GROUP_W // LANES):
                pc = slice(pp * LANES, (pp + 1) * LANES)
                scores, colf_bc, colb_bc = [], [], []
                for hh in range(2):
                    hd = g * (SSM_HEADS // SSM_GROUPS) + pp * 2 + hh
                    h = c * 2 * SSM_HEADS + hd
                    hb = h + SSM_HEADS
                    colf_bc.append(jnp.broadcast_to(colq[:, h:h + 1], (CHUNK, CHUNK)))
                    colb_bc.append(colb[:, hd * CHUNK:(hd + 1) * CHUNK])
                    arg = jnp.where(lower, colf_bc[hh] - rowq[h:h + 1, :], colb_bc[hh] - rowq[hb:hb + 1, :])
                    scores.append(cbb * jnp.exp2(arg).astype(BF16) + cbeye * dtr[hb:hb + 1, :])
                xt = xg[:, pc]
                zero = jnp.zeros_like(xt)
                rhs = jnp.concatenate([jnp.where(left, xt, zero), jnp.where(left, zero, xt)], axis=0)
                y_diag = jnp.dot(jnp.concatenate(scores, axis=1), rhs, preferred_element_type=F32)
                y_off = (csf[:, pc] * jnp.exp2(jnp.where(left, colf_bc[0], colf_bc[1]))
                         + csb[:, pc] * jnp.exp2(jnp.where(left, colb_bc[0], colb_bc[1])))
                pairs.append(y_diag + y_off)
            yg = jnp.concatenate(pairs, axis=1) + dskip_ref[:, gc] * xg.astype(F32)
            vg = yg * z_ref[rows, gc].astype(F32)
            ms = jnp.mean(vg * vg, axis=-1, keepdims=True)
            ys_ref[rows, gc] = (vg * lax.rsqrt(ms + EPS) * gn_ref[:, gc]).astype(BF16)

        acc = (jnp.dot(ys_ref[rows, :], w_ref[0:D_SSM, :], preferred_element_type=F32)
               + jnp.dot(ya_ref[rows, :], w_ref[D_SSM:D_MIX, :], preferred_element_type=F32))
        hres = x_ref[rows, :] + acc
        ms = jnp.mean(hres * hres, axis=-1, keepdims=True)
        o_ref[rows, :] = hres * lax.rsqrt(ms + EPS) * fg_ref[...]


def _ssd_out(a_log, dskip_x, ssd_norm_g, selb, tril, xs, bm, cm, dt, z, sf, sb, y_attn, x2, w_out, final_norm_g):
    n = xs.shape[0]
    rows = Q * CHUNK
    row = lambda i: (i, 0)
    const = lambda i: (0, 0)
    return pl.pallas_call(
        _ssd_out_kernel,
        grid=(n // rows,),
        in_specs=[
            pl.BlockSpec((1, LANES), const),
            pl.BlockSpec((1, D_SSM), const),
            pl.BlockSpec((1, D_SSM), const),
            pl.BlockSpec((Q, 2 * LANES, SSM_HEADS * CHUNK), lambda i: (0, 0, 0), pipeline_mode=pl.Buffered(1)),
            pl.BlockSpec((CHUNK, CHUNK), const),
            pl.BlockSpec((rows, D_SSM), row),
            pl.BlockSpec((rows, SSM_GROUPS * D_STATE), row),
            pl.BlockSpec((rows, SSM_GROUPS * D_STATE), row),
            pl.BlockSpec((CHUNK, LANES), lambda i: (0, i)),
            pl.BlockSpec((rows, D_SSM), row),
            pl.BlockSpec((Q, SSM_GROUPS * D_STATE, GROUP_W), lambda i: (i, 0, 0)),
            pl.BlockSpec((Q, SSM_GROUPS * D_STATE, GROUP_W), lambda i: (i, 0, 0)),
            pl.BlockSpec((rows, D_ATTN), row),
            pl.BlockSpec((rows, D_MODEL), row),
            pl.BlockSpec((D_MIX, D_MODEL), const, pipeline_mode=pl.Buffered(1)),
            pl.BlockSpec((1, D_MODEL), const),
        ],
        out_specs=pl.BlockSpec((rows, D_MODEL), row),
        out_shape=jax.ShapeDtypeStruct((n, D_MODEL), F32),
        scratch_shapes=[pltpu.VMEM((rows, D_SSM), BF16)],
        compiler_params=pltpu.CompilerParams(dimension_semantics=("parallel",),
                                             vmem_limit_bytes=VMEM_LIMIT),
        name="ssd_out",
    )(a_log, dskip_x, ssd_norm_g, selb, tril, xs, bm, cm, dt, z, sf, sb, y_attn, x2, w_out, final_norm_g)


def _bucket_table_t():
    rel = jnp.arange(3 * BLK)[:, None] - BLK - jnp.arange(BLK)[None, :]
    half = NUM_BUCKETS // 2
    ret = (rel > 0).astype(jnp.int32) * half
    n = jnp.abs(rel)
    nf = jnp.maximum(n, 1).astype(F32)
    large = MAX_EXACT + (jnp.log(nf / MAX_EXACT) / math.log(MAX_DISTANCE / MAX_EXACT)
                         * (half - MAX_EXACT)).astype(jnp.int32)
    large = jnp.minimum(large, half - 1)
    return (ret + jnp.where(n < MAX_EXACT, n, large)).astype(jnp.int32)


def _attn_kernel(nblk, nstep, rb_ref, bucket_ref, sink_ref, gn_ref, q_ref, kp_ref, kc_ref, kn_ref,
                 vp_ref, vc_ref, vn_ref, ga_ref, y_ref,
                 tab_ref, pe_ref, mk_ref, ot_ref, vs_ref):
    g = pl.program_id(0)

    @pl.when(g == 0)
    def _init():
        pe_ref[...] = jnp.zeros_like(pe_ref)
        mk_ref[...] = jnp.zeros_like(mk_ref)
        ot_ref[...] = jnp.zeros_like(ot_ref)
        bk = bucket_ref[...]
        t = lax.broadcasted_iota(jnp.int32, (3 * BLK, BLK), 0)
        qq = lax.broadcasted_iota(jnp.int32, (3 * BLK, BLK), 1)
        in_window = jnp.abs(t - BLK - qq) <= WINDOW

        def per_head(h, carry):
            acc = jnp.zeros((3 * BLK, BLK), F32)
            for b in range(NUM_BUCKETS):
                acc = jnp.where(bk == b, rb_ref[b, h], acc)
            tab = jnp.where(in_window, acc * LOG2E, NEG_INF)
            tab_ref[0, h] = tab
            tab_ref[1, h] = jnp.where(t >= BLK, tab, NEG_INF)
            tab_ref[2, h] = jnp.where(t < 2 * BLK, tab, NEG_INF)
            return carry

        lax.fori_loop(0, ATTN_HEADS, per_head, 0)

    left = lax.broadcasted_iota(jnp.int32, (BLK, LANES), 1) < ATTN_HEAD_DIM
    left3 = lax.broadcasted_iota(jnp.int32, (3 * BLK, LANES), 1) < ATTN_HEAD_DIM
    s2 = sink_ref[...] * LOG2E

    def run(even, odd):
        first_blk = jnp.minimum(g, nstep - 1) * ATTN_NB
        kall = jnp.concatenate([kp_ref[...], kc_ref[...], kn_ref[...]], axis=0)
        vall = jnp.concatenate([vp_ref[...], vc_ref[...], vn_ref[...]], axis=0)
        one = jnp.ones((3 * BLK, LANES), BF16)

        def stage_a(j, p, var):
            kd = kall[j * BLK:(j + 3) * BLK, (p // 2) * LANES:(p // 2 + 1) * LANES]
            qt = q_ref[j * BLK:(j + 1) * BLK, p * LANES:(p + 1) * LANES]
            zero = jnp.zeros_like(qt)
            rhs_t = jnp.concatenate([jnp.where(left, qt, zero), jnp.where(left, zero, qt)], axis=0)
            lg = lax.dot_general(kd, rhs_t, (((1,), (1,)), ((), ())), preferred_element_type=F32)
            lg = lg + jnp.concatenate([tab_ref[var, 2 * p], tab_ref[var, 2 * p + 1]], axis=1)
            for kb in range(3):
                blk = lg[kb * BLK:(kb + 1) * BLK]
                mk = jnp.max(blk, axis=0, keepdims=True)
                pe_ref[even, j, p, kb * BLK:(kb + 1) * BLK, :] = jnp.exp2(blk - mk).astype(BF16)
                mk_ref[even, j, p, kb:kb + 1, :] = mk

        def stage_b(j, p):
            kv = p // 2
            vt = vall[j * BLK:(j + 3) * BLK, (kv // 2) * LANES:(kv // 2 + 1) * LANES]
            vmod = jnp.where(left3, vt, one) if kv % 2 == 0 else jnp.where(left3, one, vt)
            r0 = (kv % 2) * ATTN_HEAD_DIM
            d0 = ATTN_HEAD_DIM - r0
            mk = mk_ref[odd, j, p]
            sp = s2[p:p + 1, :]
            m = jnp.maximum(jnp.max(mk[0:3], axis=0, keepdims=True), sp)
            resc = jnp.exp2(mk[0:3] - m).astype(BF16)
            pes = jnp.concatenate([pe_ref[odd, j, p, kb * BLK:(kb + 1) * BLK, :] * resc[kb:kb + 1]
                                   for kb in range(3)], axis=0)
            o2 = lax.dot_general(vmod, pes, (((0,), (0,)), ((), ())), preferred_element_type=F32)
            o = o2[r0:r0 + ATTN_HEAD_DIM, :] / (o2[d0:d0 + 1, :] + jnp.exp2(sp - m))
            ot_ref[odd, j, 2 * p * ATTN_HEAD_DIM:(2 * p + 1) * ATTN_HEAD_DIM, :] = o[:, 0:BLK]
            ot_ref[odd, j, (2 * p + 1) * ATTN_HEAD_DIM:(2 * p + 2) * ATTN_HEAD_DIM, :] = o[:, BLK:2 * BLK]

        def stage_c(j, p, ssq):
            rows = slice(j * BLK, (j + 1) * BLK)
            cols = slice(p * LANES, (p + 1) * LANES)
            v = ot_ref[even, j, p * LANES:(p + 1) * LANES, :].T * ga_ref[rows, cols].astype(F32)
            vs_ref[rows, cols] = v
            return ssq + jnp.sum(v * v, axis=-1, keepdims=True)

        for j in range(ATTN_NB):
            pos = lax.rem(first_blk + j, nblk)
            var = jnp.where(pos == 0, 1, jnp.where(pos == nblk - 1, 2, 0))
            rows = slice(j * BLK, (j + 1) * BLK)
            ssq = jnp.zeros((BLK, 1), F32)
            for p in range(ATTN_HEADS // 2):
                stage_a(j, p, var)
                stage_b(j, p)
                ssq = stage_c(j, p, ssq)
            y_ref[rows, :] = (vs_ref[rows, :] * lax.rsqrt(ssq * (1.0 / D_ATTN) + EPS) * gn_ref[...]).astype(BF16)

    @pl.when(lax.rem(g, 2) == 0)
    def _():
        run(0, 1)

    @pl.when(lax.rem(g, 2) == 1)
    def _():
        run(1, 0)


def _attn(rel_bias, bucket_t, sink_x, attn_norm_g, q, kd, v, ga, batch, seq):
    n = q.shape[0]
    nblk = seq // BLK
    assert nblk % ATTN_NB == 0
    nstep = batch * nblk // ATTN_NB

    def step(lag):
        return lambda g: jnp.clip(g - lag, 0, nstep - 1)

    def cur(lag):
        return lambda g: (step(lag)(g), 0)

    def prev(lag):
        def index(g):
            b = step(lag)(g) * ATTN_NB
            return (b - jnp.where(lax.rem(b, nblk) > 0, 1, 0), 0)
        return index

    def nxt(lag):
        def index(g):
            b = step(lag)(g) * ATTN_NB + ATTN_NB - 1
            return (b + jnp.where(lax.rem(b, nblk) < nblk - 1, 1, 0), 0)
        return index

    const = lambda g: (0, 0)
    kw = 2 * KV_HEADS * ATTN_HEAD_DIM
    vw = KV_HEADS * ATTN_HEAD_DIM
    npair = ATTN_HEADS // 2
    rows = ATTN_NB * BLK
    return pl.pallas_call(
        lambda *refs: _attn_kernel(nblk, nstep, *refs),
        grid=(nstep + 2,),
        in_specs=[
            pl.BlockSpec(memory_space=pltpu.SMEM),
            pl.BlockSpec((3 * BLK, BLK), const),
            pl.BlockSpec((npair, 2 * BLK), const),
            pl.BlockSpec((1, D_ATTN), const),
            pl.BlockSpec((rows, D_ATTN), cur(0)),
            pl.BlockSpec((BLK, kw), prev(0)), pl.BlockSpec((rows, kw), cur(0)), pl.BlockSpec((BLK, kw), nxt(0)),
            pl.BlockSpec((BLK, vw), prev(1)), pl.BlockSpec((rows, vw), cur(1)), pl.BlockSpec((BLK, vw), nxt(1)),
            pl.BlockSpec((rows, D_ATTN), cur(2)),
        ],
        out_specs=pl.BlockSpec((rows, D_ATTN), cur(2)),
        out_shape=jax.ShapeDtypeStruct((n, D_ATTN), BF16),
        scratch_shapes=[pltpu.VMEM((3, ATTN_HEADS, 3 * BLK, BLK), F32),
                        pltpu.VMEM((2, ATTN_NB, npair, 3 * BLK, 2 * BLK), BF16),
                        pltpu.VMEM((2, ATTN_NB, npair, 8, 2 * BLK), F32),
                        pltpu.VMEM((2, ATTN_NB, D_ATTN, BLK), F32),
                        pltpu.VMEM((rows, D_ATTN), F32)],
        compiler_params=pltpu.CompilerParams(dimension_semantics=("arbitrary",),
                                             vmem_limit_bytes=VMEM_LIMIT),
        name="attn",
    )(rel_bias, bucket_t, sink_x, attn_norm_g, q, kd, kd, kd, v, v, v, ga)


def _arrange_w_in(w_in):
    sizes = [D_SSM, CONV_CH, 2 * SSM_HEADS, D_ATTN, KV_HEADS * ATTN_HEAD_DIM,
             KV_HEADS * ATTN_HEAD_DIM, D_ATTN]
    wz, wxbc, wdt, wq, wk, wv, wga = jnp.split(w_in, [int(s) for s in np.cumsum(sizes)[:-1]], axis=1)
    wk = wk.reshape(D_MODEL, KV_HEADS, 1, ATTN_HEAD_DIM)
    wkd = jnp.broadcast_to(wk, (D_MODEL, KV_HEADS, 2, ATTN_HEAD_DIM)).reshape(D_MODEL, -1)
    wdt = jnp.pad(wdt, ((0, 0), (0, LANES - 2 * SSM_HEADS)))
    return jnp.concatenate([wz, wxbc, wq, wkd, wv, wga, wdt], axis=1).astype(BF16)


def _expand_matrices():
    e = np.zeros((Q, 2, 2 * LANES, D_SSM), np.float32)
    for c in range(Q):
        for d in range(2):
            for h in range(SSM_HEADS):
                r = (c * 2 + d) * SSM_HEADS + h
                e[c, d, r, h * SSM_HEAD_DIM:(h + 1) * SSM_HEAD_DIM] = 1.0
                e[c, d, LANES + r, h * SSM_HEAD_DIM:(h + 1) * SSM_HEAD_DIM] = 1.0
    return e


def _bcast_matrices():
    m = np.zeros((Q, 2 * LANES, SSM_HEADS * CHUNK), np.float32)
    for c in range(Q):
        for h in range(SSM_HEADS):
            r = (c * 2 + 1) * SSM_HEADS + h
            m[c, r, h * CHUNK:(h + 1) * CHUNK] = 1.0
            m[c, LANES + r, h * CHUNK:(h + 1) * CHUNK] = 1.0
    return m


def kernel(x, norm_in_g, w_in, conv_w, conv_b, dt_bias, a_log, d_skip, ssd_norm_g, rel_bias, sink,
           attn_norm_g, w_out, final_norm_g):
    batch, seq, _ = x.shape
    assert w_out.shape[0] == 1 and seq % (Q * CHUNK) == 0 and seq % TM == 0
    x2 = x.reshape(batch * seq, D_MODEL)

    w_all = _arrange_w_in(w_in)
    e = jnp.asarray(_expand_matrices(), BF16)
    tril = jnp.asarray(np.tril(np.ones((CHUNK, CHUNK), np.float32)), BF16)
    a_log2 = jnp.tile(a_log.reshape(1, 2 * SSM_HEADS).astype(F32), (1, Q))
    dt_bias_x = jnp.pad(dt_bias.reshape(1, 2 * SSM_HEADS).astype(F32), ((0, 0), (0, LANES - 2 * SSM_HEADS)))
    dskip_x = jnp.repeat(d_skip.astype(F32), SSM_HEAD_DIM).reshape(1, D_SSM)
    sink_x = jnp.repeat(sink.astype(F32), BLK).reshape(ATTN_HEADS // 2, 2 * BLK)

    z, xs, bm, cm, dt, q, kd, v, ga = _inproj(
        x2, norm_in_g.reshape(1, D_MODEL), w_all, conv_w, conv_b.reshape(1, CONV_CH),
        dt_bias_x, seq)
    sf, sb = _states(a_log2, e, tril, xs, bm, dt, batch, seq)
    y_attn = _attn(rel_bias.astype(F32), _bucket_table_t(), sink_x,
                   attn_norm_g.reshape(1, D_ATTN), q, kd, v, ga, batch, seq)
    out = _ssd_out(a_log2, dskip_x, ssd_norm_g.reshape(1, D_SSM), jnp.asarray(_bcast_matrices(), BF16),
                   tril, xs, bm, cm, dt, z, sf, sb, y_attn, x2, w_out[0].astype(BF16),
                   final_norm_g.reshape(1, D_MODEL))
    return out.reshape(batch, seq, D_MODEL)
```

```python
import math

import numpy as np
import jax
import jax.numpy as jnp
from jax import lax
from jax.experimental import pallas as pl
from jax.experimental.pallas import tpu as pltpu

F32 = jnp.float32
BF16 = jnp.bfloat16

D_MODEL = 1024
D_SSM = 1024
D_ATTN = 1024
D_MIX = D_SSM + D_ATTN
SSM_HEAD_DIM = 64
SSM_HEADS = D_SSM // SSM_HEAD_DIM
SSM_GROUPS = 2
GROUP_W = D_SSM // SSM_GROUPS
D_STATE = 128
D_CONV = 5
CHUNK = 128
CONV_CH = D_SSM + 2 * SSM_GROUPS * D_STATE
ATTN_HEAD_DIM = 64
ATTN_HEADS = D_ATTN // ATTN_HEAD_DIM
KV_HEADS = 4
WINDOW = 128
BLK = 128
NUM_BUCKETS = 32
MAX_DISTANCE = 128
MAX_EXACT = 8
EPS = 1e-6
NEG_INF = -1e30
SCALE = ATTN_HEAD_DIM ** -0.5
LOG2E = math.log2(math.e)

LANES = 128
HALO = 8
TM = 512
CONV_ROWS = 64
ATTN_NB = 4
Q = 4
VMEM_LIMIT = 56 * 1024 * 1024

WA_COLS = D_SSM + CONV_CH
OFF_Z, OFF_XBC = 0, D_SSM
WB_START = WA_COLS + 2 * SSM_HEADS
OFF_Q = 0
OFF_V = D_ATTN + KV_HEADS * ATTN_HEAD_DIM
OFF_GA = OFF_V + KV_HEADS * ATTN_HEAD_DIM
WB_COLS = OFF_GA + D_ATTN
assert TM == Q * CHUNK and Q * 2 * SSM_HEADS == LANES


def _silu(v):
    return v * jax.nn.sigmoid(v)


def _softplus(v):
    return jnp.maximum(v, 0.0) + jnp.log1p(jnp.exp(-jnp.abs(v)))


def _inproj_kernel(tiles_per_seq, x_ref, g_ref, wa_ref, wb_ref, wkd_ref, wdt_ref, cw_ref, cb_ref, dtb_ref,
                   z_ref, xs_ref, bm_ref, cm_ref, dt_ref, q_ref, kd_ref, v_ref, ga_ref, pad_ref):
    i = pl.program_id(0)

    @pl.when(i == 0)
    def _():
        pad_ref[...] = jnp.zeros_like(pad_ref)

    j = lax.rem(i, tiles_per_seq)
    g = g_ref[...]

    def proj(hv, w_ref, off, n):
        return jnp.dot(hv, w_ref[:, off:off + n], preferred_element_type=F32)

    xv = x_ref[...]
    ms = jnp.mean(xv * xv, axis=-1, keepdims=True)
    h = (xv * lax.rsqrt(ms + EPS) * g).astype(BF16)

    s_cur = lax.rem(i, 3)
    s_prev = lax.rem(i + 2, 3)
    s_next = lax.rem(i + 1, 3)
    xbc = proj(h, wa_ref, OFF_XBC, CONV_CH)
    pad_ref[s_cur, HALO:HALO + TM, :] = xbc
    pad_ref[s_prev, HALO + TM:2 * HALO + TM, :] = xbc[0:HALO] * jnp.where(j > 0, 1.0, 0.0)
    pad_ref[s_next, 0:HALO, :] = xbc[TM - HALO:TM] * jnp.where(j < tiles_per_seq - 1, 1.0, 0.0)

    z_ref[...] = _silu(proj(h, wa_ref, OFF_Z, D_SSM)).astype(BF16)
    q_ref[...] = (proj(h, wb_ref, OFF_Q, D_ATTN) * (SCALE * LOG2E)).astype(BF16)
    kd_ref[...] = proj(h, wkd_ref, 0, 2 * KV_HEADS * ATTN_HEAD_DIM).astype(BF16)
    v_ref[...] = proj(h, wb_ref, OFF_V, KV_HEADS * ATTN_HEAD_DIM).astype(BF16)
    ga_ref[...] = _silu(proj(h, wb_ref, OFF_GA, D_ATTN)).astype(BF16)
    dt_lane = lax.broadcasted_iota(jnp.int32, (TM, LANES), 1) < 2 * SSM_HEADS
    dt = jnp.where(dt_lane, _softplus(proj(h, wdt_ref, 0, LANES) + dtb_ref[...]), 0.0)
    dtc = dt[0:CHUNK]
    for c in range(1, Q):
        dtc = dtc + pltpu.roll(dt[c * CHUNK:(c + 1) * CHUNK], 2 * SSM_HEADS * c, axis=1)
    dt_ref[...] = dtc

    base = HALO - (D_CONV - 1) // 2
    for cc in range(CONV_CH // LANES):
        sl = slice(cc * LANES, (cc + 1) * LANES)
        if cc < D_SSM // LANES:
            dst, o = xs_ref, cc * LANES
        elif cc < (D_SSM + SSM_GROUPS * D_STATE) // LANES:
            dst, o = bm_ref, cc * LANES - D_SSM
        else:
            dst, o = cm_ref, cc * LANES - D_SSM - SSM_GROUPS * D_STATE
        for r0 in range(0, TM, CONV_ROWS):
            acc = cb_ref[:, sl] + cw_ref[0:1, sl] * pad_ref[s_prev, base + r0:base + r0 + CONV_ROWS, sl]
            for k in range(1, D_CONV):
                acc = acc + cw_ref[k:k + 1, sl] * pad_ref[s_prev, base + r0 + k:base + r0 + k + CONV_ROWS, sl]
            dst[r0:r0 + CONV_ROWS, o:o + LANES] = _silu(acc).astype(BF16)


def _inproj(x2, norm_in_g, weights, conv_w, conv_b, dt_bias, seq):
    n = x2.shape[0]
    tiles_per_seq = seq // TM
    ntile = n // TM
    cur = lambda i: (jnp.minimum(i, ntile - 1), 0)
    lag = lambda i: (jnp.maximum(i - 1, 0), 0)
    const = lambda i: (0, 0)
    out_cols = (D_SSM, D_SSM, SSM_GROUPS * D_STATE, SSM_GROUPS * D_STATE,
                D_ATTN, 2 * KV_HEADS * ATTN_HEAD_DIM, KV_HEADS * ATTN_HEAD_DIM, D_ATTN)
    out_maps = (cur, lag, lag, lag, cur, cur, cur, cur)
    out_specs = [pl.BlockSpec((TM, c), m) for c, m in zip(out_cols, out_maps)]
    out_shape = [jax.ShapeDtypeStruct((n, c), BF16) for c in out_cols]
    out_specs.insert(4, pl.BlockSpec((CHUNK, LANES), lambda i: (0, jnp.minimum(i, ntile - 1))))
    out_shape.insert(4, jax.ShapeDtypeStruct((CHUNK, ntile * LANES), F32))
    return pl.pallas_call(
        lambda *refs: _inproj_kernel(tiles_per_seq, *refs),
        grid=(ntile + 1,),
        in_specs=[
            pl.BlockSpec((TM, D_MODEL), cur),
            pl.BlockSpec((1, D_MODEL), const),
            pl.BlockSpec((D_MODEL, WA_COLS), const, pipeline_mode=pl.Buffered(1)),
            pl.BlockSpec((D_MODEL, WB_COLS), const, pipeline_mode=pl.Buffered(1)),
            pl.BlockSpec((D_MODEL, 2 * KV_HEADS * ATTN_HEAD_DIM), const, pipeline_mode=pl.Buffered(1)),
            pl.BlockSpec((D_MODEL, LANES), const, pipeline_mode=pl.Buffered(1)),
            pl.BlockSpec((D_CONV, CONV_CH), const),
            pl.BlockSpec((1, CONV_CH), const),
            pl.BlockSpec((1, LANES), const),
        ],
        out_specs=out_specs,
        out_shape=out_shape,
        scratch_shapes=[pltpu.VMEM((3, TM + 2 * HALO, CONV_CH), F32)],
        compiler_params=pltpu.CompilerParams(dimension_semantics=("arbitrary",),
                                             vmem_limit_bytes=VMEM_LIMIT),
        name="inproj",
    )(x2, norm_in_g, *weights, conv_w, conv_b, dt_bias)


def _split_terms(v, n):
    terms, r = [], v
    for _ in range(n):
        t = r.astype(BF16)
        terms.append(t)
        r = r - t.astype(F32)
    return terms


def _chunk_cumsum(tril, a):
    t0, t1, t2 = _split_terms(a, 3)
    return (jnp.dot(tril, t0, preferred_element_type=F32) + jnp.dot(tril, t1, preferred_element_type=F32)
            + jnp.dot(tril, t2, preferred_element_type=F32))


def _hi_lo(v):
    return jnp.concatenate(_split_terms(v, 2), axis=1)


def _fwd_lanes():
    lane = lax.broadcasted_iota(jnp.int32, (CHUNK, LANES), 1)
    return jnp.bitwise_and(lane, 2 * SSM_HEADS - 1) < SSM_HEADS


def _states_kernel(alog_ref, e_ref, tril_ref, xsf_ref, bf_ref, dtf_ref, xsb_ref, bb_ref, dtb_ref,
                   sf_ref, sb_ref, cf_ref, cb_ref):
    @pl.when(pl.program_id(1) == 0)
    def _():
        cf_ref[...] = jnp.zeros_like(cf_ref)
        cb_ref[...] = jnp.zeros_like(cb_ref)

    a_neg = -jnp.exp(alog_ref[...])
    tril = tril_ref[...]
    fwd_lane = _fwd_lanes()

    def prep(dt_ref):
        dt = dt_ref[...]
        a = dt * a_neg
        cs = _chunk_cumsum(tril, a)
        tot = cs[CHUNK - 1:CHUNK, :]
        w = dt * jnp.exp(jnp.where(fwd_lane, tot - cs, cs - a))
        return _hi_lo(jnp.concatenate([w, jnp.broadcast_to(jnp.exp(tot), (8, LANES))], axis=0))

    def chunk(c, d, cat, xs_ref, b_ref, out_ref, carry_ref):
        rows = slice(c * CHUNK, (c + 1) * CHUNK)
        wx = jnp.dot(cat, e_ref[c, d], preferred_element_type=F32)
        xd = (xs_ref[rows, :].astype(F32) * wx[0:CHUNK]).astype(BF16)
        dec = wx[CHUNK:CHUNK + 1, :]
        for g in range(SSM_GROUPS):
            gr = slice(g * D_STATE, (g + 1) * D_STATE)
            gc = slice(g * GROUP_W, (g + 1) * GROUP_W)
            st = lax.dot_general(b_ref[rows, gr], xd[:, gc], (((0,), (0,)), ((), ())),
                                 preferred_element_type=F32)
            prev = carry_ref[gr, :]
            out_ref[c, gr, :] = prev.astype(BF16)
            carry_ref[gr, :] = prev * dec[:, gc] + st

    cat_f = prep(dtf_ref)
    cat_b = prep(dtb_ref)
    for c in range(Q):
        chunk(c, 0, cat_f, xsf_ref, bf_ref, sf_ref, cf_ref)
    for c in reversed(range(Q)):
        chunk(c, 1, cat_b, xsb_ref, bb_ref, sb_ref, cb_ref)


def _states(a_log, e, tril, xs, bm, dt, batch, seq):
    nstep = seq // (Q * CHUNK)
    nchunk = batch * seq // CHUNK
    fwd = lambda b, i: (b * nstep + i, 0)
    bwd = lambda b, i: (b * nstep + nstep - 1 - i, 0)
    const = lambda b, i: (0, 0)
    rows = Q * CHUNK
    st_shape = jax.ShapeDtypeStruct((nchunk, SSM_GROUPS * D_STATE, GROUP_W), BF16)
    return pl.pallas_call(
        _states_kernel,
        grid=(batch, nstep),
        in_specs=[
            pl.BlockSpec((1, LANES), const),
            pl.BlockSpec((Q, 2, 2 * LANES, D_SSM), lambda b, i: (0, 0, 0, 0), pipeline_mode=pl.Buffered(1)),
            pl.BlockSpec((CHUNK, CHUNK), const),
            pl.BlockSpec((rows, D_SSM), fwd),
            pl.BlockSpec((rows, SSM_GROUPS * D_STATE), fwd),
            pl.BlockSpec((CHUNK, LANES), lambda b, i: (0, b * nstep + i)),
            pl.BlockSpec((rows, D_SSM), bwd),
            pl.BlockSpec((rows, SSM_GROUPS * D_STATE), bwd),
            pl.BlockSpec((CHUNK, LANES), lambda b, i: (0, b * nstep + nstep - 1 - i)),
        ],
        out_specs=[
            pl.BlockSpec((Q, SSM_GROUPS * D_STATE, GROUP_W), lambda b, i: (b * nstep + i, 0, 0)),
            pl.BlockSpec((Q, SSM_GROUPS * D_STATE, GROUP_W), lambda b, i: (b * nstep + nstep - 1 - i, 0, 0)),
        ],
        out_shape=[st_shape, st_shape],
        scratch_shapes=[pltpu.VMEM((SSM_GROUPS * D_STATE, GROUP_W), F32),
                        pltpu.VMEM((SSM_GROUPS * D_STATE, GROUP_W), F32)],
        compiler_params=pltpu.CompilerParams(dimension_semantics=("arbitrary", "arbitrary"),
                                             vmem_limit_bytes=VMEM_LIMIT),
        name="ssd_states",
    )(a_log, e, tril, xs, bm, dt, xs, bm, dt)


def _ssd_out_kernel(alog_ref, dskip_ref, gn_ref, selb_ref, tril_ref, xs_ref, bm_ref, cm_ref, dt_ref,
                    z_ref, sf_ref, sb_ref, ya_ref, x_ref, w_ref, fg_ref, o_ref, ys_ref, vs_ref):
    li = lax.broadcasted_iota(jnp.int32, (CHUNK, CHUNK), 0)
    si = lax.broadcasted_iota(jnp.int32, (CHUNK, CHUNK), 1)
    lower = li >= si
    eye = li == si
    left = lax.broadcasted_iota(jnp.int32, (CHUNK, LANES), 1) < SSM_HEAD_DIM

    dt = dt_ref[...]
    a = dt * (-LOG2E * jnp.exp(alog_ref[...]))
    cs = _chunk_cumsum(tril_ref[...], a)
    tot = cs[CHUNK - 1:CHUNK, :]
    colq = jnp.where(_fwd_lanes(), cs, tot - cs + a)
    rowq = (colq - jnp.log2(dt)).T
    dtr = dt.T.astype(BF16)
    catq = _hi_lo(colq)

    for c in range(Q):
        rows = slice(c * CHUNK, (c + 1) * CHUNK)
        for g in range(SSM_GROUPS):
            gr = slice(g * D_STATE, (g + 1) * D_STATE)
            cmg = cm_ref[rows, gr]
            cb = lax.dot_general(cmg, bm_ref[rows, gr], (((1,), (1,)), ((), ())),
                                 preferred_element_type=F32)
            cbb = cb.astype(BF16)
            cbeye = jnp.where(eye, cb, 0.0).astype(BF16)
            ssq = jnp.zeros((CHUNK, 1), F32)
            for pp in range(GROUP_W // LANES):
                hd0 = g * (SSM_HEADS // SSM_GROUPS) + pp * 2
                cols = slice(hd0 * SSM_HEAD_DIM, (hd0 + 2) * SSM_HEAD_DIM)
                if pp % 2 == 0:
                    c2 = slice(pp * LANES, (pp + 2) * LANES)
                    csf = jnp.dot(cmg, sf_ref[c, gr, c2], preferred_element_type=F32)
                    csb = jnp.dot(cmg, sb_ref[c, gr, c2], preferred_element_type=F32)
                pc = slice((pp % 2) * LANES, (pp % 2 + 1) * LANES)
                colb = jnp.dot(catq, selb_ref[c, :, hd0 * CHUNK:(hd0 + 2) * CHUNK],
                               preferred_element_type=F32)
                scores, colf_bc, colb_bc = [], [], []
                for hh in range(2):
                    h = c * 2 * SSM_HEADS + hd0 + hh
                    hb = h + SSM_HEADS
                    colf_bc.append(jnp.broadcast_to(colq[:, h:h + 1], (CHUNK, CHUNK)))
                    colb_bc.append(colb[:, hh * CHUNK:(hh + 1) * CHUNK])
                    arg = jnp.where(lower, colf_bc[hh] - rowq[h:h + 1, :], colb_bc[hh] - rowq[hb:hb + 1, :])
                    scores.append(cbb * jnp.exp2(arg).astype(BF16) + cbeye * dtr[hb:hb + 1, :])
                xt = xs_ref[rows, cols]
                zero = jnp.zeros_like(xt)
                rhs = jnp.concatenate([jnp.where(left, xt, zero), jnp.where(left, zero, xt)], axis=0)
                y_diag = jnp.dot(jnp.concatenate(scores, axis=1), rhs, preferred_element_type=F32)
                y_off = (csf[:, pc] * jnp.exp2(jnp.where(left, colf_bc[0], colf_bc[1]))
                         + csb[:, pc] * jnp.exp2(jnp.where(left, colb_bc[0], colb_bc[1])))
                vg = (y_diag + y_off + dskip_ref[:, cols] * xt.astype(F32)) * z_ref[rows, cols].astype(F32)
                vs_ref[rows, cols] = vg
                ssq = ssq + jnp.sum(vg * vg, axis=-1, keepdims=True)
            gc = slice(g * GROUP_W, (g + 1) * GROUP_W)
            ys_ref[rows, gc] = (vs_ref[rows, gc] * lax.rsqrt(ssq * (1.0 / GROUP_W) + EPS)
                                * gn_ref[:, gc]).astype(BF16)

        acc = (jnp.dot(ys_ref[rows, :], w_ref[0:D_SSM, :], preferred_element_type=F32)
               + jnp.dot(ya_ref[rows, :], w_ref[D_SSM:D_MIX, :], preferred_element_type=F32))
        hres = x_ref[rows, :] + acc
        ms = jnp.mean(hres * hres, axis=-1, keepdims=True)
        o_ref[rows, :] = hres * lax.rsqrt(ms + EPS) * fg_ref[...]


def _ssd_out(a_log, dskip_x, ssd_norm_g, selb, tril, xs, bm, cm, dt, z, sf, sb, y_attn, x2, w_out, final_norm_g):
    n = xs.shape[0]
    rows = Q * CHUNK
    row = lambda i: (i, 0)
    const = lambda i: (0, 0)
    return pl.pallas_call(
        _ssd_out_kernel,
        grid=(n // rows,),
        in_specs=[
            pl.BlockSpec((1, LANES), const),
            pl.BlockSpec((1, D_SSM), const),
            pl.BlockSpec((1, D_SSM), const),
            pl.BlockSpec((Q, 2 * LANES, SSM_HEADS * CHUNK), lambda i: (0, 0, 0), pipeline_mode=pl.Buffered(1)),
            pl.BlockSpec((CHUNK, CHUNK), const),
            pl.BlockSpec((rows, D_SSM), row),
            pl.BlockSpec((rows, SSM_GROUPS * D_STATE), row),
            pl.BlockSpec((rows, SSM_GROUPS * D_STATE), row),
            pl.BlockSpec((CHUNK, LANES), lambda i: (0, i)),
            pl.BlockSpec((rows, D_SSM), row),
            pl.BlockSpec((Q, SSM_GROUPS * D_STATE, GROUP_W), lambda i: (i, 0, 0)),
            pl.BlockSpec((Q, SSM_GROUPS * D_STATE, GROUP_W), lambda i: (i, 0, 0)),
            pl.BlockSpec((rows, D_ATTN), row),
            pl.BlockSpec((rows, D_MODEL), row),
            pl.BlockSpec((D_MIX, D_MODEL), const, pipeline_mode=pl.Buffered(1)),
            pl.BlockSpec((1, D_MODEL), const),
        ],
        out_specs=pl.BlockSpec((rows, D_MODEL), row),
        out_shape=jax.ShapeDtypeStruct((n, D_MODEL), F32),
        scratch_shapes=[pltpu.VMEM((rows, D_SSM), BF16), pltpu.VMEM((rows, D_SSM), F32)],
        compiler_params=pltpu.CompilerParams(dimension_semantics=("parallel",),
                                             vmem_limit_bytes=VMEM_LIMIT),
        name="ssd_out",
    )(a_log, dskip_x, ssd_norm_g, selb, tril, xs, bm, cm, dt, z, sf, sb, y_attn, x2, w_out, final_norm_g)


def _bucket_table_t():
    rel = jnp.arange(3 * BLK)[:, None] - BLK - jnp.arange(BLK)[None, :]
    half = NUM_BUCKETS // 2
    ret = (rel > 0).astype(jnp.int32) * half
    n = jnp.abs(rel)
    nf = jnp.maximum(n, 1).astype(F32)
    large = MAX_EXACT + (jnp.log(nf / MAX_EXACT) / math.log(MAX_DISTANCE / MAX_EXACT)
                         * (half - MAX_EXACT)).astype(jnp.int32)
    large = jnp.minimum(large, half - 1)
    return (ret + jnp.where(n < MAX_EXACT, n, large)).astype(jnp.int32)


def _attn_kernel(nblk, nstep, rb_ref, bucket_ref, sink_ref, gn_ref, q_ref, kp_ref, kc_ref, kn_ref,
                 vp_ref, vc_ref, vn_ref, ga_ref, y_ref,
                 tab_ref, pe_ref, mk_ref, ot_ref, vs_ref):
    g = pl.program_id(0)

    @pl.when(g == 0)
    def _init():
        pe_ref[...] = jnp.zeros_like(pe_ref)
        mk_ref[...] = jnp.zeros_like(mk_ref)
        ot_ref[...] = jnp.zeros_like(ot_ref)
        bk = bucket_ref[...]
        t = lax.broadcasted_iota(jnp.int32, (3 * BLK, BLK), 0)
        qq = lax.broadcasted_iota(jnp.int32, (3 * BLK, BLK), 1)
        in_window = jnp.abs(t - BLK - qq) <= WINDOW

        def per_head(h, carry):
            acc = jnp.zeros((3 * BLK, BLK), F32)
            for b in range(NUM_BUCKETS):
                acc = jnp.where(bk == b, rb_ref[b, h], acc)
            tab = jnp.where(in_window, acc * LOG2E, NEG_INF)
            tab_ref[0, h] = tab
            tab_ref[1, h] = jnp.where(t >= BLK, tab, NEG_INF)
            tab_ref[2, h] = jnp.where(t < 2 * BLK, tab, NEG_INF)
            return carry

        lax.fori_loop(0, ATTN_HEADS, per_head, 0)

    left = lax.broadcasted_iota(jnp.int32, (BLK, LANES), 1) < ATTN_HEAD_DIM
    left3 = lax.broadcasted_iota(jnp.int32, (3 * BLK, LANES), 1) < ATTN_HEAD_DIM
    s2 = sink_ref[...] * LOG2E

    def run(even, odd):
        first_blk = jnp.minimum(g, nstep - 1) * ATTN_NB
        kall = jnp.concatenate([kp_ref[...], kc_ref[...], kn_ref[...]], axis=0)
        vall = jnp.concatenate([vp_ref[...], vc_ref[...], vn_ref[...]], axis=0)
        one = jnp.ones((3 * BLK, LANES), BF16)

        def stage_a(j, p, var):
            kd = kall[j * BLK:(j + 3) * BLK, (p // 2) * LANES:(p // 2 + 1) * LANES]
            qt = q_ref[j * BLK:(j + 1) * BLK, p * LANES:(p + 1) * LANES]
            zero = jnp.zeros_like(qt)
            rhs_t = jnp.concatenate([jnp.where(left, qt, zero), jnp.where(left, zero, qt)], axis=0)
            lg = lax.dot_general(kd, rhs_t, (((1,), (1,)), ((), ())), preferred_element_type=F32)
            lg = lg + jnp.concatenate([tab_ref[var, 2 * p], tab_ref[var, 2 * p + 1]], axis=1)
            for kb in range(3):
                blk = lg[kb * BLK:(kb + 1) * BLK]
                mk = jnp.max(blk, axis=0, keepdims=True)
                pe_ref[even, j, p, kb * BLK:(kb + 1) * BLK, :] = jnp.exp2(blk - mk).astype(BF16)
                mk_ref[even, j, p, kb:kb + 1, :] = mk

        def stage_b(j, p):
            kv = p // 2
            vt = vall[j * BLK:(j + 3) * BLK, (kv // 2) * LANES:(kv // 2 + 1) * LANES]
            vmod = jnp.where(left3, vt, one) if kv % 2 == 0 else jnp.where(left3, one, vt)
            r0 = (kv % 2) * ATTN_HEAD_DIM
            d0 = ATTN_HEAD_DIM - r0
            mk = mk_ref[odd, j, p]
            sp = s2[p:p + 1, :]
            m = jnp.maximum(jnp.max(mk[0:3], axis=0, keepdims=True), sp)
            resc = jnp.exp2(mk[0:3] - m).astype(BF16)
            pes = jnp.concatenate([pe_ref[odd, j, p, kb * BLK:(kb + 1) * BLK, :] * resc[kb:kb + 1]
                                   for kb in range(3)], axis=0)
            o2 = lax.dot_general(vmod, pes, (((0,), (0,)), ((), ())), preferred_element_type=F32)
            o = o2[r0:r0 + ATTN_HEAD_DIM, :] / (o2[d0:d0 + 1, :] + jnp.exp2(sp - m))
            ot_ref[odd, j, 2 * p * ATTN_HEAD_DIM:(2 * p + 1) * ATTN_HEAD_DIM, :] = o[:, 0:BLK]
            ot_ref[odd, j, (2 * p + 1) * ATTN_HEAD_DIM:(2 * p + 2) * ATTN_HEAD_DIM, :] = o[:, BLK:2 * BLK]

        def stage_c(j, p, ssq):
            rows = slice(j * BLK, (j + 1) * BLK)
            cols = slice(p * LANES, (p + 1) * LANES)
            v = ot_ref[even, j, p * LANES:(p + 1) * LANES, :].T * ga_ref[rows, cols].astype(F32)
            vs_ref[rows, cols] = v
            return ssq + jnp.sum(v * v, axis=-1, keepdims=True)

        for j in range(ATTN_NB):
            pos = lax.rem(first_blk + j, nblk)
            var = jnp.where(pos == 0, 1, jnp.where(pos == nblk - 1, 2, 0))
            rows = slice(j * BLK, (j + 1) * BLK)
            ssq = jnp.zeros((BLK, 1), F32)
            for p in range(ATTN_HEADS // 2):
                stage_a(j, p, var)
                stage_b(j, p)
                ssq = stage_c(j, p, ssq)
            y_ref[rows, :] = (vs_ref[rows, :] * lax.rsqrt(ssq * (1.0 / D_ATTN) + EPS) * gn_ref[...]).astype(BF16)

    @pl.when(lax.rem(g, 2) == 0)
    def _():
        run(0, 1)

    @pl.when(lax.rem(g, 2) == 1)
    def _():
        run(1, 0)


def _attn(rel_bias, bucket_t, sink_x, attn_norm_g, q, kd, v, ga, batch, seq):
    n = q.shape[0]
    nblk = seq // BLK
    assert nblk % ATTN_NB == 0
    nstep = batch * nblk // ATTN_NB

    def step(lag):
        return lambda g: jnp.clip(g - lag, 0, nstep - 1)

    def cur(lag):
        return lambda g: (step(lag)(g), 0)

    def prev(lag):
        def index(g):
            b = step(lag)(g) * ATTN_NB
            return (b - jnp.where(lax.rem(b, nblk) > 0, 1, 0), 0)
        return index

    def nxt(lag):
        def index(g):
            b = step(lag)(g) * ATTN_NB + ATTN_NB - 1
            return (b + jnp.where(lax.rem(b, nblk) < nblk - 1, 1, 0), 0)
        return index

    const = lambda g: (0, 0)
    kw = 2 * KV_HEADS * ATTN_HEAD_DIM
    vw = KV_HEADS * ATTN_HEAD_DIM
    npair = ATTN_HEADS // 2
    rows = ATTN_NB * BLK
    return pl.pallas_call(
        lambda *refs: _attn_kernel(nblk, nstep, *refs),
        grid=(nstep + 2,),
        in_specs=[
            pl.BlockSpec(memory_space=pltpu.SMEM),
            pl.BlockSpec((3 * BLK, BLK), const),
            pl.BlockSpec((npair, 2 * BLK), const),
            pl.BlockSpec((1, D_ATTN), const),
            pl.BlockSpec((rows, D_ATTN), cur(0)),
            pl.BlockSpec((BLK, kw), prev(0)), pl.BlockSpec((rows, kw), cur(0)), pl.BlockSpec((BLK, kw), nxt(0)),
            pl.BlockSpec((BLK, vw), prev(1)), pl.BlockSpec((rows, vw), cur(1)), pl.BlockSpec((BLK, vw), nxt(1)),
            pl.BlockSpec((rows, D_ATTN), cur(2)),
        ],
        out_specs=pl.BlockSpec((rows, D_ATTN), cur(2)),
        out_shape=jax.ShapeDtypeStruct((n, D_ATTN), BF16),
        scratch_shapes=[pltpu.VMEM((3, ATTN_HEADS, 3 * BLK, BLK), F32),
                        pltpu.VMEM((2, ATTN_NB, npair, 3 * BLK, 2 * BLK), BF16),
                        pltpu.VMEM((2, ATTN_NB, npair, 8, 2 * BLK), F32),
                        pltpu.VMEM((2, ATTN_NB, D_ATTN, BLK), F32),
                        pltpu.VMEM((rows, D_ATTN), F32)],
        compiler_params=pltpu.CompilerParams(dimension_semantics=("arbitrary",),
                                             vmem_limit_bytes=VMEM_LIMIT),
        name="attn",
    )(rel_bias, bucket_t, sink_x, attn_norm_g, q, kd, kd, kd, v, v, v, ga)


def _arrange_w_in(w_in):
    wa = w_in[:, 0:WA_COLS].astype(BF16)
    wdt = jnp.pad(w_in[:, WA_COLS:WB_START], ((0, 0), (0, LANES - 2 * SSM_HEADS))).astype(BF16)
    wb = w_in[:, WB_START:WB_START + WB_COLS].astype(BF16)
    wk = wb[:, D_ATTN:OFF_V].reshape(D_MODEL, KV_HEADS, 1, ATTN_HEAD_DIM)
    wkd = jnp.broadcast_to(wk, (D_MODEL, KV_HEADS, 2, ATTN_HEAD_DIM)).reshape(D_MODEL, -1)
    return wa, wb, wkd, wdt


def _expand_matrices():
    e = np.zeros((Q, 2, 2 * LANES, D_SSM), np.float32)
    for c in range(Q):
        for d in range(2):
            for h in range(SSM_HEADS):
                r = (c * 2 + d) * SSM_HEADS + h
                e[c, d, r, h * SSM_HEAD_DIM:(h + 1) * SSM_HEAD_DIM] = 1.0
                e[c, d, LANES + r, h * SSM_HEAD_DIM:(h + 1) * SSM_HEAD_DIM] = 1.0
    return e


def _bcast_matrices():
    m = np.zeros((Q, 2 * LANES, SSM_HEADS * CHUNK), np.float32)
    for c in range(Q):
        for h in range(SSM_HEADS):
            r = (c * 2 + 1) * SSM_HEADS + h
            m[c, r, h * CHUNK:(h + 1) * CHUNK] = 1.0
            m[c, LANES + r, h * CHUNK:(h + 1) * CHUNK] = 1.0
    return m


def kernel(x, norm_in_g, w_in, conv_w, conv_b, dt_bias, a_log, d_skip, ssd_norm_g, rel_bias, sink,
           attn_norm_g, w_out, final_norm_g):
    batch, seq, _ = x.shape
    assert w_out.shape[0] == 1 and seq % (Q * CHUNK) == 0 and seq % TM == 0
    x2 = x.reshape(batch * seq, D_MODEL)

    weights = _arrange_w_in(w_in)
    e = jnp.asarray(_expand_matrices(), BF16)
    tril = jnp.asarray(np.tril(np.ones((CHUNK, CHUNK), np.float32)), BF16)
    a_log2 = jnp.tile(a_log.reshape(1, 2 * SSM_HEADS).astype(F32), (1, Q))
    dt_bias_x = jnp.pad(dt_bias.reshape(1, 2 * SSM_HEADS).astype(F32), ((0, 0), (0, LANES - 2 * SSM_HEADS)))
    dskip_x = jnp.repeat(d_skip.astype(F32), SSM_HEAD_DIM).reshape(1, D_SSM)
    sink_x = jnp.repeat(sink.astype(F32), BLK).reshape(ATTN_HEADS // 2, 2 * BLK)

    z, xs, bm, cm, dt, q, kd, v, ga = _inproj(
        x2, norm_in_g.reshape(1, D_MODEL), weights, conv_w, conv_b.reshape(1, CONV_CH),
        dt_bias_x, seq)
    sf, sb = _states(a_log2, e, tril, xs, bm, dt, batch, seq)
    y_attn = _attn(rel_bias.astype(F32), _bucket_table_t(), sink_x,
                   attn_norm_g.reshape(1, D_ATTN), q, kd, v, ga, batch, seq)
    out = _ssd_out(a_log2, dskip_x, ssd_norm_g.reshape(1, D_SSM), jnp.asarray(_bcast_matrices(), BF16),
                   tril, xs, bm, cm, dt, z, sf, sb, y_attn, x2, w_out[0].astype(BF16),
                   final_norm_g.reshape(1, D_MODEL))
    return out.reshape(batch, seq, D_MODEL)
```

```python
import math

import numpy as np
import jax
import jax.numpy as jnp
from jax import lax
from jax.experimental import pallas as pl
from jax.experimental.pallas import tpu as pltpu

F32 = jnp.float32
BF16 = jnp.bfloat16

D_MODEL = 1024
D_SSM = 1024
D_ATTN = 1024
D_MIX = D_SSM + D_ATTN
SSM_HEAD_DIM = 64
SSM_HEADS = D_SSM // SSM_HEAD_DIM
SSM_GROUPS = 2
GROUP_W = D_SSM // SSM_GROUPS
D_STATE = 128
D_CONV = 5
CHUNK = 128
CONV_CH = D_SSM + 2 * SSM_GROUPS * D_STATE
ATTN_HEAD_DIM = 64
ATTN_HEADS = D_ATTN // ATTN_HEAD_DIM
KV_HEADS = 4
WINDOW = 128
BLK = 128
NUM_BUCKETS = 32
MAX_DISTANCE = 128
MAX_EXACT = 8
EPS = 1e-6
NEG_INF = -1e30
SCALE = ATTN_HEAD_DIM ** -0.5
LOG2E = math.log2(math.e)

LANES = 128
HALO = 8
TM = 512
CONV_ROWS = 64
ATTN_NB = 4
CAT_PAD = 16
Q = 4
VMEM_LIMIT = 56 * 1024 * 1024

WA_COLS = D_SSM + CONV_CH
OFF_Z, OFF_XBC = 0, D_SSM
WB_START = WA_COLS + 2 * SSM_HEADS
OFF_Q = 0
OFF_V = D_ATTN + KV_HEADS * ATTN_HEAD_DIM
OFF_GA = OFF_V + KV_HEADS * ATTN_HEAD_DIM
WB_COLS = OFF_GA + D_ATTN
assert TM == Q * CHUNK and Q * 2 * SSM_HEADS == LANES


def _silu(v):
    return v * jax.nn.sigmoid(v)


def _softplus(v):
    return jnp.maximum(v, 0.0) + jnp.log1p(jnp.exp(-jnp.abs(v)))


def _inproj_kernel(tiles_per_seq, x_ref, g_ref, wa_ref, wb_ref, wkd_ref, wdt_ref, cw_ref, cb_ref, dtb_ref,
                   z_ref, xs_ref, bm_ref, cm_ref, dt_ref, q_ref, kd_ref, v_ref, ga_ref, pad_ref):
    i = pl.program_id(0)

    @pl.when(i == 0)
    def _():
        pad_ref[...] = jnp.zeros_like(pad_ref)

    j = lax.rem(i, tiles_per_seq)
    g = g_ref[...]

    def proj(hv, w_ref, off, n):
        return jnp.dot(hv, w_ref[:, off:off + n], preferred_element_type=F32)

    xv = x_ref[...]
    ms = jnp.mean(xv * xv, axis=-1, keepdims=True)
    h = (xv * lax.rsqrt(ms + EPS) * g).astype(BF16)

    s_cur = lax.rem(i, 3)
    s_prev = lax.rem(i + 2, 3)
    s_next = lax.rem(i + 1, 3)
    xbc = proj(h, wa_ref, OFF_XBC, CONV_CH)
    pad_ref[s_cur, HALO:HALO + TM, :] = xbc
    pad_ref[s_prev, HALO + TM:2 * HALO + TM, :] = xbc[0:HALO] * jnp.where(j > 0, 1.0, 0.0)
    pad_ref[s_next, 0:HALO, :] = xbc[TM - HALO:TM] * jnp.where(j < tiles_per_seq - 1, 1.0, 0.0)

    z_ref[...] = _silu(proj(h, wa_ref, OFF_Z, D_SSM)).astype(BF16)
    q_ref[...] = (proj(h, wb_ref, OFF_Q, D_ATTN) * (SCALE * LOG2E)).astype(BF16)
    kd_ref[...] = proj(h, wkd_ref, 0, 2 * KV_HEADS * ATTN_HEAD_DIM).astype(BF16)
    v_ref[...] = proj(h, wb_ref, OFF_V, KV_HEADS * ATTN_HEAD_DIM).astype(BF16)
    ga_ref[...] = _silu(proj(h, wb_ref, OFF_GA, D_ATTN)).astype(BF16)
    dt_lane = lax.broadcasted_iota(jnp.int32, (TM, LANES), 1) < 2 * SSM_HEADS
    dt = jnp.where(dt_lane, _softplus(proj(h, wdt_ref, 0, LANES) + dtb_ref[...]), 0.0)
    dtc = dt[0:CHUNK]
    for c in range(1, Q):
        dtc = dtc + pltpu.roll(dt[c * CHUNK:(c + 1) * CHUNK], 2 * SSM_HEADS * c, axis=1)
    dt_ref[...] = dtc

    base = HALO - (D_CONV - 1) // 2
    for cc in range(CONV_CH // LANES):
        sl = slice(cc * LANES, (cc + 1) * LANES)
        if cc < D_SSM // LANES:
            dst, o = xs_ref, cc * LANES
        elif cc < (D_SSM + SSM_GROUPS * D_STATE) // LANES:
            dst, o = bm_ref, cc * LANES - D_SSM
        else:
            dst, o = cm_ref, cc * LANES - D_SSM - SSM_GROUPS * D_STATE
        for r0 in range(0, TM, CONV_ROWS):
            acc = cb_ref[:, sl] + cw_ref[0:1, sl] * pad_ref[s_prev, base + r0:base + r0 + CONV_ROWS, sl]
            for k in range(1, D_CONV):
                acc = acc + cw_ref[k:k + 1, sl] * pad_ref[s_prev, base + r0 + k:base + r0 + k + CONV_ROWS, sl]
            dst[r0:r0 + CONV_ROWS, o:o + LANES] = _silu(acc).astype(BF16)


def _inproj(x2, norm_in_g, weights, conv_w, conv_b, dt_bias, seq):
    n = x2.shape[0]
    tiles_per_seq = seq // TM
    ntile = n // TM
    cur = lambda i: (jnp.minimum(i, ntile - 1), 0)
    lag = lambda i: (jnp.maximum(i - 1, 0), 0)
    const = lambda i: (0, 0)
    out_cols = (D_SSM, D_SSM, SSM_GROUPS * D_STATE, SSM_GROUPS * D_STATE,
                D_ATTN, 2 * KV_HEADS * ATTN_HEAD_DIM, KV_HEADS * ATTN_HEAD_DIM, D_ATTN)
    out_maps = (cur, lag, lag, lag, cur, cur, cur, cur)
    out_specs = [pl.BlockSpec((TM, c), m) for c, m in zip(out_cols, out_maps)]
    out_shape = [jax.ShapeDtypeStruct((n, c), BF16) for c in out_cols]
    out_specs.insert(4, pl.BlockSpec((CHUNK, LANES), lambda i: (0, jnp.minimum(i, ntile - 1))))
    out_shape.insert(4, jax.ShapeDtypeStruct((CHUNK, ntile * LANES), F32))
    return pl.pallas_call(
        lambda *refs: _inproj_kernel(tiles_per_seq, *refs),
        grid=(ntile + 1,),
        in_specs=[
            pl.BlockSpec((TM, D_MODEL), cur),
            pl.BlockSpec((1, D_MODEL), const),
            pl.BlockSpec((D_MODEL, WA_COLS), const, pipeline_mode=pl.Buffered(1)),
            pl.BlockSpec((D_MODEL, WB_COLS), const, pipeline_mode=pl.Buffered(1)),
            pl.BlockSpec((D_MODEL, 2 * KV_HEADS * ATTN_HEAD_DIM), const, pipeline_mode=pl.Buffered(1)),
            pl.BlockSpec((D_MODEL, LANES), const, pipeline_mode=pl.Buffered(1)),
            pl.BlockSpec((D_CONV, CONV_CH), const),
            pl.BlockSpec((1, CONV_CH), const),
            pl.BlockSpec((1, LANES), const),
        ],
        out_specs=out_specs,
        out_shape=out_shape,
        scratch_shapes=[pltpu.VMEM((3, TM + 2 * HALO, CONV_CH), F32)],
        compiler_params=pltpu.CompilerParams(dimension_semantics=("arbitrary",),
                                             vmem_limit_bytes=VMEM_LIMIT),
        name="inproj",
    )(x2, norm_in_g, *weights, conv_w, conv_b, dt_bias)


def _split_terms(v, n):
    terms, r = [], v
    for _ in range(n):
        t = r.astype(BF16)
        terms.append(t)
        r = r - t.astype(F32)
    return terms


def _chunk_cumsum(tril, a):
    t0, t1, t2 = _split_terms(a, 3)
    return (jnp.dot(tril, t0, preferred_element_type=F32) + jnp.dot(tril, t1, preferred_element_type=F32)
            + jnp.dot(tril, t2, preferred_element_type=F32))


def _hi_lo(v):
    return jnp.concatenate(_split_terms(v, 2), axis=1)


def _fwd_lanes():
    lane = lax.broadcasted_iota(jnp.int32, (CHUNK, LANES), 1)
    return jnp.bitwise_and(lane, 2 * SSM_HEADS - 1) < SSM_HEADS


def _states_kernel(nstep, alog_ref, e_ref, tril_ref, dtf_ref, dtb_ref, xsf_ref, bf_ref, xsb_ref, bb_ref,
                   sf_ref, sb_ref, cat_ref, cf_ref, cb_ref):
    t = pl.program_id(0)

    @pl.when(t == 0)
    def _():
        cat_ref[...] = jnp.zeros_like(cat_ref)

    @pl.when((t == 0) | (lax.rem(t, nstep) == 1))
    def _():
        cf_ref[...] = jnp.zeros_like(cf_ref)
        cb_ref[...] = jnp.zeros_like(cb_ref)

    a_neg = -jnp.exp(alog_ref[...])
    tril = tril_ref[...]
    fwd_lane = _fwd_lanes()

    def prep(dt_ref):
        dt = dt_ref[...]
        a = dt * a_neg
        cs = _chunk_cumsum(tril, a)
        tot = cs[CHUNK - 1:CHUNK, :]
        w = dt * jnp.exp(jnp.where(fwd_lane, tot - cs, cs - a))
        return _hi_lo(jnp.concatenate([w, jnp.broadcast_to(jnp.exp(tot), (CAT_PAD, LANES))], axis=0))

    def chunk(c, d, slot, xs_ref, b_ref, out_ref, carry_ref):
        rows = slice(c * CHUNK, (c + 1) * CHUNK)
        wx = jnp.dot(cat_ref[slot, d], e_ref[c, d], preferred_element_type=F32)
        xd = (xs_ref[rows, :].astype(F32) * wx[0:CHUNK]).astype(BF16)
        dec = wx[CHUNK:CHUNK + 1, :]
        for g in range(SSM_GROUPS):
            gr = slice(g * D_STATE, (g + 1) * D_STATE)
            gc = slice(g * GROUP_W, (g + 1) * GROUP_W)
            st = lax.dot_general(b_ref[rows, gr], xd[:, gc], (((0,), (0,)), ((), ())),
                                 preferred_element_type=F32)
            prev = carry_ref[gr, :]
            out_ref[c, gr, :] = prev.astype(BF16)
            carry_ref[gr, :] = prev * dec[:, gc] + st

    def run(slot_p, slot_m):
        cat_ref[slot_p, 0] = prep(dtf_ref)
        cat_ref[slot_p, 1] = prep(dtb_ref)
        for c in range(Q):
            chunk(c, 0, slot_m, xsf_ref, bf_ref, sf_ref, cf_ref)
            chunk(Q - 1 - c, 1, slot_m, xsb_ref, bb_ref, sb_ref, cb_ref)

    @pl.when(lax.rem(t, 2) == 0)
    def _():
        run(0, 1)

    @pl.when(lax.rem(t, 2) == 1)
    def _():
        run(1, 0)


def _states(a_log, e, tril, xs, bm, dt, batch, seq):
    nstep = seq // (Q * CHUNK)
    total = batch * nstep
    nchunk = batch * seq // CHUNK

    def mirrored(s):
        return s - lax.rem(s, nstep) + nstep - 1 - lax.rem(s, nstep)

    cur = lambda t: jnp.minimum(t, total - 1)
    lag = lambda t: jnp.maximum(t - 1, 0)
    const = lambda t: (0, 0)
    rows = Q * CHUNK
    st_shape = jax.ShapeDtypeStruct((nchunk, SSM_GROUPS * D_STATE, GROUP_W), BF16)
    st_block = (Q, SSM_GROUPS * D_STATE, GROUP_W)
    return pl.pallas_call(
        lambda *refs: _states_kernel(nstep, *refs),
        grid=(total + 1,),
        in_specs=[
            pl.BlockSpec((1, LANES), const),
            pl.BlockSpec((Q, 2, 2 * LANES, D_SSM), lambda t: (0, 0, 0, 0), pipeline_mode=pl.Buffered(1)),
            pl.BlockSpec((CHUNK, CHUNK), const),
            pl.BlockSpec((CHUNK, LANES), lambda t: (0, cur(t))),
            pl.BlockSpec((CHUNK, LANES), lambda t: (0, mirrored(cur(t)))),
            pl.BlockSpec((rows, D_SSM), lambda t: (lag(t), 0)),
            pl.BlockSpec((rows, SSM_GROUPS * D_STATE), lambda t: (lag(t), 0)),
            pl.BlockSpec((rows, D_SSM), lambda t: (mirrored(lag(t)), 0)),
            pl.BlockSpec((rows, SSM_GROUPS * D_STATE), lambda t: (mirrored(lag(t)), 0)),
        ],
        out_specs=[
            pl.BlockSpec(st_block, lambda t: (lag(t), 0, 0)),
            pl.BlockSpec(st_block, lambda t: (mirrored(lag(t)), 0, 0)),
        ],
        out_shape=[st_shape, st_shape],
        scratch_shapes=[pltpu.VMEM((2, 2, CHUNK + CAT_PAD, 2 * LANES), BF16),
                        pltpu.VMEM((SSM_GROUPS * D_STATE, GROUP_W), F32),
                        pltpu.VMEM((SSM_GROUPS * D_STATE, GROUP_W), F32)],
        compiler_params=pltpu.CompilerParams(dimension_semantics=("arbitrary",),
                                             vmem_limit_bytes=VMEM_LIMIT),
        name="ssd_states",
    )(a_log, e, tril, dt, dt, xs, bm, xs, bm)


def _ssd_out_kernel(alog_ref, dskip_ref, gn_ref, selb_ref, tril_ref, xs_ref, bm_ref, cm_ref, dt_ref,
                    z_ref, sf_ref, sb_ref, ya_ref, x_ref, w_ref, fg_ref, o_ref, ys_ref, vs_ref):
    li = lax.broadcasted_iota(jnp.int32, (CHUNK, CHUNK), 0)
    si = lax.broadcasted_iota(jnp.int32, (CHUNK, CHUNK), 1)
    lower = li >= si
    eye = li == si
    left = lax.broadcasted_iota(jnp.int32, (CHUNK, LANES), 1) < SSM_HEAD_DIM

    dt = dt_ref[...]
    a = dt * (-LOG2E * jnp.exp(alog_ref[...]))
    cs = _chunk_cumsum(tril_ref[...], a)
    tot = cs[CHUNK - 1:CHUNK, :]
    colq = jnp.where(_fwd_lanes(), cs, tot - cs + a)
    rowq = (colq - jnp.log2(dt)).T
    dtr = dt.T.astype(BF16)
    catq = _hi_lo(colq)

    for c in range(Q):
        rows = slice(c * CHUNK, (c + 1) * CHUNK)
        for g in range(SSM_GROUPS):
            gr = slice(g * D_STATE, (g + 1) * D_STATE)
            cmg = cm_ref[rows, gr]
            cb = lax.dot_general(cmg, bm_ref[rows, gr], (((1,), (1,)), ((), ())),
                                 preferred_element_type=F32)
            cbb = cb.astype(BF16)
            cbeye = jnp.where(eye, cb, 0.0).astype(BF16)
            ssq = jnp.zeros((CHUNK, 1), F32)
            for pp in range(GROUP_W // LANES):
                hd0 = g * (SSM_HEADS // SSM_GROUPS) + pp * 2
                cols = slice(hd0 * SSM_HEAD_DIM, (hd0 + 2) * SSM_HEAD_DIM)
                if pp % 2 == 0:
                    c2 = slice(pp * LANES, (pp + 2) * LANES)
                    csf = jnp.dot(cmg, sf_ref[c, gr, c2], preferred_element_type=F32)
                    csb = jnp.dot(cmg, sb_ref[c, gr, c2], preferred_element_type=F32)
                pc = slice((pp % 2) * LANES, (pp % 2 + 1) * LANES)
                colb = jnp.dot(catq, selb_ref[c, :, hd0 * CHUNK:(hd0 + 2) * CHUNK],
                               preferred_element_type=F32)
                scores, colf_bc, colb_bc = [], [], []
                for hh in range(2):
                    h = c * 2 * SSM_HEADS + hd0 + hh
                    hb = h + SSM_HEADS
                    colf_bc.append(jnp.broadcast_to(colq[:, h:h + 1], (CHUNK, CHUNK)))
                    colb_bc.append(colb[:, hh * CHUNK:(hh + 1) * CHUNK])
                    arg = jnp.where(lower, colf_bc[hh] - rowq[h:h + 1, :], colb_bc[hh] - rowq[hb:hb + 1, :])
                    scores.append(cbb * jnp.exp2(arg).astype(BF16) + cbeye * dtr[hb:hb + 1, :])
                xt = xs_ref[rows, cols]
                zero = jnp.zeros_like(xt)
                rhs = jnp.concatenate([jnp.where(left, xt, zero), jnp.where(left, zero, xt)], axis=0)
                y_diag = jnp.dot(jnp.concatenate(scores, axis=1), rhs, preferred_element_type=F32)
                y_off = (csf[:, pc] * jnp.exp2(jnp.where(left, colf_bc[0], colf_bc[1]))
                         + csb[:, pc] * jnp.exp2(jnp.where(left, colb_bc[0], colb_bc[1])))
                vg = (y_diag + y_off + dskip_ref[:, cols] * xt.astype(F32)) * z_ref[rows, cols].astype(F32)
                vs_ref[rows, cols] = vg
                ssq = ssq + jnp.sum(vg * vg, axis=-1, keepdims=True)
            gc = slice(g * GROUP_W, (g + 1) * GROUP_W)
            ys_ref[rows, gc] = (vs_ref[rows, gc] * lax.rsqrt(ssq * (1.0 / GROUP_W) + EPS)
                                * gn_ref[:, gc]).astype(BF16)

        acc = (jnp.dot(ys_ref[rows, :], w_ref[0:D_SSM, :], preferred_element_type=F32)
               + jnp.dot(ya_ref[rows, :], w_ref[D_SSM:D_MIX, :], preferred_element_type=F32))
        hres = x_ref[rows, :] + acc
        ms = jnp.mean(hres * hres, axis=-1, keepdims=True)
        o_ref[rows, :] = hres * lax.rsqrt(ms + EPS) * fg_ref[...]


def _ssd_out(a_log, dskip_x, ssd_norm_g, selb, tril, xs, bm, cm, dt, z, sf, sb, y_attn, x2, w_out, final_norm_g):
    n = xs.shape[0]
    rows = Q * CHUNK
    row = lambda i: (i, 0)
    const = lambda i: (0, 0)
    return pl.pallas_call(
        _ssd_out_kernel,
        grid=(n // rows,),
        in_specs=[
            pl.BlockSpec((1, LANES), const),
            pl.BlockSpec((1, D_SSM), const),
            pl.BlockSpec((1, D_SSM), const),
            pl.BlockSpec((Q, 2 * LANES, SSM_HEADS * CHUNK), lambda i: (0, 0, 0), pipeline_mode=pl.Buffered(1)),
            pl.BlockSpec((CHUNK, CHUNK), const),
            pl.BlockSpec((rows, D_SSM), row),
            pl.BlockSpec((rows, SSM_GROUPS * D_STATE), row),
            pl.BlockSpec((rows, SSM_GROUPS * D_STATE), row),
            pl.BlockSpec((CHUNK, LANES), lambda i: (0, i)),
            pl.BlockSpec((rows, D_SSM), row),
            pl.BlockSpec((Q, SSM_GROUPS * D_STATE, GROUP_W), lambda i: (i, 0, 0)),
            pl.BlockSpec((Q, SSM_GROUPS * D_STATE, GROUP_W), lambda i: (i, 0, 0)),
            pl.BlockSpec((rows, D_ATTN), row),
            pl.BlockSpec((rows, D_MODEL), row),
            pl.BlockSpec((D_MIX, D_MODEL), const, pipeline_mode=pl.Buffered(1)),
            pl.BlockSpec((1, D_MODEL), const),
        ],
        out_specs=pl.BlockSpec((rows, D_MODEL), row),
        out_shape=jax.ShapeDtypeStruct((n, D_MODEL), F32),
        scratch_shapes=[pltpu.VMEM((rows, D_SSM), BF16), pltpu.VMEM((rows, D_SSM), F32)],
        compiler_params=pltpu.CompilerParams(dimension_semantics=("parallel",),
                                             vmem_limit_bytes=VMEM_LIMIT),
        name="ssd_out",
    )(a_log, dskip_x, ssd_norm_g, selb, tril, xs, bm, cm, dt, z, sf, sb, y_attn, x2, w_out, final_norm_g)


def _bucket_table_t():
    rel = jnp.arange(3 * BLK)[:, None] - BLK - jnp.arange(BLK)[None, :]
    half = NUM_BUCKETS // 2
    ret = (rel > 0).astype(jnp.int32) * half
    n = jnp.abs(rel)
    nf = jnp.maximum(n, 1).astype(F32)
    large = MAX_EXACT + (jnp.log(nf / MAX_EXACT) / math.log(MAX_DISTANCE / MAX_EXACT)
                         * (half - MAX_EXACT)).astype(jnp.int32)
    large = jnp.minimum(large, half - 1)
    return (ret + jnp.where(n < MAX_EXACT, n, large)).astype(jnp.int32)


def _attn_kernel(nblk, nstep, rb_ref, bucket_ref, sink_ref, gn_ref, q_ref, kp_ref, kc_ref, kn_ref,
                 vp_ref, vc_ref, vn_ref, ga_ref, y_ref,
                 tab_ref, pe_ref, mk_ref, ot_ref, vs_ref):
    g = pl.program_id(0)

    @pl.when(g == 0)
    def _init():
        pe_ref[...] = jnp.zeros_like(pe_ref)
        mk_ref[...] = jnp.zeros_like(mk_ref)
        ot_ref[...] = jnp.zeros_like(ot_ref)
        bk = bucket_ref[...]
        t = lax.broadcasted_iota(jnp.int32, (3 * BLK, BLK), 0)
        qq = lax.broadcasted_iota(jnp.int32, (3 * BLK, BLK), 1)
        in_window = jnp.abs(t - BLK - qq) <= WINDOW

        def per_head(h, carry):
            acc = jnp.zeros((3 * BLK, BLK), F32)
            for b in range(NUM_BUCKETS):
                acc = jnp.where(bk == b, rb_ref[b, h], acc)
            tab = jnp.where(in_window, acc * LOG2E, NEG_INF)
            tab_ref[0, h] = tab
            tab_ref[1, h] = jnp.where(t >= BLK, tab, NEG_INF)
            tab_ref[2, h] = jnp.where(t < 2 * BLK, tab, NEG_INF)
            return carry

        lax.fori_loop(0, ATTN_HEADS, per_head, 0)

    left = lax.broadcasted_iota(jnp.int32, (BLK, LANES), 1) < ATTN_HEAD_DIM
    left3 = lax.broadcasted_iota(jnp.int32, (3 * BLK, LANES), 1) < ATTN_HEAD_DIM
    s2 = sink_ref[...] * LOG2E

    def run(even, odd):
        first_blk = jnp.minimum(g, nstep - 1) * ATTN_NB
        kall = jnp.concatenate([kp_ref[...], kc_ref[...], kn_ref[...]], axis=0)
        vall = jnp.concatenate([vp_ref[...], vc_ref[...], vn_ref[...]], axis=0)
        one = jnp.ones((3 * BLK, LANES), BF16)

        def stage_a(j, p, var):
            kd = kall[j * BLK:(j + 3) * BLK, (p // 2) * LANES:(p // 2 + 1) * LANES]
            qt = q_ref[j * BLK:(j + 1) * BLK, p * LANES:(p + 1) * LANES]
            zero = jnp.zeros_like(qt)
            rhs_t = jnp.concatenate([jnp.where(left, qt, zero), jnp.where(left, zero, qt)], axis=0)
            lg = lax.dot_general(kd, rhs_t, (((1,), (1,)), ((), ())), preferred_element_type=F32)
            lg = lg + jnp.concatenate([tab_ref[var, 2 * p], tab_ref[var, 2 * p + 1]], axis=1)
            for kb in range(3):
                blk = lg[kb * BLK:(kb + 1) * BLK]
                mk = jnp.max(blk, axis=0, keepdims=True)
                pe_ref[even, j, p, kb * BLK:(kb + 1) * BLK, :] = jnp.exp2(blk - mk).astype(BF16)
                mk_ref[even, j, p, kb:kb + 1, :] = mk

        def stage_b(j, p):
            kv = p // 2
            vt = vall[j * BLK:(j + 3) * BLK, (kv // 2) * LANES:(kv // 2 + 1) * LANES]
            vmod = jnp.where(left3, vt, one) if kv % 2 == 0 else jnp.where(left3, one, vt)
            r0 = (kv % 2) * ATTN_HEAD_DIM
            d0 = ATTN_HEAD_DIM - r0
            mk = mk_ref[odd, j, p]
            sp = s2[p:p + 1, :]
            m = jnp.maximum(jnp.max(mk[0:3], axis=0, keepdims=True), sp)
            resc = jnp.exp2(mk[0:3] - m).astype(BF16)
            pes = jnp.concatenate([pe_ref[odd, j, p, kb * BLK:(kb + 1) * BLK, :] * resc[kb:kb + 1]
                                   for kb in range(3)], axis=0)
            o2 = lax.dot_general(vmod, pes, (((0,), (0,)), ((), ())), preferred_element_type=F32)
            o = o2[r0:r0 + ATTN_HEAD_DIM, :] / (o2[d0:d0 + 1, :] + jnp.exp2(sp - m))
            ot_ref[odd, j, 2 * p * ATTN_HEAD_DIM:(2 * p + 1) * ATTN_HEAD_DIM, :] = o[:, 0:BLK]
            ot_ref[odd, j, (2 * p + 1) * ATTN_HEAD_DIM:(2 * p + 2) * ATTN_HEAD_DIM, :] = o[:, BLK:2 * BLK]

        def stage_c(j, p, ssq):
            rows = slice(j * BLK, (j + 1) * BLK)
            cols = slice(p * LANES, (p + 1) * LANES)
            v = ot_ref[even, j, p * LANES:(p + 1) * LANES, :].T * ga_ref[rows, cols].astype(F32)
            vs_ref[rows, cols] = v
            return ssq + jnp.sum(v * v, axis=-1, keepdims=True)

        for j in range(ATTN_NB):
            pos = lax.rem(first_blk + j, nblk)
            var = jnp.where(pos == 0, 1, jnp.where(pos == nblk - 1, 2, 0))
            rows = slice(j * BLK, (j + 1) * BLK)
            ssq = jnp.zeros((BLK, 1), F32)
            for p in range(ATTN_HEADS // 2):
                stage_a(j, p, var)
                stage_b(j, p)
                ssq = stage_c(j, p, ssq)
            y_ref[rows, :] = (vs_ref[rows, :] * lax.rsqrt(ssq * (1.0 / D_ATTN) + EPS) * gn_ref[...]).astype(BF16)

    @pl.when(lax.rem(g, 2) == 0)
    def _():
        run(0, 1)

    @pl.when(lax.rem(g, 2) == 1)
    def _():
        run(1, 0)


def _attn(rel_bias, bucket_t, sink_x, attn_norm_g, q, kd, v, ga, batch, seq):
    n = q.shape[0]
    nblk = seq // BLK
    assert nblk % ATTN_NB == 0
    nstep = batch * nblk // ATTN_NB

    def step(lag):
        return lambda g: jnp.clip(g - lag, 0, nstep - 1)

    def cur(lag):
        return lambda g: (step(lag)(g), 0)

    def prev(lag):
        def index(g):
            b = step(lag)(g) * ATTN_NB
            return (b - jnp.where(lax.rem(b, nblk) > 0, 1, 0), 0)
        return index

    def nxt(lag):
        def index(g):
            b = step(lag)(g) * ATTN_NB + ATTN_NB - 1
            return (b + jnp.where(lax.rem(b, nblk) < nblk - 1, 1, 0), 0)
        return index

    const = lambda g: (0, 0)
    kw = 2 * KV_HEADS * ATTN_HEAD_DIM
    vw = KV_HEADS * ATTN_HEAD_DIM
    npair = ATTN_HEADS // 2
    rows = ATTN_NB * BLK
    return pl.pallas_call(
        lambda *refs: _attn_kernel(nblk, nstep, *refs),
        grid=(nstep + 2,),
        in_specs=[
            pl.BlockSpec(memory_space=pltpu.SMEM),
            pl.BlockSpec((3 * BLK, BLK), const),
            pl.BlockSpec((npair, 2 * BLK), const),
            pl.BlockSpec((1, D_ATTN), const),
            pl.BlockSpec((rows, D_ATTN), cur(0)),
            pl.BlockSpec((BLK, kw), prev(0)), pl.BlockSpec((rows, kw), cur(0)), pl.BlockSpec((BLK, kw), nxt(0)),
            pl.BlockSpec((BLK, vw), prev(1)), pl.BlockSpec((rows, vw), cur(1)), pl.BlockSpec((BLK, vw), nxt(1)),
            pl.BlockSpec((rows, D_ATTN), cur(2)),
        ],
        out_specs=pl.BlockSpec((rows, D_ATTN), cur(2)),
        out_shape=jax.ShapeDtypeStruct((n, D_ATTN), BF16),
        scratch_shapes=[pltpu.VMEM((3, ATTN_HEADS, 3 * BLK, BLK), F32),
                        pltpu.VMEM((2, ATTN_NB, npair, 3 * BLK, 2 * BLK), BF16),
                        pltpu.VMEM((2, ATTN_NB, npair, 8, 2 * BLK), F32),
                        pltpu.VMEM((2, ATTN_NB, D_ATTN, BLK), F32),
                        pltpu.VMEM((rows, D_ATTN), F32)],
        compiler_params=pltpu.CompilerParams(dimension_semantics=("arbitrary",),
                                             vmem_limit_bytes=VMEM_LIMIT),
        name="attn",
    )(rel_bias, bucket_t, sink_x, attn_norm_g, q, kd, kd, kd, v, v, v, ga)


def _arrange_w_in(w_in):
    wa = w_in[:, 0:WA_COLS].astype(BF16)
    wdt = jnp.pad(w_in[:, WA_COLS:WB_START], ((0, 0), (0, LANES - 2 * SSM_HEADS))).astype(BF16)
    wb = w_in[:, WB_START:WB_START + WB_COLS].astype(BF16)
    wk = wb[:, D_ATTN:OFF_V].reshape(D_MODEL, KV_HEADS, 1, ATTN_HEAD_DIM)
    wkd = jnp.broadcast_to(wk, (D_MODEL, KV_HEADS, 2, ATTN_HEAD_DIM)).reshape(D_MODEL, -1)
    return wa, wb, wkd, wdt


def _expand_matrices():
    e = np.zeros((Q, 2, 2 * LANES, D_SSM), np.float32)
    for c in range(Q):
        for d in range(2):
            for h in range(SSM_HEADS):
                r = (c * 2 + d) * SSM_HEADS + h
                e[c, d, r, h * SSM_HEAD_DIM:(h + 1) * SSM_HEAD_DIM] = 1.0
                e[c, d, LANES + r, h * SSM_HEAD_DIM:(h + 1) * SSM_HEAD_DIM] = 1.0
    return e


def _bcast_matrices():
    m = np.zeros((Q, 2 * LANES, SSM_HEADS * CHUNK), np.float32)
    for c in range(Q):
        for h in range(SSM_HEADS):
            r = (c * 2 + 1) * SSM_HEADS + h
            m[c, r, h * CHUNK:(h + 1) * CHUNK] = 1.0
            m[c, LANES + r, h * CHUNK:(h + 1) * CHUNK] = 1.0
    return m


def kernel(x, norm_in_g, w_in, conv_w, conv_b, dt_bias, a_log, d_skip, ssd_norm_g, rel_bias, sink,
           attn_norm_g, w_out, final_norm_g):
    batch, seq, _ = x.shape
    assert w_out.shape[0] == 1 and seq % (Q * CHUNK) == 0 and seq % TM == 0
    x2 = x.reshape(batch * seq, D_MODEL)

    weights = _arrange_w_in(w_in)
    e = jnp.asarray(_expand_matrices(), BF16)
    tril = jnp.asarray(np.tril(np.ones((CHUNK, CHUNK), np.float32)), BF16)
    a_log2 = jnp.tile(a_log.reshape(1, 2 * SSM_HEADS).astype(F32), (1, Q))
    dt_bias_x = jnp.pad(dt_bias.reshape(1, 2 * SSM_HEADS).astype(F32), ((0, 0), (0, LANES - 2 * SSM_HEADS)))
    dskip_x = jnp.repeat(d_skip.astype(F32), SSM_HEAD_DIM).reshape(1, D_SSM)
    sink_x = jnp.repeat(sink.astype(F32), BLK).reshape(ATTN_HEADS // 2, 2 * BLK)

    z, xs, bm, cm, dt, q, kd, v, ga = _inproj(
        x2, norm_in_g.reshape(1, D_MODEL), weights, conv_w, conv_b.reshape(1, CONV_CH),
        dt_bias_x, seq)
    sf, sb = _states(a_log2, e, tril, xs, bm, dt, batch, seq)
    y_attn = _attn(rel_bias.astype(F32), _bucket_table_t(), sink_x,
                   attn_norm_g.reshape(1, D_ATTN), q, kd, v, ga, batch, seq)
    out = _ssd_out(a_log2, dskip_x, ssd_norm_g.reshape(1, D_SSM), jnp.asarray(_bcast_matrices(), BF16),
                   tril, xs, bm, cm, dt, z, sf, sb, y_attn, x2, w_out[0].astype(BF16),
                   final_norm_g.reshape(1, D_MODEL))
    return out.reshape(batch, seq, D_MODEL)
```

```python
import math

import numpy as np
import jax
import jax.numpy as jnp
from jax import lax
from jax.experimental import pallas as pl
from jax.experimental.pallas import tpu as pltpu

F32 = jnp.float32
BF16 = jnp.bfloat16

D_MODEL = 1024
D_SSM = 1024
D_ATTN = 1024
D_MIX = D_SSM + D_ATTN
SSM_HEAD_DIM = 64
SSM_HEADS = D_SSM // SSM_HEAD_DIM
SSM_GROUPS = 2
GROUP_W = D_SSM // SSM_GROUPS
D_STATE = 128
D_CONV = 5
CHUNK = 128
CONV_CH = D_SSM + 2 * SSM_GROUPS * D_STATE
ATTN_HEAD_DIM = 64
ATTN_HEADS = D_ATTN // ATTN_HEAD_DIM
KV_HEADS = 4
WINDOW = 128
BLK = 128
NUM_BUCKETS = 32
MAX_DISTANCE = 128
MAX_EXACT = 8
EPS = 1e-6
NEG_INF = -1e30
SCALE = ATTN_HEAD_DIM ** -0.5
LOG2E = math.log2(math.e)

LANES = 128
HALO = 8
TM = 512
CONV_ROWS = 64
ATTN_NB = 4
Q = 4
VMEM_LIMIT = 56 * 1024 * 1024

WA_COLS = D_SSM + CONV_CH
OFF_Z, OFF_XBC = 0, D_SSM
WB_START = WA_COLS + 2 * SSM_HEADS
OFF_Q = 0
OFF_V = D_ATTN + KV_HEADS * ATTN_HEAD_DIM
OFF_GA = OFF_V + KV_HEADS * ATTN_HEAD_DIM
WB_COLS = OFF_GA + D_ATTN
assert TM == Q * CHUNK and Q * 2 * SSM_HEADS == LANES


def _silu(v):
    return v * jax.nn.sigmoid(v)


def _softplus(v):
    return jnp.maximum(v, 0.0) + jnp.log1p(jnp.exp(-jnp.abs(v)))


def _inproj_kernel(tiles_per_seq, x_ref, g_ref, wa_ref, wb_ref, wkd_ref, wdt_ref, cw_ref, cb_ref, dtb_ref,
                   z_ref, xs_ref, bm_ref, cm_ref, dt_ref, q_ref, kd_ref, v_ref, ga_ref, pad_ref):
    i = pl.program_id(0)

    @pl.when(i == 0)
    def _():
        pad_ref[...] = jnp.zeros_like(pad_ref)

    j = lax.rem(i, tiles_per_seq)
    g = g_ref[...]

    def proj(hv, w_ref, off, n):
        return jnp.dot(hv, w_ref[:, off:off + n], preferred_element_type=F32)

    xv = x_ref[...]
    ms = jnp.mean(xv * xv, axis=-1, keepdims=True)
    h = (xv * lax.rsqrt(ms + EPS) * g).astype(BF16)

    s_cur = lax.rem(i, 3)
    s_prev = lax.rem(i + 2, 3)
    s_next = lax.rem(i + 1, 3)
    xbc = proj(h, wa_ref, OFF_XBC, CONV_CH)
    pad_ref[s_cur, HALO:HALO + TM, :] = xbc
    pad_ref[s_prev, HALO + TM:2 * HALO + TM, :] = xbc[0:HALO] * jnp.where(j > 0, 1.0, 0.0)
    pad_ref[s_next, 0:HALO, :] = xbc[TM - HALO:TM] * jnp.where(j < tiles_per_seq - 1, 1.0, 0.0)

    z_ref[...] = _silu(proj(h, wa_ref, OFF_Z, D_SSM)).astype(BF16)
    q_ref[...] = (proj(h, wb_ref, OFF_Q, D_ATTN) * (SCALE * LOG2E)).astype(BF16)
    kd_ref[...] = proj(h, wkd_ref, 0, 2 * KV_HEADS * ATTN_HEAD_DIM).astype(BF16)
    v_ref[...] = proj(h, wb_ref, OFF_V, KV_HEADS * ATTN_HEAD_DIM).astype(BF16)
    ga_ref[...] = _silu(proj(h, wb_ref, OFF_GA, D_ATTN)).astype(BF16)
    dt_lane = lax.broadcasted_iota(jnp.int32, (TM, LANES), 1) < 2 * SSM_HEADS
    dt = jnp.where(dt_lane, _softplus(proj(h, wdt_ref, 0, LANES) + dtb_ref[...]), 0.0)
    dtc = dt[0:CHUNK]
    for c in range(1, Q):
        dtc = dtc + pltpu.roll(dt[c * CHUNK:(c + 1) * CHUNK], 2 * SSM_HEADS * c, axis=1)
    dt_ref[...] = dtc

    base = HALO - (D_CONV - 1) // 2
    for cc in range(CONV_CH // LANES):
        sl = slice(cc * LANES, (cc + 1) * LANES)
        if cc < D_SSM // LANES:
            dst, o = xs_ref, cc * LANES
        elif cc < (D_SSM + SSM_GROUPS * D_STATE) // LANES:
            dst, o = bm_ref, cc * LANES - D_SSM
        else:
            dst, o = cm_ref, cc * LANES - D_SSM - SSM_GROUPS * D_STATE
        for r0 in range(0, TM, CONV_ROWS):
            acc = cb_ref[:, sl] + cw_ref[0:1, sl] * pad_ref[s_prev, base + r0:base + r0 + CONV_ROWS, sl]
            for k in range(1, D_CONV):
                acc = acc + cw_ref[k:k + 1, sl] * pad_ref[s_prev, base + r0 + k:base + r0 + k + CONV_ROWS, sl]
            dst[r0:r0 + CONV_ROWS, o:o + LANES] = _silu(acc).astype(BF16)


def _inproj(x2, norm_in_g, weights, conv_w, conv_b, dt_bias, seq):
    n = x2.shape[0]
    tiles_per_seq = seq // TM
    ntile = n // TM
    cur = lambda i: (jnp.minimum(i, ntile - 1), 0)
    lag = lambda i: (jnp.maximum(i - 1, 0), 0)
    const = lambda i: (0, 0)
    out_cols = (D_SSM, D_SSM, SSM_GROUPS * D_STATE, SSM_GROUPS * D_STATE,
                D_ATTN, 2 * KV_HEADS * ATTN_HEAD_DIM, KV_HEADS * ATTN_HEAD_DIM, D_ATTN)
    out_maps = (cur, lag, lag, lag, cur, cur, cur, cur)
    out_specs = [pl.BlockSpec((TM, c), m) for c, m in zip(out_cols, out_maps)]
    out_shape = [jax.ShapeDtypeStruct((n, c), BF16) for c in out_cols]
    out_specs.insert(4, pl.BlockSpec((CHUNK, LANES), lambda i: (0, jnp.minimum(i, ntile - 1))))
    out_shape.insert(4, jax.ShapeDtypeStruct((CHUNK, ntile * LANES), F32))
    return pl.pallas_call(
        lambda *refs: _inproj_kernel(tiles_per_seq, *refs),
        grid=(ntile + 1,),
        in_specs=[
            pl.BlockSpec((TM, D_MODEL), cur),
            pl.BlockSpec((1, D_MODEL), const),
            pl.BlockSpec((D_MODEL, WA_COLS), const, pipeline_mode=pl.Buffered(1)),
            pl.BlockSpec((D_MODEL, WB_COLS), const, pipeline_mode=pl.Buffered(1)),
            pl.BlockSpec((D_MODEL, 2 * KV_HEADS * ATTN_HEAD_DIM), const, pipeline_mode=pl.Buffered(1)),
            pl.BlockSpec((D_MODEL, LANES), const, pipeline_mode=pl.Buffered(1)),
            pl.BlockSpec((D_CONV, CONV_CH), const),
            pl.BlockSpec((1, CONV_CH), const),
            pl.BlockSpec((1, LANES), const),
        ],
        out_specs=out_specs,
        out_shape=out_shape,
        scratch_shapes=[pltpu.VMEM((3, TM + 2 * HALO, CONV_CH), F32)],
        compiler_params=pltpu.CompilerParams(dimension_semantics=("arbitrary",),
                                             vmem_limit_bytes=VMEM_LIMIT),
        name="inproj",
    )(x2, norm_in_g, *weights, conv_w, conv_b, dt_bias)


def _split_terms(v, n):
    terms, r = [], v
    for _ in range(n):
        t = r.astype(BF16)
        terms.append(t)
        r = r - t.astype(F32)
    return terms


def _chunk_cumsum(tril, a):
    t0, t1, t2 = _split_terms(a, 3)
    return (jnp.dot(tril, t0, preferred_element_type=F32) + jnp.dot(tril, t1, preferred_element_type=F32)
            + jnp.dot(tril, t2, preferred_element_type=F32))


def _hi_lo(v):
    return jnp.concatenate(_split_terms(v, 2), axis=1)


def _fwd_lanes():
    lane = lax.broadcasted_iota(jnp.int32, (CHUNK, LANES), 1)
    return jnp.bitwise_and(lane, 2 * SSM_HEADS - 1) < SSM_HEADS


def _states_kernel(alog_ref, e_ref, tril_ref, xsf_ref, bf_ref, dtf_ref, xsb_ref, bb_ref, dtb_ref,
                   sf_ref, sb_ref, cf_ref, cb_ref):
    @pl.when(pl.program_id(1) == 0)
    def _():
        cf_ref[...] = jnp.zeros_like(cf_ref)
        cb_ref[...] = jnp.zeros_like(cb_ref)

    a_neg = -jnp.exp(alog_ref[...])
    tril = tril_ref[...]
    fwd_lane = _fwd_lanes()

    def prep(dt_ref):
        dt = dt_ref[...]
        a = dt * a_neg
        cs = _chunk_cumsum(tril, a)
        tot = cs[CHUNK - 1:CHUNK, :]
        w = dt * jnp.exp(jnp.where(fwd_lane, tot - cs, cs - a))
        return _hi_lo(jnp.concatenate([w, jnp.broadcast_to(jnp.exp(tot), (8, LANES))], axis=0))

    def chunk(c, d, cat, xs_ref, b_ref, out_ref, carry_ref):
        rows = slice(c * CHUNK, (c + 1) * CHUNK)
        wx = jnp.dot(cat, e_ref[c, d], preferred_element_type=F32)
        xd = (xs_ref[rows, :].astype(F32) * wx[0:CHUNK]).astype(BF16)
        dec = wx[CHUNK:CHUNK + 1, :]
        for g in range(SSM_GROUPS):
            gr = slice(g * D_STATE, (g + 1) * D_STATE)
            gc = slice(g * GROUP_W, (g + 1) * GROUP_W)
            st = lax.dot_general(b_ref[rows, gr], xd[:, gc], (((0,), (0,)), ((), ())),
                                 preferred_element_type=F32)
            prev = carry_ref[gr, :]
            out_ref[c, gr, :] = prev.astype(BF16)
            carry_ref[gr, :] = prev * dec[:, gc] + st

    cat_f = prep(dtf_ref)
    cat_b = prep(dtb_ref)
    for c in range(Q):
        chunk(c, 0, cat_f, xsf_ref, bf_ref, sf_ref, cf_ref)
    for c in reversed(range(Q)):
        chunk(c, 1, cat_b, xsb_ref, bb_ref, sb_ref, cb_ref)


def _states(a_log, e, tril, xs, bm, dt, batch, seq):
    nstep = seq // (Q * CHUNK)
    nchunk = batch * seq // CHUNK
    fwd = lambda b, i: (b * nstep + i, 0)
    bwd = lambda b, i: (b * nstep + nstep - 1 - i, 0)
    const = lambda b, i: (0, 0)
    rows = Q * CHUNK
    st_shape = jax.ShapeDtypeStruct((nchunk, SSM_GROUPS * D_STATE, GROUP_W), BF16)
    return pl.pallas_call(
        _states_kernel,
        grid=(batch, nstep),
        in_specs=[
            pl.BlockSpec((1, LANES), const),
            pl.BlockSpec((Q, 2, 2 * LANES, D_SSM), lambda b, i: (0, 0, 0, 0), pipeline_mode=pl.Buffered(1)),
            pl.BlockSpec((CHUNK, CHUNK), const),
            pl.BlockSpec((rows, D_SSM), fwd),
            pl.BlockSpec((rows, SSM_GROUPS * D_STATE), fwd),
            pl.BlockSpec((CHUNK, LANES), lambda b, i: (0, b * nstep + i)),
            pl.BlockSpec((rows, D_SSM), bwd),
            pl.BlockSpec((rows, SSM_GROUPS * D_STATE), bwd),
            pl.BlockSpec((CHUNK, LANES), lambda b, i: (0, b * nstep + nstep - 1 - i)),
        ],
        out_specs=[
            pl.BlockSpec((Q, SSM_GROUPS * D_STATE, GROUP_W), lambda b, i: (b * nstep + i, 0, 0)),
            pl.BlockSpec((Q, SSM_GROUPS * D_STATE, GROUP_W), lambda b, i: (b * nstep + nstep - 1 - i, 0, 0)),
        ],
        out_shape=[st_shape, st_shape],
        scratch_shapes=[pltpu.VMEM((SSM_GROUPS * D_STATE, GROUP_W), F32),
                        pltpu.VMEM((SSM_GROUPS * D_STATE, GROUP_W), F32)],
        compiler_params=pltpu.CompilerParams(dimension_semantics=("arbitrary", "arbitrary"),
                                             vmem_limit_bytes=VMEM_LIMIT),
        name="ssd_states",
    )(a_log, e, tril, xs, bm, dt, xs, bm, dt)


def _ssd_out_kernel(alog_ref, dskip_ref, gn_ref, selb_ref, tril_ref, xs_ref, bm_ref, cm_ref, dt_ref,
                    z_ref, sf_ref, sb_ref, ya_ref, x_ref, w_ref, fg_ref, o_ref, ys_ref, vs_ref):
    li = lax.broadcasted_iota(jnp.int32, (CHUNK, CHUNK), 0)
    si = lax.broadcasted_iota(jnp.int32, (CHUNK, CHUNK), 1)
    lower = li >= si
    eye = li == si
    left = lax.broadcasted_iota(jnp.int32, (CHUNK, LANES), 1) < SSM_HEAD_DIM

    dt = dt_ref[...]
    a = dt * (-LOG2E * jnp.exp(alog_ref[...]))
    cs = _chunk_cumsum(tril_ref[...], a)
    tot = cs[CHUNK - 1:CHUNK, :]
    colq = jnp.where(_fwd_lanes(), cs, tot - cs + a)
    rowq = (colq - jnp.log2(dt)).T
    dtr = dt.T.astype(BF16)
    catq = _hi_lo(colq)

    for c in range(Q):
        rows = slice(c * CHUNK, (c + 1) * CHUNK)
        for g in range(SSM_GROUPS):
            gr = slice(g * D_STATE, (g + 1) * D_STATE)
            cmg = cm_ref[rows, gr]
            cb = lax.dot_general(cmg, bm_ref[rows, gr], (((1,), (1,)), ((), ())),
                                 preferred_element_type=F32)
            cbb = cb.astype(BF16)
            cbeye = jnp.where(eye, cb, 0.0).astype(BF16)
            ssq = jnp.zeros((CHUNK, 1), F32)
            for pp in range(GROUP_W // LANES):
                hd0 = g * (SSM_HEADS // SSM_GROUPS) + pp * 2
                cols = slice(hd0 * SSM_HEAD_DIM, (hd0 + 2) * SSM_HEAD_DIM)
                if pp % 2 == 0:
                    c2 = slice(pp * LANES, (pp + 2) * LANES)
                    csf = jnp.dot(cmg, sf_ref[c, gr, c2], preferred_element_type=F32)
                    csb = jnp.dot(cmg, sb_ref[c, gr, c2], preferred_element_type=F32)
                pc = slice((pp % 2) * LANES, (pp % 2 + 1) * LANES)
                colb = jnp.dot(catq, selb_ref[c, :, hd0 * CHUNK:(hd0 + 2) * CHUNK],
                               preferred_element_type=F32)
                scores, colf_bc, colb_bc = [], [], []
                for hh in range(2):
                    h = c * 2 * SSM_HEADS + hd0 + hh
                    hb = h + SSM_HEADS
                    colf_bc.append(jnp.broadcast_to(colq[:, h:h + 1], (CHUNK, CHUNK)))
                    colb_bc.append(colb[:, hh * CHUNK:(hh + 1) * CHUNK])
                    arg = jnp.where(lower, colf_bc[hh] - rowq[h:h + 1, :], colb_bc[hh] - rowq[hb:hb + 1, :])
                    scores.append(cbb * jnp.exp2(arg).astype(BF16) + cbeye * dtr[hb:hb + 1, :])
                xt = xs_ref[rows, cols]
                zero = jnp.zeros_like(xt)
                rhs = jnp.concatenate([jnp.where(left, xt, zero), jnp.where(left, zero, xt)], axis=0)
                y_diag = jnp.dot(jnp.concatenate(scores, axis=1), rhs, preferred_element_type=F32)
                y_off = (csf[:, pc] * jnp.exp2(jnp.where(left, colf_bc[0], colf_bc[1]))
                         + csb[:, pc] * jnp.exp2(jnp.where(left, colb_bc[0], colb_bc[1])))
                vg = (y_diag + y_off + dskip_ref[:, cols] * xt.astype(F32)) * z_ref[rows, cols].astype(F32)
                vs_ref[rows, cols] = vg
                ssq = ssq + jnp.sum(vg * vg, axis=-1, keepdims=True)
            gc = slice(g * GROUP_W, (g + 1) * GROUP_W)
            ys_ref[rows, gc] = (vs_ref[rows, gc] * lax.rsqrt(ssq * (1.0 / GROUP_W) + EPS)
                                * gn_ref[:, gc]).astype(BF16)

        acc = (jnp.dot(ys_ref[rows, :], w_ref[0:D_SSM, :], preferred_element_type=F32)
               + jnp.dot(ya_ref[rows, :], w_ref[D_SSM:D_MIX, :], preferred_element_type=F32))
        hres = x_ref[rows, :] + acc
        ms = jnp.mean(hres * hres, axis=-1, keepdims=True)
        o_ref[rows, :] = hres * lax.rsqrt(ms + EPS) * fg_ref[...]


def _ssd_out(a_log, dskip_x, ssd_norm_g, selb, tril, xs, bm, cm, dt, z, sf, sb, y_attn, x2, w_out, final_norm_g):
    n = xs.shape[0]
    rows = Q * CHUNK
    row = lambda i: (i, 0)
    const = lambda i: (0, 0)
    return pl.pallas_call(
        _ssd_out_kernel,
        grid=(n // rows,),
        in_specs=[
            pl.BlockSpec((1, LANES), const),
            pl.BlockSpec((1, D_SSM), const),
            pl.BlockSpec((1, D_SSM), const),
            pl.BlockSpec((Q, 2 * LANES, SSM_HEADS * CHUNK), lambda i: (0, 0, 0), pipeline_mode=pl.Buffered(1)),
            pl.BlockSpec((CHUNK, CHUNK), const),
            pl.BlockSpec((rows, D_SSM), row),
            pl.BlockSpec((rows, SSM_GROUPS * D_STATE), row),
            pl.BlockSpec((rows, SSM_GROUPS * D_STATE), row),
            pl.BlockSpec((CHUNK, LANES), lambda i: (0, i)),
            pl.BlockSpec((rows, D_SSM), row),
            pl.BlockSpec((Q, SSM_GROUPS * D_STATE, GROUP_W), lambda i: (i, 0, 0)),
            pl.BlockSpec((Q, SSM_GROUPS * D_STATE, GROUP_W), lambda i: (i, 0, 0)),
            pl.BlockSpec((rows, D_ATTN), row),
            pl.BlockSpec((rows, D_MODEL), row),
            pl.BlockSpec((D_MIX, D_MODEL), const, pipeline_mode=pl.Buffered(1)),
            pl.BlockSpec((1, D_MODEL), const),
        ],
        out_specs=pl.BlockSpec((rows, D_MODEL), row),
        out_shape=jax.ShapeDtypeStruct((n, D_MODEL), F32),
        scratch_shapes=[pltpu.VMEM((rows, D_SSM), BF16), pltpu.VMEM((rows, D_SSM), F32)],
        compiler_params=pltpu.CompilerParams(dimension_semantics=("parallel",),
                                             vmem_limit_bytes=VMEM_LIMIT),
        name="ssd_out",
    )(a_log, dskip_x, ssd_norm_g, selb, tril, xs, bm, cm, dt, z, sf, sb, y_attn, x2, w_out, final_norm_g)


def _bucket_table_t():
    rel = jnp.arange(3 * BLK)[:, None] - BLK - jnp.arange(BLK)[None, :]
    half = NUM_BUCKETS // 2
    ret = (rel > 0).astype(jnp.int32) * half
    n = jnp.abs(rel)
    nf = jnp.maximum(n, 1).astype(F32)
    large = MAX_EXACT + (jnp.log(nf / MAX_EXACT) / math.log(MAX_DISTANCE / MAX_EXACT)
                         * (half - MAX_EXACT)).astype(jnp.int32)
    large = jnp.minimum(large, half - 1)
    return (ret + jnp.where(n < MAX_EXACT, n, large)).astype(jnp.int32)


def _attn_kernel(nblk, nstep, rb_ref, bucket_ref, sink_ref, gn_ref, q_ref, kp_ref, kc_ref, kn_ref,
                 vp_ref, vc_ref, vn_ref, ga_ref, y_ref,
                 tab_ref, pe_ref, mk_ref, ot_ref, vs_ref):
    g = pl.program_id(0)

    @pl.when(g == 0)
    def _init():
        pe_ref[...] = jnp.zeros_like(pe_ref)
        mk_ref[...] = jnp.zeros_like(mk_ref)
        ot_ref[...] = jnp.zeros_like(ot_ref)
        bk = bucket_ref[...]
        t = lax.broadcasted_iota(jnp.int32, (3 * BLK, BLK), 0)
        qq = lax.broadcasted_iota(jnp.int32, (3 * BLK, BLK), 1)
        in_window = jnp.abs(t - BLK - qq) <= WINDOW

        def per_head(h, carry):
            acc = jnp.zeros((3 * BLK, BLK), F32)
            for b in range(NUM_BUCKETS):
                acc = jnp.where(bk == b, rb_ref[b, h], acc)
            tab = jnp.where(in_window, acc * LOG2E, NEG_INF)
            tab_ref[0, h] = tab
            tab_ref[1, h] = jnp.where(t >= BLK, tab, NEG_INF)
            tab_ref[2, h] = jnp.where(t < 2 * BLK, tab, NEG_INF)
            return carry

        lax.fori_loop(0, ATTN_HEADS, per_head, 0)

    left = lax.broadcasted_iota(jnp.int32, (BLK, LANES), 1) < ATTN_HEAD_DIM
    left3 = lax.broadcasted_iota(jnp.int32, (3 * BLK, LANES), 1) < ATTN_HEAD_DIM
    s2 = sink_ref[...] * LOG2E

    def run(even, odd):
        first_blk = jnp.minimum(g, nstep - 1) * ATTN_NB
        kall = jnp.concatenate([kp_ref[...], kc_ref[...], kn_ref[...]], axis=0)
        vall = jnp.concatenate([vp_ref[...], vc_ref[...], vn_ref[...]], axis=0)
        one = jnp.ones((3 * BLK, LANES), BF16)

        def stage_a(j, p, var):
            kd = kall[j * BLK:(j + 3) * BLK, (p // 2) * LANES:(p // 2 + 1) * LANES]
            qt = q_ref[j * BLK:(j + 1) * BLK, p * LANES:(p + 1) * LANES]
            zero = jnp.zeros_like(qt)
            rhs_t = jnp.concatenate([jnp.where(left, qt, zero), jnp.where(left, zero, qt)], axis=0)
            lg = lax.dot_general(kd, rhs_t, (((1,), (1,)), ((), ())), preferred_element_type=F32)
            lg = lg + jnp.concatenate([tab_ref[var, 2 * p], tab_ref[var, 2 * p + 1]], axis=1)
            for kb in range(3):
                blk = lg[kb * BLK:(kb + 1) * BLK]
                mk = jnp.max(blk, axis=0, keepdims=True)
                pe_ref[even, j, p, kb * BLK:(kb + 1) * BLK, :] = jnp.exp2(blk - mk).astype(BF16)
                mk_ref[even, j, p, kb:kb + 1, :] = mk

        def stage_b(j, p):
            kv = p // 2
            vt = vall[j * BLK:(j + 3) * BLK, (kv // 2) * LANES:(kv // 2 + 1) * LANES]
            vmod = jnp.where(left3, vt, one) if kv % 2 == 0 else jnp.where(left3, one, vt)
            r0 = (kv % 2) * ATTN_HEAD_DIM
            d0 = ATTN_HEAD_DIM - r0
            mk = mk_ref[odd, j, p]
            sp = s2[p:p + 1, :]
            m = jnp.maximum(jnp.max(mk[0:3], axis=0, keepdims=True), sp)
            resc = jnp.exp2(mk[0:3] - m).astype(BF16)
            pes = jnp.concatenate([pe_ref[odd, j, p, kb * BLK:(kb + 1) * BLK, :] * resc[kb:kb + 1]
                                   for kb in range(3)], axis=0)
            o2 = lax.dot_general(vmod, pes, (((0,), (0,)), ((), ())), preferred_element_type=F32)
            o = o2[r0:r0 + ATTN_HEAD_DIM, :] / (o2[d0:d0 + 1, :] + jnp.exp2(sp - m))
            ot_ref[odd, j, 2 * p * ATTN_HEAD_DIM:(2 * p + 1) * ATTN_HEAD_DIM, :] = o[:, 0:BLK]
            ot_ref[odd, j, (2 * p + 1) * ATTN_HEAD_DIM:(2 * p + 2) * ATTN_HEAD_DIM, :] = o[:, BLK:2 * BLK]

        def stage_c(j, p, ssq):
            rows = slice(j * BLK, (j + 1) * BLK)
            cols = slice(p * LANES, (p + 1) * LANES)
            v = ot_ref[even, j, p * LANES:(p + 1) * LANES, :].T * ga_ref[rows, cols].astype(F32)
            vs_ref[rows, cols] = v
            return ssq + jnp.sum(v * v, axis=-1, keepdims=True)

        for j in range(ATTN_NB):
            pos = lax.rem(first_blk + j, nblk)
            var = jnp.where(pos == 0, 1, jnp.where(pos == nblk - 1, 2, 0))
            rows = slice(j * BLK, (j + 1) * BLK)
            ssq = jnp.zeros((BLK, 1), F32)
            for p in range(ATTN_HEADS // 2):
                stage_a(j, p, var)
                stage_b(j, p)
                ssq = stage_c(j, p, ssq)
            y_ref[rows, :] = (vs_ref[rows, :] * lax.rsqrt(ssq * (1.0 / D_ATTN) + EPS) * gn_ref[...]).astype(BF16)

    @pl.when(lax.rem(g, 2) == 0)
    def _():
        run(0, 1)

    @pl.when(lax.rem(g, 2) == 1)
    def _():
        run(1, 0)


def _attn(rel_bias, bucket_t, sink_x, attn_norm_g, q, kd, v, ga, batch, seq):
    n = q.shape[0]
    nblk = seq // BLK
    assert nblk % ATTN_NB == 0
    nstep = batch * nblk // ATTN_NB

    def step(lag):
        return lambda g: jnp.clip(g - lag, 0, nstep - 1)

    def cur(lag):
        return lambda g: (step(lag)(g), 0)

    def prev(lag):
        def index(g):
            b = step(lag)(g) * ATTN_NB
            return (b - jnp.where(lax.rem(b, nblk) > 0, 1, 0), 0)
        return index

    def nxt(lag):
        def index(g):
            b = step(lag)(g) * ATTN_NB + ATTN_NB - 1
            return (b + jnp.where(lax.rem(b, nblk) < nblk - 1, 1, 0), 0)
        return index

    const = lambda g: (0, 0)
    kw = 2 * KV_HEADS * ATTN_HEAD_DIM
    vw = KV_HEADS * ATTN_HEAD_DIM
    npair = ATTN_HEADS // 2
    rows = ATTN_NB * BLK
    return pl.pallas_call(
        lambda *refs: _attn_kernel(nblk, nstep, *refs),
        grid=(nstep + 2,),
        in_specs=[
            pl.BlockSpec(memory_space=pltpu.SMEM),
            pl.BlockSpec((3 * BLK, BLK), const),
            pl.BlockSpec((npair, 2 * BLK), const),
            pl.BlockSpec((1, D_ATTN), const),
            pl.BlockSpec((rows, D_ATTN), cur(0)),
            pl.BlockSpec((BLK, kw), prev(0)), pl.BlockSpec((rows, kw), cur(0)), pl.BlockSpec((BLK, kw), nxt(0)),
            pl.BlockSpec((BLK, vw), prev(1)), pl.BlockSpec((rows, vw), cur(1)), pl.BlockSpec((BLK, vw), nxt(1)),
            pl.BlockSpec((rows, D_ATTN), cur(2)),
        ],
        out_specs=pl.BlockSpec((rows, D_ATTN), cur(2)),
        out_shape=jax.ShapeDtypeStruct((n, D_ATTN), BF16),
        scratch_shapes=[pltpu.VMEM((3, ATTN_HEADS, 3 * BLK, BLK), F32),
                        pltpu.VMEM((2, ATTN_NB, npair, 3 * BLK, 2 * BLK), BF16),
                        pltpu.VMEM((2, ATTN_NB, npair, 8, 2 * BLK), F32),
                        pltpu.VMEM((2, ATTN_NB, D_ATTN, BLK), F32),
                        pltpu.VMEM((rows, D_ATTN), F32)],
        compiler_params=pltpu.CompilerParams(dimension_semantics=("arbitrary",),
                                             vmem_limit_bytes=VMEM_LIMIT),
        name="attn",
    )(rel_bias, bucket_t, sink_x, attn_norm_g, q, kd, kd, kd, v, v, v, ga)


def _arrange_w_in_kernel(w_ref, wa_ref, wb_ref, wkd_ref, wdt_ref):
    w = w_ref[...]
    wa_ref[...] = w[:, 0:WA_COLS].astype(BF16)
    wb = w[:, WB_START:WB_START + WB_COLS]
    wb_ref[...] = wb.astype(BF16)
    lane = lax.broadcasted_iota(jnp.int32, (w.shape[0], LANES), 1)
    wdt_ref[...] = jnp.where(lane < 2 * SSM_HEADS, w[:, WA_COLS:WA_COLS + LANES], 0.0).astype(BF16)
    for kv in range(KV_HEADS):
        wk = wb[:, D_ATTN + kv * ATTN_HEAD_DIM:D_ATTN + (kv + 1) * ATTN_HEAD_DIM]
        wkd_ref[:, kv * LANES:(kv + 1) * LANES] = jnp.concatenate([wk, wk], axis=1).astype(BF16)


def _arrange_w_in(w_in):
    rows = LANES
    row = lambda i: (i, 0)
    cols = (WA_COLS, WB_COLS, 2 * KV_HEADS * ATTN_HEAD_DIM, LANES)
    return pl.pallas_call(
        _arrange_w_in_kernel,
        grid=(D_MODEL // rows,),
        in_specs=[pl.BlockSpec((rows, w_in.shape[1]), row)],
        out_specs=[pl.BlockSpec((rows, c), row) for c in cols],
        out_shape=[jax.ShapeDtypeStruct((D_MODEL, c), BF16) for c in cols],
        compiler_params=pltpu.CompilerParams(dimension_semantics=("parallel",)),
        name="w_in_prep",
    )(w_in)


def _expand_matrices():
    e = np.zeros((Q, 2, 2 * LANES, D_SSM), np.float32)
    for c in range(Q):
        for d in range(2):
            for h in range(SSM_HEADS):
                r = (c * 2 + d) * SSM_HEADS + h
                e[c, d, r, h * SSM_HEAD_DIM:(h + 1) * SSM_HEAD_DIM] = 1.0
                e[c, d, LANES + r, h * SSM_HEAD_DIM:(h + 1) * SSM_HEAD_DIM] = 1.0
    return e


def _bcast_matrices():
    m = np.zeros((Q, 2 * LANES, SSM_HEADS * CHUNK), np.float32)
    for c in range(Q):
        for h in range(SSM_HEADS):
            r = (c * 2 + 1) * SSM_HEADS + h
            m[c, r, h * CHUNK:(h + 1) * CHUNK] = 1.0
            m[c, LANES + r, h * CHUNK:(h + 1) * CHUNK] = 1.0
    return m


def kernel(x, norm_in_g, w_in, conv_w, conv_b, dt_bias, a_log, d_skip, ssd_norm_g, rel_bias, sink,
           attn_norm_g, w_out, final_norm_g):
    batch, seq, _ = x.shape
    assert w_out.shape[0] == 1 and seq % (Q * CHUNK) == 0 and seq % TM == 0
    x2 = x.reshape(batch * seq, D_MODEL)

    weights = _arrange_w_in(w_in)
    e = jnp.asarray(_expand_matrices(), BF16)
    tril = jnp.asarray(np.tril(np.ones((CHUNK, CHUNK), np.float32)), BF16)
    a_log2 = jnp.tile(a_log.reshape(1, 2 * SSM_HEADS).astype(F32), (1, Q))
    dt_bias_x = jnp.pad(dt_bias.reshape(1, 2 * SSM_HEADS).astype(F32), ((0, 0), (0, LANES - 2 * SSM_HEADS)))
    dskip_x = jnp.repeat(d_skip.astype(F32), SSM_HEAD_DIM).reshape(1, D_SSM)
    sink_x = jnp.repeat(sink.astype(F32), BLK).reshape(ATTN_HEADS // 2, 2 * BLK)

    z, xs, bm, cm, dt, q, kd, v, ga = _inproj(
        x2, norm_in_g.reshape(1, D_MODEL), weights, conv_w, conv_b.reshape(1, CONV_CH),
        dt_bias_x, seq)
    sf, sb = _states(a_log2, e, tril, xs, bm, dt, batch, seq)
    y_attn = _attn(rel_bias.astype(F32), _bucket_table_t(), sink_x,
                   attn_norm_g.reshape(1, D_ATTN), q, kd, v, ga, batch, seq)
    out = _ssd_out(a_log2, dskip_x, ssd_norm_g.reshape(1, D_SSM), jnp.asarray(_bcast_matrices(), BF16),
                   tril, xs, bm, cm, dt, z, sf, sb, y_attn, x2, w_out[0].astype(BF16),
                   final_norm_g.reshape(1, D_MODEL))
    return out.reshape(batch, seq, D_MODEL)
```

```python
import math

import numpy as np
import jax
import jax.numpy as jnp
from jax import lax
from jax.experimental import pallas as pl
from jax.experimental.pallas import tpu as pltpu

F32 = jnp.float32
BF16 = jnp.bfloat16

D_MODEL = 1024
D_SSM = 1024
D_ATTN = 1024
D_MIX = D_SSM + D_ATTN
SSM_HEAD_DIM = 64
SSM_HEADS = D_SSM // SSM_HEAD_DIM
SSM_GROUPS = 2
GROUP_W = D_SSM // SSM_GROUPS
D_STATE = 128
D_CONV = 5
CHUNK = 128
CONV_CH = D_SSM + 2 * SSM_GROUPS * D_STATE
ATTN_HEAD_DIM = 64
ATTN_HEADS = D_ATTN // ATTN_HEAD_DIM
KV_HEADS = 4
WINDOW = 128
BLK = 128
NUM_BUCKETS = 32
MAX_DISTANCE = 128
MAX_EXACT = 8
EPS = 1e-6
NEG_INF = -1e30
SCALE = ATTN_HEAD_DIM ** -0.5
LOG2E = math.log2(math.e)

LANES = 128
HALO = 8
TM = 512
CONV_ROWS = 64
ATTN_NB = 4
Q = 4
VMEM_LIMIT = 56 * 1024 * 1024

WA_COLS = D_SSM + CONV_CH
OFF_Z, OFF_XBC = 0, D_SSM
WB_START = WA_COLS + 2 * SSM_HEADS
OFF_Q = 0
OFF_V = D_ATTN + KV_HEADS * ATTN_HEAD_DIM
OFF_GA = OFF_V + KV_HEADS * ATTN_HEAD_DIM
WB_COLS = OFF_GA + D_ATTN
assert TM == Q * CHUNK and Q * 2 * SSM_HEADS == LANES


def _silu(v):
    return v * jax.nn.sigmoid(v)


def _softplus(v):
    return jnp.maximum(v, 0.0) + jnp.log1p(jnp.exp(-jnp.abs(v)))


def _inproj_kernel(tiles_per_seq, x_ref, g_ref, wa_ref, wb_ref, wkd_ref, wdt_ref, cw_ref, cb_ref, dtb_ref,
                   z_ref, xs_ref, bm_ref, cm_ref, dt_ref, q_ref, kd_ref, v_ref, ga_ref, pad_ref):
    i = pl.program_id(0)

    @pl.when(i == 0)
    def _():
        pad_ref[...] = jnp.zeros_like(pad_ref)

    j = lax.rem(i, tiles_per_seq)
    g = g_ref[...]

    def proj(hv, w_ref, off, n):
        return jnp.dot(hv, w_ref[:, off:off + n], preferred_element_type=F32)

    xv = x_ref[...]
    ms = jnp.mean(xv * xv, axis=-1, keepdims=True)
    h = (xv * lax.rsqrt(ms + EPS) * g).astype(BF16)

    s_cur = lax.rem(i, 3)
    s_prev = lax.rem(i + 2, 3)
    s_next = lax.rem(i + 1, 3)
    xbc = proj(h, wa_ref, OFF_XBC, CONV_CH)
    pad_ref[s_cur, HALO:HALO + TM, :] = xbc
    pad_ref[s_prev, HALO + TM:2 * HALO + TM, :] = xbc[0:HALO] * jnp.where(j > 0, 1.0, 0.0)
    pad_ref[s_next, 0:HALO, :] = xbc[TM - HALO:TM] * jnp.where(j < tiles_per_seq - 1, 1.0, 0.0)

    z_ref[...] = _silu(proj(h, wa_ref, OFF_Z, D_SSM)).astype(BF16)
    q_ref[...] = (proj(h, wb_ref, OFF_Q, D_ATTN) * (SCALE * LOG2E)).astype(BF16)
    kd_ref[...] = proj(h, wkd_ref, 0, 2 * KV_HEADS * ATTN_HEAD_DIM).astype(BF16)
    v_ref[...] = proj(h, wb_ref, OFF_V, KV_HEADS * ATTN_HEAD_DIM).astype(BF16)
    ga_ref[...] = _silu(proj(h, wb_ref, OFF_GA, D_ATTN)).astype(BF16)
    dt_lane = lax.broadcasted_iota(jnp.int32, (TM, LANES), 1) < 2 * SSM_HEADS
    dt = jnp.where(dt_lane, _softplus(proj(h, wdt_ref, 0, LANES) + dtb_ref[...]), 0.0)
    dtc = dt[0:CHUNK]
    for c in range(1, Q):
        dtc = dtc + pltpu.roll(dt[c * CHUNK:(c + 1) * CHUNK], 2 * SSM_HEADS * c, axis=1)
    dt_ref[...] = dtc

    base = HALO - (D_CONV - 1) // 2
    for cc in range(CONV_CH // LANES):
        sl = slice(cc * LANES, (cc + 1) * LANES)
        if cc < D_SSM // LANES:
            dst, o = xs_ref, cc * LANES
        elif cc < (D_SSM + SSM_GROUPS * D_STATE) // LANES:
            dst, o = bm_ref, cc * LANES - D_SSM
        else:
            dst, o = cm_ref, cc * LANES - D_SSM - SSM_GROUPS * D_STATE
        for r0 in range(0, TM, CONV_ROWS):
            acc = cb_ref[:, sl] + cw_ref[0:1, sl] * pad_ref[s_prev, base + r0:base + r0 + CONV_ROWS, sl]
            for k in range(1, D_CONV):
                acc = acc + cw_ref[k:k + 1, sl] * pad_ref[s_prev, base + r0 + k:base + r0 + k + CONV_ROWS, sl]
            dst[r0:r0 + CONV_ROWS, o:o + LANES] = _silu(acc).astype(BF16)


def _inproj(x2, norm_in_g, weights, conv_w, conv_b, dt_bias, seq):
    n = x2.shape[0]
    tiles_per_seq = seq // TM
    ntile = n // TM
    cur = lambda i: (jnp.minimum(i, ntile - 1), 0)
    lag = lambda i: (jnp.maximum(i - 1, 0), 0)
    const = lambda i: (0, 0)
    out_cols = (D_SSM, D_SSM, SSM_GROUPS * D_STATE, SSM_GROUPS * D_STATE,
                D_ATTN, 2 * KV_HEADS * ATTN_HEAD_DIM, KV_HEADS * ATTN_HEAD_DIM, D_ATTN)
    out_maps = (cur, lag, lag, lag, cur, cur, cur, cur)
    out_specs = [pl.BlockSpec((TM, c), m) for c, m in zip(out_cols, out_maps)]
    out_shape = [jax.ShapeDtypeStruct((n, c), BF16) for c in out_cols]
    out_specs.insert(4, pl.BlockSpec((CHUNK, LANES), lambda i: (0, jnp.minimum(i, ntile - 1))))
    out_shape.insert(4, jax.ShapeDtypeStruct((CHUNK, ntile * LANES), F32))
    return pl.pallas_call(
        lambda *refs: _inproj_kernel(tiles_per_seq, *refs),
        grid=(ntile + 1,),
        in_specs=[
            pl.BlockSpec((TM, D_MODEL), cur),
            pl.BlockSpec((1, D_MODEL), const),
            pl.BlockSpec((D_MODEL, WA_COLS), const, pipeline_mode=pl.Buffered(1)),
            pl.BlockSpec((D_MODEL, WB_COLS), const, pipeline_mode=pl.Buffered(1)),
            pl.BlockSpec((D_MODEL, 2 * KV_HEADS * ATTN_HEAD_DIM), const, pipeline_mode=pl.Buffered(1)),
            pl.BlockSpec((D_MODEL, LANES), const, pipeline_mode=pl.Buffered(1)),
            pl.BlockSpec((D_CONV, CONV_CH), const),
            pl.BlockSpec((1, CONV_CH), const),
            pl.BlockSpec((1, LANES), const),
        ],
        out_specs=out_specs,
        out_shape=out_shape,
        scratch_shapes=[pltpu.VMEM((3, TM + 2 * HALO, CONV_CH), F32)],
        compiler_params=pltpu.CompilerParams(dimension_semantics=("arbitrary",),
                                             vmem_limit_bytes=VMEM_LIMIT),
        name="inproj",
    )(x2, norm_in_g, *weights, conv_w, conv_b, dt_bias)


def _split_terms(v, n):
    terms, r = [], v
    for _ in range(n):
        t = r.astype(BF16)
        terms.append(t)
        r = r - t.astype(F32)
    return terms


def _chunk_cumsum(tril, a):
    t0, t1, t2 = _split_terms(a, 3)
    return (jnp.dot(tril, t0, preferred_element_type=F32) + jnp.dot(tril, t1, preferred_element_type=F32)
            + jnp.dot(tril, t2, preferred_element_type=F32))


def _hi_lo(v):
    return jnp.concatenate(_split_terms(v, 2), axis=1)


def _fwd_lanes():
    lane = lax.broadcasted_iota(jnp.int32, (CHUNK, LANES), 1)
    return jnp.bitwise_and(lane, 2 * SSM_HEADS - 1) < SSM_HEADS


def _states_units(alog_ref, e_ref, tril_ref, xsf_ref, bf_ref, dtf_ref, xsb_ref, bb_ref, dtb_ref,
                  sf_ref, sb_ref, cf_ref, cb_ref):
    a_neg = -jnp.exp(alog_ref[...])
    tril = tril_ref[...]
    fwd_lane = _fwd_lanes()

    def prep(dt_ref):
        dt = dt_ref[...]
        a = dt * a_neg
        cs = _chunk_cumsum(tril, a)
        tot = cs[CHUNK - 1:CHUNK, :]
        w = dt * jnp.exp(jnp.where(fwd_lane, tot - cs, cs - a))
        return _hi_lo(jnp.concatenate([w, jnp.broadcast_to(jnp.exp(tot), (8, LANES))], axis=0))

    def chunk(c, d, cat, xs_ref, b_ref, out_ref, carry_ref):
        rows = slice(c * CHUNK, (c + 1) * CHUNK)
        wx = jnp.dot(cat, e_ref[c, d], preferred_element_type=F32)
        xd = (xs_ref[rows, :].astype(F32) * wx[0:CHUNK]).astype(BF16)
        dec = wx[CHUNK:CHUNK + 1, :]
        for g in range(SSM_GROUPS):
            gr = slice(g * D_STATE, (g + 1) * D_STATE)
            gc = slice(g * GROUP_W, (g + 1) * GROUP_W)
            st = lax.dot_general(b_ref[rows, gr], xd[:, gc], (((0,), (0,)), ((), ())),
                                 preferred_element_type=F32)
            prev = carry_ref[gr, :]
            out_ref[c, gr, :] = prev.astype(BF16)
            carry_ref[gr, :] = prev * dec[:, gc] + st

    cat_f = prep(dtf_ref)
    cat_b = prep(dtb_ref)
    units = []
    for c in range(Q):
        units.append(lambda c=c: chunk(c, 0, cat_f, xsf_ref, bf_ref, sf_ref, cf_ref))
        units.append(lambda c=c: chunk(Q - 1 - c, 1, cat_b, xsb_ref, bb_ref, sb_ref, cb_ref))
    return units


def _ssd_out_kernel(alog_ref, dskip_ref, gn_ref, selb_ref, tril_ref, xs_ref, bm_ref, cm_ref, dt_ref,
                    z_ref, sf_ref, sb_ref, ya_ref, x_ref, w_ref, fg_ref, o_ref, ys_ref, vs_ref):
    li = lax.broadcasted_iota(jnp.int32, (CHUNK, CHUNK), 0)
    si = lax.broadcasted_iota(jnp.int32, (CHUNK, CHUNK), 1)
    lower = li >= si
    eye = li == si
    left = lax.broadcasted_iota(jnp.int32, (CHUNK, LANES), 1) < SSM_HEAD_DIM

    dt = dt_ref[...]
    a = dt * (-LOG2E * jnp.exp(alog_ref[...]))
    cs = _chunk_cumsum(tril_ref[...], a)
    tot = cs[CHUNK - 1:CHUNK, :]
    colq = jnp.where(_fwd_lanes(), cs, tot - cs + a)
    rowq = (colq - jnp.log2(dt)).T
    dtr = dt.T.astype(BF16)
    catq = _hi_lo(colq)

    for c in range(Q):
        rows = slice(c * CHUNK, (c + 1) * CHUNK)
        for g in range(SSM_GROUPS):
            gr = slice(g * D_STATE, (g + 1) * D_STATE)
            cmg = cm_ref[rows, gr]
            cb = lax.dot_general(cmg, bm_ref[rows, gr], (((1,), (1,)), ((), ())),
                                 preferred_element_type=F32)
            cbb = cb.astype(BF16)
            cbeye = jnp.where(eye, cb, 0.0).astype(BF16)
            ssq = jnp.zeros((CHUNK, 1), F32)
            for pp in range(GROUP_W // LANES):
                hd0 = g * (SSM_HEADS // SSM_GROUPS) + pp * 2
                cols = slice(hd0 * SSM_HEAD_DIM, (hd0 + 2) * SSM_HEAD_DIM)
                if pp % 2 == 0:
                    c2 = slice(pp * LANES, (pp + 2) * LANES)
                    csf = jnp.dot(cmg, sf_ref[c, gr, c2], preferred_element_type=F32)
                    csb = jnp.dot(cmg, sb_ref[c, gr, c2], preferred_element_type=F32)
                pc = slice((pp % 2) * LANES, (pp % 2 + 1) * LANES)
                colb = jnp.dot(catq, selb_ref[c, :, hd0 * CHUNK:(hd0 + 2) * CHUNK],
                               preferred_element_type=F32)
                scores, colf_bc, colb_bc = [], [], []
                for hh in range(2):
                    h = c * 2 * SSM_HEADS + hd0 + hh
                    hb = h + SSM_HEADS
                    colf_bc.append(jnp.broadcast_to(colq[:, h:h + 1], (CHUNK, CHUNK)))
                    colb_bc.append(colb[:, hh * CHUNK:(hh + 1) * CHUNK])
                    arg = jnp.where(lower, colf_bc[hh] - rowq[h:h + 1, :], colb_bc[hh] - rowq[hb:hb + 1, :])
                    scores.append(cbb * jnp.exp2(arg).astype(BF16) + cbeye * dtr[hb:hb + 1, :])
                xt = xs_ref[rows, cols]
                zero = jnp.zeros_like(xt)
                rhs = jnp.concatenate([jnp.where(left, xt, zero), jnp.where(left, zero, xt)], axis=0)
                y_diag = jnp.dot(jnp.concatenate(scores, axis=1), rhs, preferred_element_type=F32)
                y_off = (csf[:, pc] * jnp.exp2(jnp.where(left, colf_bc[0], colf_bc[1]))
                         + csb[:, pc] * jnp.exp2(jnp.where(left, colb_bc[0], colb_bc[1])))
                vg = (y_diag + y_off + dskip_ref[:, cols] * xt.astype(F32)) * z_ref[rows, cols].astype(F32)
                vs_ref[rows, cols] = vg
                ssq = ssq + jnp.sum(vg * vg, axis=-1, keepdims=True)
            gc = slice(g * GROUP_W, (g + 1) * GROUP_W)
            ys_ref[rows, gc] = (vs_ref[rows, gc] * lax.rsqrt(ssq * (1.0 / GROUP_W) + EPS)
                                * gn_ref[:, gc]).astype(BF16)

        acc = (jnp.dot(ys_ref[rows, :], w_ref[0:D_SSM, :], preferred_element_type=F32)
               + jnp.dot(ya_ref[rows, :], w_ref[D_SSM:D_MIX, :], preferred_element_type=F32))
        hres = x_ref[rows, :] + acc
        ms = jnp.mean(hres * hres, axis=-1, keepdims=True)
        o_ref[rows, :] = hres * lax.rsqrt(ms + EPS) * fg_ref[...]


def _ssd_out(a_log, dskip_x, ssd_norm_g, selb, tril, xs, bm, cm, dt, z, sf, sb, y_attn, x2, w_out, final_norm_g):
    n = xs.shape[0]
    rows = Q * CHUNK
    row = lambda i: (i, 0)
    const = lambda i: (0, 0)
    return pl.pallas_call(
        _ssd_out_kernel,
        grid=(n // rows,),
        in_specs=[
            pl.BlockSpec((1, LANES), const),
            pl.BlockSpec((1, D_SSM), const),
            pl.BlockSpec((1, D_SSM), const),
            pl.BlockSpec((Q, 2 * LANES, SSM_HEADS * CHUNK), lambda i: (0, 0, 0), pipeline_mode=pl.Buffered(1)),
            pl.BlockSpec((CHUNK, CHUNK), const),
            pl.BlockSpec((rows, D_SSM), row),
            pl.BlockSpec((rows, SSM_GROUPS * D_STATE), row),
            pl.BlockSpec((rows, SSM_GROUPS * D_STATE), row),
            pl.BlockSpec((CHUNK, LANES), lambda i: (0, i)),
            pl.BlockSpec((rows, D_SSM), row),
            pl.BlockSpec((Q, SSM_GROUPS * D_STATE, GROUP_W), lambda i: (i, 0, 0)),
            pl.BlockSpec((Q, SSM_GROUPS * D_STATE, GROUP_W), lambda i: (i, 0, 0)),
            pl.BlockSpec((rows, D_ATTN), row),
            pl.BlockSpec((rows, D_MODEL), row),
            pl.BlockSpec((D_MIX, D_MODEL), const, pipeline_mode=pl.Buffered(1)),
            pl.BlockSpec((1, D_MODEL), const),
        ],
        out_specs=pl.BlockSpec((rows, D_MODEL), row),
        out_shape=jax.ShapeDtypeStruct((n, D_MODEL), F32),
        scratch_shapes=[pltpu.VMEM((rows, D_SSM), BF16), pltpu.VMEM((rows, D_SSM), F32)],
        compiler_params=pltpu.CompilerParams(dimension_semantics=("parallel",),
                                             vmem_limit_bytes=VMEM_LIMIT),
        name="ssd_out",
    )(a_log, dskip_x, ssd_norm_g, selb, tril, xs, bm, cm, dt, z, sf, sb, y_attn, x2, w_out, final_norm_g)


def _bucket_table_t():
    rel = jnp.arange(3 * BLK)[:, None] - BLK - jnp.arange(BLK)[None, :]
    half = NUM_BUCKETS // 2
    ret = (rel > 0).astype(jnp.int32) * half
    n = jnp.abs(rel)
    nf = jnp.maximum(n, 1).astype(F32)
    large = MAX_EXACT + (jnp.log(nf / MAX_EXACT) / math.log(MAX_DISTANCE / MAX_EXACT)
                         * (half - MAX_EXACT)).astype(jnp.int32)
    large = jnp.minimum(large, half - 1)
    return (ret + jnp.where(n < MAX_EXACT, n, large)).astype(jnp.int32)


def _attn_kernel(nblk, nstep, rb_ref, bucket_ref, sink_ref, gn_ref, q_ref, kp_ref, kc_ref, kn_ref,
                 vp_ref, vc_ref, vn_ref, ga_ref,
                 alog_ref, e_ref, tril_ref, xsf_ref, bf_ref, dtf_ref, xsb_ref, bb_ref, dtb_ref,
                 y_ref, sf_ref, sb_ref,
                 tab_ref, pe_ref, mk_ref, ot_ref, vs_ref, cf_ref, cb_ref):
    g = pl.program_id(0)

    @pl.when((g == 0) | (lax.rem(g + (nblk // ATTN_NB) - 2, nblk // ATTN_NB) == 0))
    def _():
        cf_ref[...] = jnp.zeros_like(cf_ref)
        cb_ref[...] = jnp.zeros_like(cb_ref)

    @pl.when(g == 0)
    def _init():
        pe_ref[...] = jnp.zeros_like(pe_ref)
        mk_ref[...] = jnp.zeros_like(mk_ref)
        ot_ref[...] = jnp.zeros_like(ot_ref)
        bk = bucket_ref[...]
        t = lax.broadcasted_iota(jnp.int32, (3 * BLK, BLK), 0)
        qq = lax.broadcasted_iota(jnp.int32, (3 * BLK, BLK), 1)
        in_window = jnp.abs(t - BLK - qq) <= WINDOW

        def per_head(h, carry):
            acc = jnp.zeros((3 * BLK, BLK), F32)
            for b in range(NUM_BUCKETS):
                acc = jnp.where(bk == b, rb_ref[b, h], acc)
            tab = jnp.where(in_window, acc * LOG2E, NEG_INF)
            tab_ref[0, h] = tab
            tab_ref[1, h] = jnp.where(t >= BLK, tab, NEG_INF)
            tab_ref[2, h] = jnp.where(t < 2 * BLK, tab, NEG_INF)
            return carry

        lax.fori_loop(0, ATTN_HEADS, per_head, 0)

    left = lax.broadcasted_iota(jnp.int32, (BLK, LANES), 1) < ATTN_HEAD_DIM
    left3 = lax.broadcasted_iota(jnp.int32, (3 * BLK, LANES), 1) < ATTN_HEAD_DIM
    s2 = sink_ref[...] * LOG2E

    def run(even, odd):
        first_blk = jnp.minimum(g, nstep - 1) * ATTN_NB
        kall = jnp.concatenate([kp_ref[...], kc_ref[...], kn_ref[...]], axis=0)
        vall = jnp.concatenate([vp_ref[...], vc_ref[...], vn_ref[...]], axis=0)
        one = jnp.ones((3 * BLK, LANES), BF16)

        def stage_a(j, p, var):
            kd = kall[j * BLK:(j + 3) * BLK, (p // 2) * LANES:(p // 2 + 1) * LANES]
            qt = q_ref[j * BLK:(j + 1) * BLK, p * LANES:(p + 1) * LANES]
            zero = jnp.zeros_like(qt)
            rhs_t = jnp.concatenate([jnp.where(left, qt, zero), jnp.where(left, zero, qt)], axis=0)
            lg = lax.dot_general(kd, rhs_t, (((1,), (1,)), ((), ())), preferred_element_type=F32)
            lg = lg + jnp.concatenate([tab_ref[var, 2 * p], tab_ref[var, 2 * p + 1]], axis=1)
            for kb in range(3):
                blk = lg[kb * BLK:(kb + 1) * BLK]
                mk = jnp.max(blk, axis=0, keepdims=True)
                pe_ref[even, j, p, kb * BLK:(kb + 1) * BLK, :] = jnp.exp2(blk - mk).astype(BF16)
                mk_ref[even, j, p, kb:kb + 1, :] = mk

        def stage_b(j, p):
            kv = p // 2
            vt = vall[j * BLK:(j + 3) * BLK, (kv // 2) * LANES:(kv // 2 + 1) * LANES]
            vmod = jnp.where(left3, vt, one) if kv % 2 == 0 else jnp.where(left3, one, vt)
            r0 = (kv % 2) * ATTN_HEAD_DIM
            d0 = ATTN_HEAD_DIM - r0
            mk = mk_ref[odd, j, p]
            sp = s2[p:p + 1, :]
            m = jnp.maximum(jnp.max(mk[0:3], axis=0, keepdims=True), sp)
            resc = jnp.exp2(mk[0:3] - m).astype(BF16)
            pes = jnp.concatenate([pe_ref[odd, j, p, kb * BLK:(kb + 1) * BLK, :] * resc[kb:kb + 1]
                                   for kb in range(3)], axis=0)
            o2 = lax.dot_general(vmod, pes, (((0,), (0,)), ((), ())), preferred_element_type=F32)
            o = o2[r0:r0 + ATTN_HEAD_DIM, :] / (o2[d0:d0 + 1, :] + jnp.exp2(sp - m))
            ot_ref[odd, j, 2 * p * ATTN_HEAD_DIM:(2 * p + 1) * ATTN_HEAD_DIM, :] = o[:, 0:BLK]
            ot_ref[odd, j, (2 * p + 1) * ATTN_HEAD_DIM:(2 * p + 2) * ATTN_HEAD_DIM, :] = o[:, BLK:2 * BLK]

        def stage_c(j, p, ssq):
            rows = slice(j * BLK, (j + 1) * BLK)
            cols = slice(p * LANES, (p + 1) * LANES)
            v = ot_ref[even, j, p * LANES:(p + 1) * LANES, :].T * ga_ref[rows, cols].astype(F32)
            vs_ref[rows, cols] = v
            return ssq + jnp.sum(v * v, axis=-1, keepdims=True)

        scan_units = _states_units(alog_ref, e_ref, tril_ref, xsf_ref, bf_ref, dtf_ref, xsb_ref, bb_ref,
                                   dtb_ref, sf_ref, sb_ref, cf_ref, cb_ref)
        per_block = len(scan_units) // ATTN_NB
        npair = ATTN_HEADS // 2

        for j in range(ATTN_NB):
            pos = lax.rem(first_blk + j, nblk)
            var = jnp.where(pos == 0, 1, jnp.where(pos == nblk - 1, 2, 0))
            rows = slice(j * BLK, (j + 1) * BLK)
            ssq = jnp.zeros((BLK, 1), F32)
            for p in range(ATTN_HEADS // 2):
                stage_a(j, p, var)
                stage_b(j, p)
                ssq = stage_c(j, p, ssq)
                if (p + 1) % (npair // per_block) == 0:
                    scan_units.pop(0)()
            y_ref[rows, :] = (vs_ref[rows, :] * lax.rsqrt(ssq * (1.0 / D_ATTN) + EPS) * gn_ref[...]).astype(BF16)

    @pl.when(lax.rem(g, 2) == 0)
    def _():
        run(0, 1)

    @pl.when(lax.rem(g, 2) == 1)
    def _():
        run(1, 0)


def _attn(rel_bias, bucket_t, sink_x, attn_norm_g, q, kd, v, ga, a_log, e, tril, xs, bm, dt, batch, seq):
    n = q.shape[0]
    nblk = seq // BLK
    assert nblk % ATTN_NB == 0 and ATTN_NB * BLK == Q * CHUNK
    nstep = batch * nblk // ATTN_NB
    per_seq = nblk // ATTN_NB
    nchunk = batch * seq // CHUNK

    def mirrored(lag):
        def index(g):
            s = step(lag)(g)
            return s - lax.rem(s, per_seq) + per_seq - 1 - lax.rem(s, per_seq)
        return index

    def step(lag):
        return lambda g: jnp.clip(g - lag, 0, nstep - 1)

    def cur(lag):
        return lambda g: (step(lag)(g), 0)

    def prev(lag):
        def index(g):
            b = step(lag)(g) * ATTN_NB
            return (b - jnp.where(lax.rem(b, nblk) > 0, 1, 0), 0)
        return index

    def nxt(lag):
        def index(g):
            b = step(lag)(g) * ATTN_NB + ATTN_NB - 1
            return (b + jnp.where(lax.rem(b, nblk) < nblk - 1, 1, 0), 0)
        return index

    const = lambda g: (0, 0)
    kw = 2 * KV_HEADS * ATTN_HEAD_DIM
    vw = KV_HEADS * ATTN_HEAD_DIM
    npair = ATTN_HEADS // 2
    rows = ATTN_NB * BLK
    st_shape = jax.ShapeDtypeStruct((nchunk, SSM_GROUPS * D_STATE, GROUP_W), BF16)
    st_block = (Q, SSM_GROUPS * D_STATE, GROUP_W)
    return pl.pallas_call(
        lambda *refs: _attn_kernel(nblk, nstep, *refs),
        grid=(nstep + 2,),
        in_specs=[
            pl.BlockSpec(memory_space=pltpu.SMEM),
            pl.BlockSpec((3 * BLK, BLK), const),
            pl.BlockSpec((npair, 2 * BLK), const),
            pl.BlockSpec((1, D_ATTN), const),
            pl.BlockSpec((rows, D_ATTN), cur(0)),
            pl.BlockSpec((BLK, kw), prev(0)), pl.BlockSpec((rows, kw), cur(0)), pl.BlockSpec((BLK, kw), nxt(0)),
            pl.BlockSpec((BLK, vw), prev(1)), pl.BlockSpec((rows, vw), cur(1)), pl.BlockSpec((BLK, vw), nxt(1)),
            pl.BlockSpec((rows, D_ATTN), cur(2)),
            pl.BlockSpec((1, LANES), const),
            pl.BlockSpec((Q, 2, 2 * LANES, D_SSM), lambda g: (0, 0, 0, 0), pipeline_mode=pl.Buffered(1)),
            pl.BlockSpec((CHUNK, CHUNK), const),
            pl.BlockSpec((rows, D_SSM), cur(2)),
            pl.BlockSpec((rows, SSM_GROUPS * D_STATE), cur(2)),
            pl.BlockSpec((CHUNK, LANES), lambda g: (0, step(2)(g))),
            pl.BlockSpec((rows, D_SSM), lambda g: (mirrored(2)(g), 0)),
            pl.BlockSpec((rows, SSM_GROUPS * D_STATE), lambda g: (mirrored(2)(g), 0)),
            pl.BlockSpec((CHUNK, LANES), lambda g: (0, mirrored(2)(g))),
        ],
        out_specs=[pl.BlockSpec((rows, D_ATTN), cur(2)),
                   pl.BlockSpec(st_block, lambda g: (step(2)(g), 0, 0)),
                   pl.BlockSpec(st_block, lambda g: (mirrored(2)(g), 0, 0))],
        out_shape=[jax.ShapeDtypeStruct((n, D_ATTN), BF16), st_shape, st_shape],
        scratch_shapes=[pltpu.VMEM((3, ATTN_HEADS, 3 * BLK, BLK), F32),
                        pltpu.VMEM((2, ATTN_NB, npair, 3 * BLK, 2 * BLK), BF16),
                        pltpu.VMEM((2, ATTN_NB, npair, 8, 2 * BLK), F32),
                        pltpu.VMEM((2, ATTN_NB, D_ATTN, BLK), F32),
                        pltpu.VMEM((rows, D_ATTN), F32),
                        pltpu.VMEM((SSM_GROUPS * D_STATE, GROUP_W), F32),
                        pltpu.VMEM((SSM_GROUPS * D_STATE, GROUP_W), F32)],
        compiler_params=pltpu.CompilerParams(dimension_semantics=("arbitrary",),
                                             vmem_limit_bytes=VMEM_LIMIT),
        name="attn_scan",
    )(rel_bias, bucket_t, sink_x, attn_norm_g, q, kd, kd, kd, v, v, v, ga,
      a_log, e, tril, xs, bm, dt, xs, bm, dt)


def _arrange_w_in(w_in):
    wa = w_in[:, 0:WA_COLS].astype(BF16)
    wdt = jnp.pad(w_in[:, WA_COLS:WB_START], ((0, 0), (0, LANES - 2 * SSM_HEADS))).astype(BF16)
    wb = w_in[:, WB_START:WB_START + WB_COLS].astype(BF16)
    wk = wb[:, D_ATTN:OFF_V].reshape(D_MODEL, KV_HEADS, 1, ATTN_HEAD_DIM)
    wkd = jnp.broadcast_to(wk, (D_MODEL, KV_HEADS, 2, ATTN_HEAD_DIM)).reshape(D_MODEL, -1)
    return wa, wb, wkd, wdt


def _expand_matrices():
    e = np.zeros((Q, 2, 2 * LANES, D_SSM), np.float32)
    for c in range(Q):
        for d in range(2):
            for h in range(SSM_HEADS):
                r = (c * 2 + d) * SSM_HEADS + h
                e[c, d, r, h * SSM_HEAD_DIM:(h + 1) * SSM_HEAD_DIM] = 1.0
                e[c, d, LANES + r, h * SSM_HEAD_DIM:(h + 1) * SSM_HEAD_DIM] = 1.0
    return e


def _bcast_matrices():
    m = np.zeros((Q, 2 * LANES, SSM_HEADS * CHUNK), np.float32)
    for c in range(Q):
        for h in range(SSM_HEADS):
            r = (c * 2 + 1) * SSM_HEADS + h
            m[c, r, h * CHUNK:(h + 1) * CHUNK] = 1.0
            m[c, LANES + r, h * CHUNK:(h + 1) * CHUNK] = 1.0
    return m


def kernel(x, norm_in_g, w_in, conv_w, conv_b, dt_bias, a_log, d_skip, ssd_norm_g, rel_bias, sink,
           attn_norm_g, w_out, final_norm_g):
    batch, seq, _ = x.shape
    assert w_out.shape[0] == 1 and seq % (Q * CHUNK) == 0 and seq % TM == 0
    x2 = x.reshape(batch * seq, D_MODEL)

    weights = _arrange_w_in(w_in)
    e = jnp.asarray(_expand_matrices(), BF16)
    tril = jnp.asarray(np.tril(np.ones((CHUNK, CHUNK), np.float32)), BF16)
    a_log2 = jnp.tile(a_log.reshape(1, 2 * SSM_HEADS).astype(F32), (1, Q))
    dt_bias_x = jnp.pad(dt_bias.reshape(1, 2 * SSM_HEADS).astype(F32), ((0, 0), (0, LANES - 2 * SSM_HEADS)))
    dskip_x = jnp.repeat(d_skip.astype(F32), SSM_HEAD_DIM).reshape(1, D_SSM)
    sink_x = jnp.repeat(sink.astype(F32), BLK).reshape(ATTN_HEADS // 2, 2 * BLK)

    z, xs, bm, cm, dt, q, kd, v, ga = _inproj(
        x2, norm_in_g.reshape(1, D_MODEL), weights, conv_w, conv_b.reshape(1, CONV_CH),
        dt_bias_x, seq)
    y_attn, sf, sb = _attn(rel_bias.astype(F32), _bucket_table_t(), sink_x, attn_norm_g.reshape(1, D_ATTN),
                           q, kd, v, ga, a_log2, e, tril, xs, bm, dt, batch, seq)
    out = _ssd_out(a_log2, dskip_x, ssd_norm_g.reshape(1, D_SSM), jnp.asarray(_bcast_matrices(), BF16),
                   tril, xs, bm, cm, dt, z, sf, sb, y_attn, x2, w_out[0].astype(BF16),
                   final_norm_g.reshape(1, D_MODEL))
    return out.reshape(batch, seq, D_MODEL)
```

```python
import math

import numpy as np
import jax
import jax.numpy as jnp
from jax import lax
from jax.experimental import pallas as pl
from jax.experimental.pallas import tpu as pltpu

F32 = jnp.float32
BF16 = jnp.bfloat16

D_MODEL = 1024
D_SSM = 1024
D_ATTN = 1024
D_MIX = D_SSM + D_ATTN
SSM_HEAD_DIM = 64
SSM_HEADS = D_SSM // SSM_HEAD_DIM
SSM_GROUPS = 2
GROUP_W = D_SSM // SSM_GROUPS
D_STATE = 128
D_CONV = 5
CHUNK = 128
CONV_CH = D_SSM + 2 * SSM_GROUPS * D_STATE
ATTN_HEAD_DIM = 64
ATTN_HEADS = D_ATTN // ATTN_HEAD_DIM
KV_HEADS = 4
WINDOW = 128
BLK = 128
NUM_BUCKETS = 32
MAX_DISTANCE = 128
MAX_EXACT = 8
EPS = 1e-6
NEG_INF = -1e30
SCALE = ATTN_HEAD_DIM ** -0.5
LOG2E = math.log2(math.e)

LANES = 128
HALO = 8
TM = 512
CONV_ROWS = 64
ATTN_NB = 4
Q = 4
VMEM_LIMIT = 56 * 1024 * 1024

WA_COLS = D_SSM + CONV_CH
OFF_Z, OFF_XBC = 0, D_SSM
WB_START = WA_COLS + 2 * SSM_HEADS
OFF_Q = 0
OFF_V = D_ATTN + KV_HEADS * ATTN_HEAD_DIM
OFF_GA = OFF_V + KV_HEADS * ATTN_HEAD_DIM
WB_COLS = OFF_GA + D_ATTN
assert TM == Q * CHUNK and Q * 2 * SSM_HEADS == LANES


def _silu(v):
    return v * jax.nn.sigmoid(v)


def _softplus(v):
    return jnp.maximum(v, 0.0) + jnp.log1p(jnp.exp(-jnp.abs(v)))


def _inproj_kernel(tiles_per_seq, x_ref, g_ref, wa_ref, wb_ref, wkd_ref, wdt_ref, cw_ref, cb_ref, dtb_ref,
                   z_ref, xs_ref, bm_ref, cm_ref, dt_ref, q_ref, kd_ref, v_ref, ga_ref, pad_ref):
    i = pl.program_id(0)

    @pl.when(i == 0)
    def _():
        pad_ref[...] = jnp.zeros_like(pad_ref)

    j = lax.rem(i, tiles_per_seq)
    g = g_ref[...]

    def proj(hv, w_ref, off, n):
        return jnp.dot(hv, w_ref[:, off:off + n], preferred_element_type=F32)

    xv = x_ref[...]
    ms = jnp.mean(xv * xv, axis=-1, keepdims=True)
    h = (xv * lax.rsqrt(ms + EPS) * g).astype(BF16)

    s_cur = lax.rem(i, 3)
    s_prev = lax.rem(i + 2, 3)
    s_next = lax.rem(i + 1, 3)
    xbc = proj(h, wa_ref, OFF_XBC, CONV_CH)
    pad_ref[s_cur, HALO:HALO + TM, :] = xbc
    pad_ref[s_prev, HALO + TM:2 * HALO + TM, :] = xbc[0:HALO] * jnp.where(j > 0, 1.0, 0.0)
    pad_ref[s_next, 0:HALO, :] = xbc[TM - HALO:TM] * jnp.where(j < tiles_per_seq - 1, 1.0, 0.0)

    z_ref[...] = _silu(proj(h, wa_ref, OFF_Z, D_SSM)).astype(BF16)
    q_ref[...] = (proj(h, wb_ref, OFF_Q, D_ATTN) * (SCALE * LOG2E)).astype(BF16)
    kd_ref[...] = proj(h, wkd_ref, 0, 2 * KV_HEADS * ATTN_HEAD_DIM).astype(BF16)
    v_ref[...] = proj(h, wb_ref, OFF_V, KV_HEADS * ATTN_HEAD_DIM).astype(BF16)
    ga_ref[...] = _silu(proj(h, wb_ref, OFF_GA, D_ATTN)).astype(BF16)
    dt_lane = lax.broadcasted_iota(jnp.int32, (TM, LANES), 1) < 2 * SSM_HEADS
    dt = jnp.where(dt_lane, _softplus(proj(h, wdt_ref, 0, LANES) + dtb_ref[...]), 0.0)
    dtc = dt[0:CHUNK]
    for c in range(1, Q):
        dtc = dtc + pltpu.roll(dt[c * CHUNK:(c + 1) * CHUNK], 2 * SSM_HEADS * c, axis=1)
    dt_ref[...] = dtc

    base = HALO - (D_CONV - 1) // 2
    for cc in range(CONV_CH // LANES):
        sl = slice(cc * LANES, (cc + 1) * LANES)
        if cc < D_SSM // LANES:
            dst, o = xs_ref, cc * LANES
        elif cc < (D_SSM + SSM_GROUPS * D_STATE) // LANES:
            dst, o = bm_ref, cc * LANES - D_SSM
        else:
            dst, o = cm_ref, cc * LANES - D_SSM - SSM_GROUPS * D_STATE
        for r0 in range(0, TM, CONV_ROWS):
            acc = cb_ref[:, sl] + cw_ref[0:1, sl] * pad_ref[s_prev, base + r0:base + r0 + CONV_ROWS, sl]
            for k in range(1, D_CONV):
                acc = acc + cw_ref[k:k + 1, sl] * pad_ref[s_prev, base + r0 + k:base + r0 + k + CONV_ROWS, sl]
            dst[r0:r0 + CONV_ROWS, o:o + LANES] = _silu(acc).astype(BF16)


def _inproj(x2, norm_in_g, weights, conv_w, conv_b, dt_bias, seq):
    n = x2.shape[0]
    tiles_per_seq = seq // TM
    ntile = n // TM
    cur = lambda i: (jnp.minimum(i, ntile - 1), 0)
    lag = lambda i: (jnp.maximum(i - 1, 0), 0)
    const = lambda i: (0, 0)
    out_cols = (D_SSM, D_SSM, SSM_GROUPS * D_STATE, SSM_GROUPS * D_STATE,
                D_ATTN, 2 * KV_HEADS * ATTN_HEAD_DIM, KV_HEADS * ATTN_HEAD_DIM, D_ATTN)
    out_maps = (cur, lag, lag, lag, cur, cur, cur, cur)
    out_specs = [pl.BlockSpec((TM, c), m) for c, m in zip(out_cols, out_maps)]
    out_shape = [jax.ShapeDtypeStruct((n, c), BF16) for c in out_cols]
    out_specs.insert(4, pl.BlockSpec((CHUNK, LANES), lambda i: (0, jnp.minimum(i, ntile - 1))))
    out_shape.insert(4, jax.ShapeDtypeStruct((CHUNK, ntile * LANES), F32))
    return pl.pallas_call(
        lambda *refs: _inproj_kernel(tiles_per_seq, *refs),
        grid=(ntile + 1,),
        in_specs=[
            pl.BlockSpec((TM, D_MODEL), cur),
            pl.BlockSpec((1, D_MODEL), const),
            pl.BlockSpec((D_MODEL, WA_COLS), const, pipeline_mode=pl.Buffered(1)),
            pl.BlockSpec((D_MODEL, WB_COLS), const, pipeline_mode=pl.Buffered(1)),
            pl.BlockSpec((D_MODEL, 2 * KV_HEADS * ATTN_HEAD_DIM), const, pipeline_mode=pl.Buffered(1)),
            pl.BlockSpec((D_MODEL, LANES), const, pipeline_mode=pl.Buffered(1)),
            pl.BlockSpec((D_CONV, CONV_CH), const),
            pl.BlockSpec((1, CONV_CH), const),
            pl.BlockSpec((1, LANES), const),
        ],
        out_specs=out_specs,
        out_shape=out_shape,
        scratch_shapes=[pltpu.VMEM((3, TM + 2 * HALO, CONV_CH), F32)],
        compiler_params=pltpu.CompilerParams(dimension_semantics=("arbitrary",),
                                             vmem_limit_bytes=VMEM_LIMIT),
        name="inproj",
    )(x2, norm_in_g, *weights, conv_w, conv_b, dt_bias)


def _split_terms(v, n):
    terms, r = [], v
    for _ in range(n):
        t = r.astype(BF16)
        terms.append(t)
        r = r - t.astype(F32)
    return terms


def _chunk_cumsum(tril, a):
    t0, t1, t2 = _split_terms(a, 3)
    return (jnp.dot(tril, t0, preferred_element_type=F32) + jnp.dot(tril, t1, preferred_element_type=F32)
            + jnp.dot(tril, t2, preferred_element_type=F32))


def _hi_lo(v):
    return jnp.concatenate(_split_terms(v, 2), axis=1)


def _fwd_lanes():
    lane = lax.broadcasted_iota(jnp.int32, (CHUNK, LANES), 1)
    return jnp.bitwise_and(lane, 2 * SSM_HEADS - 1) < SSM_HEADS


def _states_units(alog_ref, e_ref, tril_ref, xsf_ref, bf_ref, dtf_ref, xsb_ref, bb_ref, dtb_ref,
                  sf_ref, sb_ref, cf_ref, cb_ref):
    a_neg = -jnp.exp(alog_ref[...])
    tril = tril_ref[...]
    fwd_lane = _fwd_lanes()

    def prep(dt_ref):
        dt = dt_ref[...]
        a = dt * a_neg
        cs = _chunk_cumsum(tril, a)
        tot = cs[CHUNK - 1:CHUNK, :]
        w = dt * jnp.exp(jnp.where(fwd_lane, tot - cs, cs - a))
        return _hi_lo(jnp.concatenate([w, jnp.broadcast_to(jnp.exp(tot), (8, LANES))], axis=0))

    def chunk(c, d, cat, xs_ref, b_ref, out_ref, carry_ref):
        rows = slice(c * CHUNK, (c + 1) * CHUNK)
        wx = jnp.dot(cat, e_ref[c, d], preferred_element_type=F32)
        xd = (xs_ref[rows, :].astype(F32) * wx[0:CHUNK]).astype(BF16)
        dec = wx[CHUNK:CHUNK + 1, :]
        for g in range(SSM_GROUPS):
            gr = slice(g * D_STATE, (g + 1) * D_STATE)
            gc = slice(g * GROUP_W, (g + 1) * GROUP_W)
            st = lax.dot_general(b_ref[rows, gr], xd[:, gc], (((0,), (0,)), ((), ())),
                                 preferred_element_type=F32)
            prev = carry_ref[gr, :]
            out_ref[c, gr, :] = prev.astype(BF16)
            carry_ref[gr, :] = prev * dec[:, gc] + st

    cat_f = prep(dtf_ref)
    cat_b = prep(dtb_ref)
    units = []
    for c in range(Q):
        units.append(lambda c=c: chunk(c, 0, cat_f, xsf_ref, bf_ref, sf_ref, cf_ref))
        units.append(lambda c=c: chunk(Q - 1 - c, 1, cat_b, xsb_ref, bb_ref, sb_ref, cb_ref))
    return units


def _ssd_out_kernel(alog_ref, dskip_ref, gn_ref, selb_ref, tril_ref, xs_ref, bm_ref, cm_ref, dt_ref,
                    z_ref, sf_ref, sb_ref, ya_ref, x_ref, w_ref, fg_ref, o_ref, ys_ref, vs_ref):
    li = lax.broadcasted_iota(jnp.int32, (CHUNK, CHUNK), 0)
    si = lax.broadcasted_iota(jnp.int32, (CHUNK, CHUNK), 1)
    lower = li >= si
    eye = li == si
    left = lax.broadcasted_iota(jnp.int32, (CHUNK, LANES), 1) < SSM_HEAD_DIM

    dt = dt_ref[...]
    a = dt * (-LOG2E * jnp.exp(alog_ref[...]))
    colq = jnp.where(_fwd_lanes(), _chunk_cumsum(tril_ref[0], a), _chunk_cumsum(tril_ref[1], a))
    rowq = (colq - jnp.log2(dt)).T
    dtr = dt.T
    catq = _hi_lo(colq)

    for c in range(Q):
        rows = slice(c * CHUNK, (c + 1) * CHUNK)
        for g in range(SSM_GROUPS):
            gr = slice(g * D_STATE, (g + 1) * D_STATE)
            cmg = cm_ref[rows, gr]
            cb = lax.dot_general(cmg, bm_ref[rows, gr], (((1,), (1,)), ((), ())),
                                 preferred_element_type=F32)
            cboff = jnp.where(eye, 0.0, cb).astype(BF16)
            cbeye = jnp.where(eye, cb, 0.0).astype(BF16)
            ssq = jnp.zeros((CHUNK, 1), F32)
            for pp in range(GROUP_W // LANES):
                hd0 = g * (SSM_HEADS // SSM_GROUPS) + pp * 2
                cols = slice(hd0 * SSM_HEAD_DIM, (hd0 + 2) * SSM_HEAD_DIM)
                if pp % 2 == 0:
                    c2 = slice(pp * LANES, (pp + 2) * LANES)
                    csf = jnp.dot(cmg, sf_ref[c, gr, c2], preferred_element_type=F32)
                    csb = jnp.dot(cmg, sb_ref[c, gr, c2], preferred_element_type=F32)
                pc = slice((pp % 2) * LANES, (pp % 2 + 1) * LANES)
                colb = jnp.dot(catq, selb_ref[c, :, hd0 * CHUNK:(hd0 + 2) * CHUNK],
                               preferred_element_type=F32)
                scores, colf_bc, colb_bc = [], [], []
                for hh in range(2):
                    h = c * 2 * SSM_HEADS + hd0 + hh
                    hb = h + SSM_HEADS
                    colf_bc.append(jnp.broadcast_to(colq[:, h:h + 1], (CHUNK, CHUNK)))
                    colb_bc.append(colb[:, hh * CHUNK:(hh + 1) * CHUNK])
                    arg = jnp.where(lower, colf_bc[hh] - rowq[h:h + 1, :], colb_bc[hh] - rowq[hb:hb + 1, :])
                    dsum = (dtr[h:h + 1, :] + dtr[hb:hb + 1, :]).astype(BF16)
                    scores.append(cboff * jnp.exp2(arg).astype(BF16) + cbeye * dsum)
                xt = xs_ref[rows, cols]
                zero = jnp.zeros_like(xt)
                rhs = jnp.concatenate([jnp.where(left, xt, zero), jnp.where(left, zero, xt)], axis=0)
                y_diag = jnp.dot(jnp.concatenate(scores, axis=1), rhs, preferred_element_type=F32)
                y_off = (csf[:, pc] * jnp.exp2(jnp.where(left, colf_bc[0], colf_bc[1]))
                         + csb[:, pc] * jnp.exp2(jnp.where(left, colb_bc[0], colb_bc[1])))
                vg = (y_diag + y_off + dskip_ref[:, cols] * xt.astype(F32)) * z_ref[rows, cols].astype(F32)
                vs_ref[rows, cols] = vg
                ssq = ssq + jnp.sum(vg * vg, axis=-1, keepdims=True)
            gc = slice(g * GROUP_W, (g + 1) * GROUP_W)
            ys_ref[rows, gc] = (vs_ref[rows, gc] * lax.rsqrt(ssq * (1.0 / GROUP_W) + EPS)
                                * gn_ref[:, gc]).astype(BF16)

        acc = (jnp.dot(ys_ref[rows, :], w_ref[0:D_SSM, :], preferred_element_type=F32)
               + jnp.dot(ya_ref[rows, :], w_ref[D_SSM:D_MIX, :], preferred_element_type=F32))
        hres = x_ref[rows, :] + acc
        ms = jnp.mean(hres * hres, axis=-1, keepdims=True)
        o_ref[rows, :] = hres * lax.rsqrt(ms + EPS) * fg_ref[...]


def _ssd_out(a_log, dskip_x, ssd_norm_g, selb, tril, xs, bm, cm, dt, z, sf, sb, y_attn, x2, w_out, final_norm_g):
    n = xs.shape[0]
    rows = Q * CHUNK
    row = lambda i: (i, 0)
    const = lambda i: (0, 0)
    return pl.pallas_call(
        _ssd_out_kernel,
        grid=(n // rows,),
        in_specs=[
            pl.BlockSpec((1, LANES), const),
            pl.BlockSpec((1, D_SSM), const),
            pl.BlockSpec((1, D_SSM), const),
            pl.BlockSpec((Q, 2 * LANES, SSM_HEADS * CHUNK), lambda i: (0, 0, 0), pipeline_mode=pl.Buffered(1)),
            pl.BlockSpec((2, CHUNK, CHUNK), lambda i: (0, 0, 0)),
            pl.BlockSpec((rows, D_SSM), row),
            pl.BlockSpec((rows, SSM_GROUPS * D_STATE), row),
            pl.BlockSpec((rows, SSM_GROUPS * D_STATE), row),
            pl.BlockSpec((CHUNK, LANES), lambda i: (0, i)),
            pl.BlockSpec((rows, D_SSM), row),
            pl.BlockSpec((Q, SSM_GROUPS * D_STATE, GROUP_W), lambda i: (i, 0, 0)),
            pl.BlockSpec((Q, SSM_GROUPS * D_STATE, GROUP_W), lambda i: (i, 0, 0)),
            pl.BlockSpec((rows, D_ATTN), row),
            pl.BlockSpec((rows, D_MODEL), row),
            pl.BlockSpec((D_MIX, D_MODEL), const, pipeline_mode=pl.Buffered(1)),
            pl.BlockSpec((1, D_MODEL), const),
        ],
        out_specs=pl.BlockSpec((rows, D_MODEL), row),
        out_shape=jax.ShapeDtypeStruct((n, D_MODEL), F32),
        scratch_shapes=[pltpu.VMEM((rows, D_SSM), BF16), pltpu.VMEM((rows, D_SSM), F32)],
        compiler_params=pltpu.CompilerParams(dimension_semantics=("parallel",),
                                             vmem_limit_bytes=VMEM_LIMIT),
        name="ssd_out",
    )(a_log, dskip_x, ssd_norm_g, selb, tril, xs, bm, cm, dt, z, sf, sb, y_attn, x2, w_out, final_norm_g)


def _bucket_table_t():
    rel = jnp.arange(3 * BLK)[:, None] - BLK - jnp.arange(BLK)[None, :]
    half = NUM_BUCKETS // 2
    ret = (rel > 0).astype(jnp.int32) * half
    n = jnp.abs(rel)
    nf = jnp.maximum(n, 1).astype(F32)
    large = MAX_EXACT + (jnp.log(nf / MAX_EXACT) / math.log(MAX_DISTANCE / MAX_EXACT)
                         * (half - MAX_EXACT)).astype(jnp.int32)
    large = jnp.minimum(large, half - 1)
    return (ret + jnp.where(n < MAX_EXACT, n, large)).astype(jnp.int32)


def _attn_kernel(nblk, nstep, rb_ref, bucket_ref, sink_ref, gn_ref, q_ref, kp_ref, kc_ref, kn_ref,
                 vp_ref, vc_ref, vn_ref, ga_ref,
                 alog_ref, e_ref, tril_ref, xsf_ref, bf_ref, dtf_ref, xsb_ref, bb_ref, dtb_ref,
                 y_ref, sf_ref, sb_ref,
                 tab_ref, pe_ref, mk_ref, ot_ref, vs_ref, cf_ref, cb_ref):
    g = pl.program_id(0)

    @pl.when((g == 0) | (lax.rem(g + (nblk // ATTN_NB) - 2, nblk // ATTN_NB) == 0))
    def _():
        cf_ref[...] = jnp.zeros_like(cf_ref)
        cb_ref[...] = jnp.zeros_like(cb_ref)

    @pl.when(g == 0)
    def _init():
        pe_ref[...] = jnp.zeros_like(pe_ref)
        mk_ref[...] = jnp.zeros_like(mk_ref)
        ot_ref[...] = jnp.zeros_like(ot_ref)
        bk = bucket_ref[...]
        t = lax.broadcasted_iota(jnp.int32, (3 * BLK, BLK), 0)
        qq = lax.broadcasted_iota(jnp.int32, (3 * BLK, BLK), 1)
        in_window = jnp.abs(t - BLK - qq) <= WINDOW

        def per_head(h, carry):
            acc = jnp.zeros((3 * BLK, BLK), F32)
            for b in range(NUM_BUCKETS):
                acc = jnp.where(bk == b, rb_ref[b, h], acc)
            tab = jnp.where(in_window, acc * LOG2E, NEG_INF)
            tab_ref[0, h] = tab
            tab_ref[1, h] = jnp.where(t >= BLK, tab, NEG_INF)
            tab_ref[2, h] = jnp.where(t < 2 * BLK, tab, NEG_INF)
            return carry

        lax.fori_loop(0, ATTN_HEADS, per_head, 0)

    left = lax.broadcasted_iota(jnp.int32, (BLK, LANES), 1) < ATTN_HEAD_DIM
    left3 = lax.broadcasted_iota(jnp.int32, (3 * BLK, LANES), 1) < ATTN_HEAD_DIM
    s2 = sink_ref[...] * LOG2E

    def run(even, odd):
        first_blk = jnp.minimum(g, nstep - 1) * ATTN_NB
        kall = jnp.concatenate([kp_ref[...], kc_ref[...], kn_ref[...]], axis=0)
        vall = jnp.concatenate([vp_ref[...], vc_ref[...], vn_ref[...]], axis=0)
        one = jnp.ones((3 * BLK, LANES), BF16)

        def stage_a(j, p, var):
            kd = kall[j * BLK:(j + 3) * BLK, (p // 2) * LANES:(p // 2 + 1) * LANES]
            qt = q_ref[j * BLK:(j + 1) * BLK, p * LANES:(p + 1) * LANES]
            zero = jnp.zeros_like(qt)
            rhs_t = jnp.concatenate([jnp.where(left, qt, zero), jnp.where(left, zero, qt)], axis=0)
            lg = lax.dot_general(kd, rhs_t, (((1,), (1,)), ((), ())), preferred_element_type=F32)
            lg = lg + jnp.concatenate([tab_ref[var, 2 * p], tab_ref[var, 2 * p + 1]], axis=1)
            for kb in range(3):
                blk = lg[kb * BLK:(kb + 1) * BLK]
                mk = jnp.max(blk, axis=0, keepdims=True)
                pe_ref[even, j, p, kb * BLK:(kb + 1) * BLK, :] = jnp.exp2(blk - mk).astype(BF16)
                mk_ref[even, j, p, kb:kb + 1, :] = mk

        def stage_b(j, p):
            kv = p // 2
            vt = vall[j * BLK:(j + 3) * BLK, (kv // 2) * LANES:(kv // 2 + 1) * LANES]
            vmod = jnp.where(left3, vt, one) if kv % 2 == 0 else jnp.where(left3, one, vt)
            r0 = (kv % 2) * ATTN_HEAD_DIM
            d0 = ATTN_HEAD_DIM - r0
            mk = mk_ref[odd, j, p]
            sp = s2[p:p + 1, :]
            m = jnp.maximum(jnp.max(mk[0:3], axis=0, keepdims=True), sp)
            resc = jnp.exp2(mk[0:3] - m).astype(BF16)
            pes = jnp.concatenate([pe_ref[odd, j, p, kb * BLK:(kb + 1) * BLK, :] * resc[kb:kb + 1]
                                   for kb in range(3)], axis=0)
            o2 = lax.dot_general(vmod, pes, (((0,), (0,)), ((), ())), preferred_element_type=F32)
            o = o2[r0:r0 + ATTN_HEAD_DIM, :] / (o2[d0:d0 + 1, :] + jnp.exp2(sp - m))
            ot_ref[odd, j, 2 * p * ATTN_HEAD_DIM:(2 * p + 1) * ATTN_HEAD_DIM, :] = o[:, 0:BLK]
            ot_ref[odd, j, (2 * p + 1) * ATTN_HEAD_DIM:(2 * p + 2) * ATTN_HEAD_DIM, :] = o[:, BLK:2 * BLK]

        def stage_c(j, p, ssq):
            rows = slice(j * BLK, (j + 1) * BLK)
            cols = slice(p * LANES, (p + 1) * LANES)
            v = ot_ref[even, j, p * LANES:(p + 1) * LANES, :].T * ga_ref[rows, cols].astype(F32)
            vs_ref[rows, cols] = v
            return ssq + jnp.sum(v * v, axis=-1, keepdims=True)

        scan_units = _states_units(alog_ref, e_ref, tril_ref, xsf_ref, bf_ref, dtf_ref, xsb_ref, bb_ref,
                                   dtb_ref, sf_ref, sb_ref, cf_ref, cb_ref)
        per_block = len(scan_units) // ATTN_NB
        npair = ATTN_HEADS // 2

        for j in range(ATTN_NB):
            pos = lax.rem(first_blk + j, nblk)
            var = jnp.where(pos == 0, 1, jnp.where(pos == nblk - 1, 2, 0))
            rows = slice(j * BLK, (j + 1) * BLK)
            ssq = jnp.zeros((BLK, 1), F32)
            for p in range(ATTN_HEADS // 2):
                stage_a(j, p, var)
                stage_b(j, p)
                ssq = stage_c(j, p, ssq)
                if (p + 1) % (npair // per_block) == 0:
                    scan_units.pop(0)()
            y_ref[rows, :] = (vs_ref[rows, :] * lax.rsqrt(ssq * (1.0 / D_ATTN) + EPS) * gn_ref[...]).astype(BF16)

    @pl.when(lax.rem(g, 2) == 0)
    def _():
        run(0, 1)

    @pl.when(lax.rem(g, 2) == 1)
    def _():
        run(1, 0)


def _attn(rel_bias, bucket_t, sink_x, attn_norm_g, q, kd, v, ga, a_log, e, tril, xs, bm, dt, batch, seq):
    n = q.shape[0]
    nblk = seq // BLK
    assert nblk % ATTN_NB == 0 and ATTN_NB * BLK == Q * CHUNK
    nstep = batch * nblk // ATTN_NB
    per_seq = nblk // ATTN_NB
    nchunk = batch * seq // CHUNK

    def mirrored(lag):
        def index(g):
            s = step(lag)(g)
            return s - lax.rem(s, per_seq) + per_seq - 1 - lax.rem(s, per_seq)
        return index

    def step(lag):
        return lambda g: jnp.clip(g - lag, 0, nstep - 1)

    def cur(lag):
        return lambda g: (step(lag)(g), 0)

    def prev(lag):
        def index(g):
            b = step(lag)(g) * ATTN_NB
            return (b - jnp.where(lax.rem(b, nblk) > 0, 1, 0), 0)
        return index

    def nxt(lag):
        def index(g):
            b = step(lag)(g) * ATTN_NB + ATTN_NB - 1
            return (b + jnp.where(lax.rem(b, nblk) < nblk - 1, 1, 0), 0)
        return index

    const = lambda g: (0, 0)
    kw = 2 * KV_HEADS * ATTN_HEAD_DIM
    vw = KV_HEADS * ATTN_HEAD_DIM
    npair = ATTN_HEADS // 2
    rows = ATTN_NB * BLK
    st_shape = jax.ShapeDtypeStruct((nchunk, SSM_GROUPS * D_STATE, GROUP_W), BF16)
    st_block = (Q, SSM_GROUPS * D_STATE, GROUP_W)
    return pl.pallas_call(
        lambda *refs: _attn_kernel(nblk, nstep, *refs),
        grid=(nstep + 2,),
        in_specs=[
            pl.BlockSpec(memory_space=pltpu.SMEM),
            pl.BlockSpec((3 * BLK, BLK), const),
            pl.BlockSpec((npair, 2 * BLK), const),
            pl.BlockSpec((1, D_ATTN), const),
            pl.BlockSpec((rows, D_ATTN), cur(0)),
            pl.BlockSpec((BLK, kw), prev(0)), pl.BlockSpec((rows, kw), cur(0)), pl.BlockSpec((BLK, kw), nxt(0)),
            pl.BlockSpec((BLK, vw), prev(1)), pl.BlockSpec((rows, vw), cur(1)), pl.BlockSpec((BLK, vw), nxt(1)),
            pl.BlockSpec((rows, D_ATTN), cur(2)),
            pl.BlockSpec((1, LANES), const),
            pl.BlockSpec((Q, 2, 2 * LANES, D_SSM), lambda g: (0, 0, 0, 0), pipeline_mode=pl.Buffered(1)),
            pl.BlockSpec((CHUNK, CHUNK), const),
            pl.BlockSpec((rows, D_SSM), cur(2)),
            pl.BlockSpec((rows, SSM_GROUPS * D_STATE), cur(2)),
            pl.BlockSpec((CHUNK, LANES), lambda g: (0, step(2)(g))),
            pl.BlockSpec((rows, D_SSM), lambda g: (mirrored(2)(g), 0)),
            pl.BlockSpec((rows, SSM_GROUPS * D_STATE), lambda g: (mirrored(2)(g), 0)),
            pl.BlockSpec((CHUNK, LANES), lambda g: (0, mirrored(2)(g))),
        ],
        out_specs=[pl.BlockSpec((rows, D_ATTN), cur(2)),
                   pl.BlockSpec(st_block, lambda g: (step(2)(g), 0, 0)),
                   pl.BlockSpec(st_block, lambda g: (mirrored(2)(g), 0, 0))],
        out_shape=[jax.ShapeDtypeStruct((n, D_ATTN), BF16), st_shape, st_shape],
        scratch_shapes=[pltpu.VMEM((3, ATTN_HEADS, 3 * BLK, BLK), F32),
                        pltpu.VMEM((2, ATTN_NB, npair, 3 * BLK, 2 * BLK), BF16),
                        pltpu.VMEM((2, ATTN_NB, npair, 8, 2 * BLK), F32),
                        pltpu.VMEM((2, ATTN_NB, D_ATTN, BLK), F32),
                        pltpu.VMEM((rows, D_ATTN), F32),
                        pltpu.VMEM((SSM_GROUPS * D_STATE, GROUP_W), F32),
                        pltpu.VMEM((SSM_GROUPS * D_STATE, GROUP_W), F32)],
        compiler_params=pltpu.CompilerParams(dimension_semantics=("arbitrary",),
                                             vmem_limit_bytes=VMEM_LIMIT),
        name="attn_scan",
    )(rel_bias, bucket_t, sink_x, attn_norm_g, q, kd, kd, kd, v, v, v, ga,
      a_log, e, tril, xs, bm, dt, xs, bm, dt)


def _arrange_w_in(w_in):
    wa = w_in[:, 0:WA_COLS].astype(BF16)
    wdt = jnp.pad(w_in[:, WA_COLS:WB_START], ((0, 0), (0, LANES - 2 * SSM_HEADS))).astype(BF16)
    wb = w_in[:, WB_START:WB_START + WB_COLS].astype(BF16)
    wk = wb[:, D_ATTN:OFF_V].reshape(D_MODEL, KV_HEADS, 1, ATTN_HEAD_DIM)
    wkd = jnp.broadcast_to(wk, (D_MODEL, KV_HEADS, 2, ATTN_HEAD_DIM)).reshape(D_MODEL, -1)
    return wa, wb, wkd, wdt


def _expand_matrices():
    e = np.zeros((Q, 2, 2 * LANES, D_SSM), np.float32)
    for c in range(Q):
        for d in range(2):
            for h in range(SSM_HEADS):
                r = (c * 2 + d) * SSM_HEADS + h
                e[c, d, r, h * SSM_HEAD_DIM:(h + 1) * SSM_HEAD_DIM] = 1.0
                e[c, d, LANES + r, h * SSM_HEAD_DIM:(h + 1) * SSM_HEAD_DIM] = 1.0
    return e


def _bcast_matrices():
    m = np.zeros((Q, 2 * LANES, SSM_HEADS * CHUNK), np.float32)
    for c in range(Q):
        for h in range(SSM_HEADS):
            r = (c * 2 + 1) * SSM_HEADS + h
            m[c, r, h * CHUNK:(h + 1) * CHUNK] = 1.0
            m[c, LANES + r, h * CHUNK:(h + 1) * CHUNK] = 1.0
    return m


def kernel(x, norm_in_g, w_in, conv_w, conv_b, dt_bias, a_log, d_skip, ssd_norm_g, rel_bias, sink,
           attn_norm_g, w_out, final_norm_g):
    batch, seq, _ = x.shape
    assert w_out.shape[0] == 1 and seq % (Q * CHUNK) == 0 and seq % TM == 0
    x2 = x.reshape(batch * seq, D_MODEL)

    weights = _arrange_w_in(w_in)
    e = jnp.asarray(_expand_matrices(), BF16)
    tril = jnp.asarray(np.tril(np.ones((CHUNK, CHUNK), np.float32)), BF16)
    a_log2 = jnp.tile(a_log.reshape(1, 2 * SSM_HEADS).astype(F32), (1, Q))
    dt_bias_x = jnp.pad(dt_bias.reshape(1, 2 * SSM_HEADS).astype(F32), ((0, 0), (0, LANES - 2 * SSM_HEADS)))
    dskip_x = jnp.repeat(d_skip.astype(F32), SSM_HEAD_DIM).reshape(1, D_SSM)
    sink_x = jnp.repeat(sink.astype(F32), BLK).reshape(ATTN_HEADS // 2, 2 * BLK)

    z, xs, bm, cm, dt, q, kd, v, ga = _inproj(
        x2, norm_in_g.reshape(1, D_MODEL), weights, conv_w, conv_b.reshape(1, CONV_CH),
        dt_bias_x, seq)
    y_attn, sf, sb = _attn(rel_bias.astype(F32), _bucket_table_t(), sink_x, attn_norm_g.reshape(1, D_ATTN),
                           q, kd, v, ga, a_log2, e, tril, xs, bm, dt, batch, seq)
    out = _ssd_out(a_log2, dskip_x, ssd_norm_g.reshape(1, D_SSM), jnp.asarray(_bcast_matrices(), BF16),
                   jnp.stack([tril, tril.T]), xs, bm, cm, dt, z, sf, sb, y_attn, x2, w_out[0].astype(BF16),
                   final_norm_g.reshape(1, D_MODEL))
    return out.reshape(batch, seq, D_MODEL)
```

```python
import math

import numpy as np
import jax
import jax.numpy as jnp
from jax import lax
from jax.experimental import pallas as pl
from jax.experimental.pallas import tpu as pltpu

F32 = jnp.float32
BF16 = jnp.bfloat16

D_MODEL = 1024
D_SSM = 1024
D_ATTN = 1024
D_MIX = D_SSM + D_ATTN
SSM_HEAD_DIM = 64
SSM_HEADS = D_SSM // SSM_HEAD_DIM
SSM_GROUPS = 2
GROUP_W = D_SSM // SSM_GROUPS
D_STATE = 128
D_CONV = 5
CHUNK = 128
CONV_CH = D_SSM + 2 * SSM_GROUPS * D_STATE
ATTN_HEAD_DIM = 64
ATTN_HEADS = D_ATTN // ATTN_HEAD_DIM
KV_HEADS = 4
WINDOW = 128
BLK = 128
NUM_BUCKETS = 32
MAX_DISTANCE = 128
MAX_EXACT = 8
EPS = 1e-6
NEG_INF = -1e30
SCALE = ATTN_HEAD_DIM ** -0.5
LOG2E = math.log2(math.e)

LANES = 128
HALO = 8
TM = 512
CONV_ROWS = 64
ATTN_NB = 4
Q = 4
VMEM_LIMIT = 56 * 1024 * 1024

WA_COLS = D_SSM + CONV_CH
OFF_Z, OFF_XBC = 0, D_SSM
WB_START = WA_COLS + 2 * SSM_HEADS
OFF_Q = 0
OFF_V = D_ATTN + KV_HEADS * ATTN_HEAD_DIM
OFF_GA = OFF_V + KV_HEADS * ATTN_HEAD_DIM
WB_COLS = OFF_GA + D_ATTN
assert TM == Q * CHUNK and Q * 2 * SSM_HEADS == LANES


def _silu(v):
    return v * jax.nn.sigmoid(v)


def _softplus(v):
    return jnp.maximum(v, 0.0) + jnp.log1p(jnp.exp(-jnp.abs(v)))


def _inproj_kernel(tiles_per_seq, x_ref, g_ref, wa_ref, wb_ref, wkd_ref, wdt_ref, cw_ref, cb_ref, dtb_ref,
                   z_ref, xs_ref, bm_ref, cm_ref, dt_ref, q_ref, kd_ref, v_ref, ga_ref, pad_ref):
    i = pl.program_id(0)

    @pl.when(i == 0)
    def _():
        pad_ref[...] = jnp.zeros_like(pad_ref)

    j = lax.rem(i, tiles_per_seq)
    g = g_ref[...]

    def proj(hv, w_ref, off, n):
        return jnp.dot(hv, w_ref[:, off:off + n], preferred_element_type=F32)

    xv = x_ref[...]
    ms = jnp.mean(xv * xv, axis=-1, keepdims=True)
    h = (xv * lax.rsqrt(ms + EPS) * g).astype(BF16)

    s_cur = lax.rem(i, 3)
    s_prev = lax.rem(i + 2, 3)
    s_next = lax.rem(i + 1, 3)
    xbc = proj(h, wa_ref, OFF_XBC, CONV_CH)
    pad_ref[s_cur, HALO:HALO + TM, :] = xbc
    pad_ref[s_prev, HALO + TM:2 * HALO + TM, :] = xbc[0:HALO] * jnp.where(j > 0, 1.0, 0.0)
    pad_ref[s_next, 0:HALO, :] = xbc[TM - HALO:TM] * jnp.where(j < tiles_per_seq - 1, 1.0, 0.0)

    z_ref[...] = _silu(proj(h, wa_ref, OFF_Z, D_SSM)).astype(BF16)
    q_ref[...] = (proj(h, wb_ref, OFF_Q, D_ATTN) * (SCALE * LOG2E)).astype(BF16)
    kd_ref[...] = proj(h, wkd_ref, 0, 2 * KV_HEADS * ATTN_HEAD_DIM).astype(BF16)
    v_ref[...] = proj(h, wb_ref, OFF_V, KV_HEADS * ATTN_HEAD_DIM).astype(BF16)
    ga_ref[...] = _silu(proj(h, wb_ref, OFF_GA, D_ATTN)).astype(BF16)
    dt_lane = lax.broadcasted_iota(jnp.int32, (TM, LANES), 1) < 2 * SSM_HEADS
    dt = jnp.where(dt_lane, _softplus(proj(h, wdt_ref, 0, LANES) + dtb_ref[...]), 0.0)
    dtc = dt[0:CHUNK]
    for c in range(1, Q):
        dtc = dtc + pltpu.roll(dt[c * CHUNK:(c + 1) * CHUNK], 2 * SSM_HEADS * c, axis=1)
    dt_ref[...] = dtc

    base = HALO - (D_CONV - 1) // 2
    for cc in range(CONV_CH // LANES):
        sl = slice(cc * LANES, (cc + 1) * LANES)
        if cc < D_SSM // LANES:
            dst, o = xs_ref, cc * LANES
        elif cc < (D_SSM + SSM_GROUPS * D_STATE) // LANES:
            dst, o = bm_ref, cc * LANES - D_SSM
        else:
            dst, o = cm_ref, cc * LANES - D_SSM - SSM_GROUPS * D_STATE
        for r0 in range(0, TM, CONV_ROWS):
            acc = cb_ref[:, sl] + cw_ref[0:1, sl] * pad_ref[s_prev, base + r0:base + r0 + CONV_ROWS, sl]
            for k in range(1, D_CONV):
                acc = acc + cw_ref[k:k + 1, sl] * pad_ref[s_prev, base + r0 + k:base + r0 + k + CONV_ROWS, sl]
            dst[r0:r0 + CONV_ROWS, o:o + LANES] = _silu(acc).astype(BF16)


def _inproj(x2, norm_in_g, weights, conv_w, conv_b, dt_bias, seq):
    n = x2.shape[0]
    tiles_per_seq = seq // TM
    ntile = n // TM
    cur = lambda i: (jnp.minimum(i, ntile - 1), 0)
    lag = lambda i: (jnp.maximum(i - 1, 0), 0)
    const = lambda i: (0, 0)
    out_cols = (D_SSM, D_SSM, SSM_GROUPS * D_STATE, SSM_GROUPS * D_STATE,
                D_ATTN, 2 * KV_HEADS * ATTN_HEAD_DIM, KV_HEADS * ATTN_HEAD_DIM, D_ATTN)
    out_maps = (cur, lag, lag, lag, cur, cur, cur, cur)
    out_specs = [pl.BlockSpec((TM, c), m) for c, m in zip(out_cols, out_maps)]
    out_shape = [jax.ShapeDtypeStruct((n, c), BF16) for c in out_cols]
    out_specs.insert(4, pl.BlockSpec((CHUNK, LANES), lambda i: (0, jnp.minimum(i, ntile - 1))))
    out_shape.insert(4, jax.ShapeDtypeStruct((CHUNK, ntile * LANES), F32))
    return pl.pallas_call(
        lambda *refs: _inproj_kernel(tiles_per_seq, *refs),
        grid=(ntile + 1,),
        in_specs=[
            pl.BlockSpec((TM, D_MODEL), cur),
            pl.BlockSpec((1, D_MODEL), const),
            pl.BlockSpec((D_MODEL, WA_COLS), const, pipeline_mode=pl.Buffered(1)),
            pl.BlockSpec((D_MODEL, WB_COLS), const, pipeline_mode=pl.Buffered(1)),
            pl.BlockSpec((D_MODEL, 2 * KV_HEADS * ATTN_HEAD_DIM), const, pipeline_mode=pl.Buffered(1)),
            pl.BlockSpec((D_MODEL, LANES), const, pipeline_mode=pl.Buffered(1)),
            pl.BlockSpec((D_CONV, CONV_CH), const),
            pl.BlockSpec((1, CONV_CH), const),
            pl.BlockSpec((1, LANES), const),
        ],
        out_specs=out_specs,
        out_shape=out_shape,
        scratch_shapes=[pltpu.VMEM((3, TM + 2 * HALO, CONV_CH), F32)],
        compiler_params=pltpu.CompilerParams(dimension_semantics=("arbitrary",),
                                             vmem_limit_bytes=VMEM_LIMIT),
        name="inproj",
    )(x2, norm_in_g, *weights, conv_w, conv_b, dt_bias)


def _split_terms(v, n):
    terms, r = [], v
    for _ in range(n):
        t = r.astype(BF16)
        terms.append(t)
        r = r - t.astype(F32)
    return terms


def _chunk_cumsum(tril, a):
    t0, t1, t2 = _split_terms(a, 3)
    return (jnp.dot(tril, t0, preferred_element_type=F32) + jnp.dot(tril, t1, preferred_element_type=F32)
            + jnp.dot(tril, t2, preferred_element_type=F32))


def _hi_lo(v):
    return jnp.concatenate(_split_terms(v, 2), axis=1)


def _fwd_lanes():
    lane = lax.broadcasted_iota(jnp.int32, (CHUNK, LANES), 1)
    return jnp.bitwise_and(lane, 2 * SSM_HEADS - 1) < SSM_HEADS


def _states_units(alog_ref, e_ref, tril_ref, xsf_ref, bf_ref, dtf_ref, xsb_ref, bb_ref, dtb_ref,
                  sf_ref, sb_ref, cf_ref, cb_ref):
    a_neg = -jnp.exp(alog_ref[...])
    tril = tril_ref[...]
    fwd_lane = _fwd_lanes()

    def prep(dt_ref):
        dt = dt_ref[...]
        a = dt * a_neg
        cs = _chunk_cumsum(tril, a)
        tot = cs[CHUNK - 1:CHUNK, :]
        w = dt * jnp.exp(jnp.where(fwd_lane, tot - cs, cs - a))
        return _hi_lo(jnp.concatenate([w, jnp.broadcast_to(jnp.exp(tot), (8, LANES))], axis=0))

    def chunk(c, d, cat, xs_ref, b_ref, out_ref, carry_ref):
        rows = slice(c * CHUNK, (c + 1) * CHUNK)
        wx = jnp.dot(cat, e_ref[c, d], preferred_element_type=F32)
        xd = (xs_ref[rows, :].astype(F32) * wx[0:CHUNK]).astype(BF16)
        dec = wx[CHUNK:CHUNK + 1, :]
        for g in range(SSM_GROUPS):
            gr = slice(g * D_STATE, (g + 1) * D_STATE)
            gc = slice(g * GROUP_W, (g + 1) * GROUP_W)
            st = lax.dot_general(b_ref[rows, gr], xd[:, gc], (((0,), (0,)), ((), ())),
                                 preferred_element_type=F32)
            prev = carry_ref[gr, :]
            out_ref[c, gr, :] = prev.astype(BF16)
            carry_ref[gr, :] = prev * dec[:, gc] + st

    cat_f = prep(dtf_ref)
    cat_b = prep(dtb_ref)
    units = []
    for c in range(Q):
        units.append(lambda c=c: chunk(c, 0, cat_f, xsf_ref, bf_ref, sf_ref, cf_ref))
        units.append(lambda c=c: chunk(Q - 1 - c, 1, cat_b, xsb_ref, bb_ref, sb_ref, cb_ref))
    return units


def _ssd_out_kernel(alog_ref, dskip_ref, selb_ref, tril_ref, xs_ref, bm_ref, cm_ref, dt_ref,
                    z_ref, sf_ref, sb_ref, ya_ref, x_ref, w_ref, fg_ref, o_ref, ys_ref, vs_ref):
    li = lax.broadcasted_iota(jnp.int32, (CHUNK, CHUNK), 0)
    si = lax.broadcasted_iota(jnp.int32, (CHUNK, CHUNK), 1)
    lower = li >= si
    eye = li == si
    eyeb = jnp.where(eye, 1.0, 0.0).astype(BF16)
    left = lax.broadcasted_iota(jnp.int32, (CHUNK, LANES), 1) < SSM_HEAD_DIM

    dt = dt_ref[...]
    a = dt * (-LOG2E * jnp.exp(alog_ref[...]))
    colq = jnp.where(_fwd_lanes(), _chunk_cumsum(tril_ref[0], a), _chunk_cumsum(tril_ref[1], a))
    rowq = (colq - jnp.log2(dt)).T
    dtr = dt.T
    catq = _hi_lo(colq)

    for c in range(Q):
        rows = slice(c * CHUNK, (c + 1) * CHUNK)
        for g in range(SSM_GROUPS):
            gr = slice(g * D_STATE, (g + 1) * D_STATE)
            cmg = cm_ref[rows, gr]
            cb = lax.dot_general(cmg, bm_ref[rows, gr], (((1,), (1,)), ((), ())),
                                 preferred_element_type=F32)
            cboff = jnp.where(eye, 0.0, cb).astype(BF16)
            cbdiag = jnp.sum(jnp.where(eye, cb, 0.0), axis=0, keepdims=True)
            ssq = jnp.zeros((CHUNK, 1), F32)
            for pp in range(GROUP_W // LANES):
                hd0 = g * (SSM_HEADS // SSM_GROUPS) + pp * 2
                cols = slice(hd0 * SSM_HEAD_DIM, (hd0 + 2) * SSM_HEAD_DIM)
                if pp % 2 == 0:
                    c2 = slice(pp * LANES, (pp + 2) * LANES)
                    csf = jnp.dot(cmg, sf_ref[c, gr, c2], preferred_element_type=F32)
                    csb = jnp.dot(cmg, sb_ref[c, gr, c2], preferred_element_type=F32)
                pc = slice((pp % 2) * LANES, (pp % 2 + 1) * LANES)
                colb = jnp.dot(catq, selb_ref[c, :, hd0 * CHUNK:(hd0 + 2) * CHUNK],
                               preferred_element_type=F32)
                scores, colf_bc, colb_bc = [], [], []
                for hh in range(2):
                    h = c * 2 * SSM_HEADS + hd0 + hh
                    hb = h + SSM_HEADS
                    colf_bc.append(jnp.broadcast_to(colq[:, h:h + 1], (CHUNK, CHUNK)))
                    colb_bc.append(colb[:, hh * CHUNK:(hh + 1) * CHUNK])
                    arg = jnp.where(lower, colf_bc[hh] - rowq[h:h + 1, :], colb_bc[hh] - rowq[hb:hb + 1, :])
                    diag = (cbdiag * (dtr[h:h + 1, :] + dtr[hb:hb + 1, :]) + dskip_ref[hd0 + hh]).astype(BF16)
                    scores.append(cboff * jnp.exp2(arg).astype(BF16) + eyeb * diag)
                xt = xs_ref[rows, cols]
                zero = jnp.zeros_like(xt)
                rhs = jnp.concatenate([jnp.where(left, xt, zero), jnp.where(left, zero, xt)], axis=0)
                y_diag = jnp.dot(jnp.concatenate(scores, axis=1), rhs, preferred_element_type=F32)
                y_off = (csf[:, pc] * jnp.exp2(jnp.where(left, colf_bc[0], colf_bc[1]))
                         + csb[:, pc] * jnp.exp2(jnp.where(left, colb_bc[0], colb_bc[1])))
                vg = (y_diag + y_off) * z_ref[rows, cols].astype(F32)
                vs_ref[rows, cols] = vg
                ssq = ssq + jnp.sum(vg * vg, axis=-1, keepdims=True)
            gc = slice(g * GROUP_W, (g + 1) * GROUP_W)
            ys_ref[rows, gc] = (vs_ref[rows, gc] * lax.rsqrt(ssq * (1.0 / GROUP_W) + EPS)).astype(BF16)

        acc = (jnp.dot(ys_ref[rows, :], w_ref[0:D_SSM, :], preferred_element_type=F32)
               + jnp.dot(ya_ref[rows, :], w_ref[D_SSM:D_MIX, :], preferred_element_type=F32))
        hres = x_ref[rows, :] + acc
        ms = jnp.mean(hres * hres, axis=-1, keepdims=True)
        o_ref[rows, :] = hres * lax.rsqrt(ms + EPS) * fg_ref[...]


def _ssd_out(a_log, d_skip, selb, tril, xs, bm, cm, dt, z, sf, sb, y_attn, x2, w_out, final_norm_g):
    n = xs.shape[0]
    rows = Q * CHUNK
    row = lambda i: (i, 0)
    const = lambda i: (0, 0)
    return pl.pallas_call(
        _ssd_out_kernel,
        grid=(n // rows,),
        in_specs=[
            pl.BlockSpec((1, LANES), const),
            pl.BlockSpec(memory_space=pltpu.SMEM),
            pl.BlockSpec((Q, 2 * LANES, SSM_HEADS * CHUNK), lambda i: (0, 0, 0), pipeline_mode=pl.Buffered(1)),
            pl.BlockSpec((2, CHUNK, CHUNK), lambda i: (0, 0, 0)),
            pl.BlockSpec((rows, D_SSM), row),
            pl.BlockSpec((rows, SSM_GROUPS * D_STATE), row),
            pl.BlockSpec((rows, SSM_GROUPS * D_STATE), row),
            pl.BlockSpec((CHUNK, LANES), lambda i: (0, i)),
            pl.BlockSpec((rows, D_SSM), row),
            pl.BlockSpec((Q, SSM_GROUPS * D_STATE, GROUP_W), lambda i: (i, 0, 0)),
            pl.BlockSpec((Q, SSM_GROUPS * D_STATE, GROUP_W), lambda i: (i, 0, 0)),
            pl.BlockSpec((rows, D_ATTN), row),
            pl.BlockSpec((rows, D_MODEL), row),
            pl.BlockSpec((D_MIX, D_MODEL), const, pipeline_mode=pl.Buffered(1)),
            pl.BlockSpec((1, D_MODEL), const),
        ],
        out_specs=pl.BlockSpec((rows, D_MODEL), row),
        out_shape=jax.ShapeDtypeStruct((n, D_MODEL), F32),
        scratch_shapes=[pltpu.VMEM((rows, D_SSM), BF16), pltpu.VMEM((rows, D_SSM), F32)],
        compiler_params=pltpu.CompilerParams(dimension_semantics=("parallel",),
                                             vmem_limit_bytes=VMEM_LIMIT),
        name="ssd_out",
    )(a_log, d_skip, selb, tril, xs, bm, cm, dt, z, sf, sb, y_attn, x2, w_out, final_norm_g)


def _bucket_table_t():
    rel = jnp.arange(3 * BLK)[:, None] - BLK - jnp.arange(BLK)[None, :]
    half = NUM_BUCKETS // 2
    ret = (rel > 0).astype(jnp.int32) * half
    n = jnp.abs(rel)
    nf = jnp.maximum(n, 1).astype(F32)
    large = MAX_EXACT + (jnp.log(nf / MAX_EXACT) / math.log(MAX_DISTANCE / MAX_EXACT)
                         * (half - MAX_EXACT)).astype(jnp.int32)
    large = jnp.minimum(large, half - 1)
    return (ret + jnp.where(n < MAX_EXACT, n, large)).astype(jnp.int32)


def _attn_kernel(nblk, nstep, rb_ref, bucket_ref, sink_ref, q_ref, kp_ref, kc_ref, kn_ref,
                 vp_ref, vc_ref, vn_ref, ga_ref,
                 alog_ref, e_ref, tril_ref, xsf_ref, bf_ref, dtf_ref, xsb_ref, bb_ref, dtb_ref,
                 y_ref, sf_ref, sb_ref,
                 tab_ref, pe_ref, mk_ref, ot_ref, vs_ref, cf_ref, cb_ref):
    g = pl.program_id(0)

    @pl.when((g == 0) | (lax.rem(g + (nblk // ATTN_NB) - 2, nblk // ATTN_NB) == 0))
    def _():
        cf_ref[...] = jnp.zeros_like(cf_ref)
        cb_ref[...] = jnp.zeros_like(cb_ref)

    @pl.when(g == 0)
    def _init():
        pe_ref[...] = jnp.zeros_like(pe_ref)
        mk_ref[...] = jnp.zeros_like(mk_ref)
        ot_ref[...] = jnp.zeros_like(ot_ref)
        bk = bucket_ref[...]
        t = lax.broadcasted_iota(jnp.int32, (3 * BLK, BLK), 0)
        qq = lax.broadcasted_iota(jnp.int32, (3 * BLK, BLK), 1)
        in_window = jnp.abs(t - BLK - qq) <= WINDOW

        def per_head(h, carry):
            acc = jnp.zeros((3 * BLK, BLK), F32)
            for b in range(NUM_BUCKETS):
                acc = jnp.where(bk == b, rb_ref[b, h], acc)
            tab = jnp.where(in_window, acc * LOG2E, NEG_INF)
            tab_ref[0, h] = tab
            tab_ref[1, h] = jnp.where(t >= BLK, tab, NEG_INF)
            tab_ref[2, h] = jnp.where(t < 2 * BLK, tab, NEG_INF)
            return carry

        lax.fori_loop(0, ATTN_HEADS, per_head, 0)

    left = lax.broadcasted_iota(jnp.int32, (BLK, LANES), 1) < ATTN_HEAD_DIM
    left3 = lax.broadcasted_iota(jnp.int32, (3 * BLK, LANES), 1) < ATTN_HEAD_DIM
    s2 = sink_ref[...] * LOG2E

    def run(even, odd):
        first_blk = jnp.minimum(g, nstep - 1) * ATTN_NB
        kall = jnp.concatenate([kp_ref[...], kc_ref[...], kn_ref[...]], axis=0)
        vall = jnp.concatenate([vp_ref[...], vc_ref[...], vn_ref[...]], axis=0)
        one = jnp.ones((3 * BLK, LANES), BF16)

        def stage_a(j, p, var):
            kd = kall[j * BLK:(j + 3) * BLK, (p // 2) * LANES:(p // 2 + 1) * LANES]
            qt = q_ref[j * BLK:(j + 1) * BLK, p * LANES:(p + 1) * LANES]
            zero = jnp.zeros_like(qt)
            rhs_t = jnp.concatenate([jnp.where(left, qt, zero), jnp.where(left, zero, qt)], axis=0)
            lg = lax.dot_general(kd, rhs_t, (((1,), (1,)), ((), ())), preferred_element_type=F32)
            lg = lg + jnp.concatenate([tab_ref[var, 2 * p], tab_ref[var, 2 * p + 1]], axis=1)
            for kb in range(3):
                blk = lg[kb * BLK:(kb + 1) * BLK]
                mk = jnp.max(blk, axis=0, keepdims=True)
                pe_ref[even, j, p, kb * BLK:(kb + 1) * BLK, :] = jnp.exp2(blk - mk).astype(BF16)
                mk_ref[even, j, p, kb:kb + 1, :] = mk

        def stage_b(j, p):
            kv = p // 2
            vt = vall[j * BLK:(j + 3) * BLK, (kv // 2) * LANES:(kv // 2 + 1) * LANES]
            vmod = jnp.where(left3, vt, one) if kv % 2 == 0 else jnp.where(left3, one, vt)
            r0 = (kv % 2) * ATTN_HEAD_DIM
            d0 = ATTN_HEAD_DIM - r0
            mk = mk_ref[odd, j, p]
            sp = s2[p:p + 1, :]
            m = jnp.maximum(jnp.max(mk[0:3], axis=0, keepdims=True), sp)
            resc = jnp.exp2(mk[0:3] - m).astype(BF16)
            pes = jnp.concatenate([pe_ref[odd, j, p, kb * BLK:(kb + 1) * BLK, :] * resc[kb:kb + 1]
                                   for kb in range(3)], axis=0)
            o2 = lax.dot_general(vmod, pes, (((0,), (0,)), ((), ())), preferred_element_type=F32)
            o = o2[r0:r0 + ATTN_HEAD_DIM, :] / (o2[d0:d0 + 1, :] + jnp.exp2(sp - m))
            ot_ref[odd, j, 2 * p * ATTN_HEAD_DIM:(2 * p + 1) * ATTN_HEAD_DIM, :] = o[:, 0:BLK]
            ot_ref[odd, j, (2 * p + 1) * ATTN_HEAD_DIM:(2 * p + 2) * ATTN_HEAD_DIM, :] = o[:, BLK:2 * BLK]

        def stage_c(j, p, ssq):
            rows = slice(j * BLK, (j + 1) * BLK)
            cols = slice(p * LANES, (p + 1) * LANES)
            v = ot_ref[even, j, p * LANES:(p + 1) * LANES, :].T * ga_ref[rows, cols].astype(F32)
            vs_ref[rows, cols] = v
            return ssq + jnp.sum(v * v, axis=-1, keepdims=True)

        scan_units = _states_units(alog_ref, e_ref, tril_ref, xsf_ref, bf_ref, dtf_ref, xsb_ref, bb_ref,
                                   dtb_ref, sf_ref, sb_ref, cf_ref, cb_ref)
        per_block = len(scan_units) // ATTN_NB
        npair = ATTN_HEADS // 2

        for j in range(ATTN_NB):
            pos = lax.rem(first_blk + j, nblk)
            var = jnp.where(pos == 0, 1, jnp.where(pos == nblk - 1, 2, 0))
            rows = slice(j * BLK, (j + 1) * BLK)
            ssq = jnp.zeros((BLK, 1), F32)
            for p in range(ATTN_HEADS // 2):
                stage_a(j, p, var)
                stage_b(j, p)
                ssq = stage_c(j, p, ssq)
                if (p + 1) % (npair // per_block) == 0:
                    scan_units.pop(0)()
            y_ref[rows, :] = (vs_ref[rows, :] * lax.rsqrt(ssq * (1.0 / D_ATTN) + EPS)).astype(BF16)

    @pl.when(lax.rem(g, 2) == 0)
    def _():
        run(0, 1)

    @pl.when(lax.rem(g, 2) == 1)
    def _():
        run(1, 0)


def _attn(rel_bias, bucket_t, sink_x, q, kd, v, ga, a_log, e, tril, xs, bm, dt, batch, seq):
    n = q.shape[0]
    nblk = seq // BLK
    assert nblk % ATTN_NB == 0 and ATTN_NB * BLK == Q * CHUNK
    nstep = batch * nblk // ATTN_NB
    per_seq = nblk // ATTN_NB
    nchunk = batch * seq // CHUNK

    def mirrored(lag):
        def index(g):
            s = step(lag)(g)
            return s - lax.rem(s, per_seq) + per_seq - 1 - lax.rem(s, per_seq)
        return index

    def step(lag):
        return lambda g: jnp.clip(g - lag, 0, nstep - 1)

    def cur(lag):
        return lambda g: (step(lag)(g), 0)

    def prev(lag):
        def index(g):
            b = step(lag)(g) * ATTN_NB
            return (b - jnp.where(lax.rem(b, nblk) > 0, 1, 0), 0)
        return index

    def nxt(lag):
        def index(g):
            b = step(lag)(g) * ATTN_NB + ATTN_NB - 1
            return (b + jnp.where(lax.rem(b, nblk) < nblk - 1, 1, 0), 0)
        return index

    const = lambda g: (0, 0)
    kw = 2 * KV_HEADS * ATTN_HEAD_DIM
    vw = KV_HEADS * ATTN_HEAD_DIM
    npair = ATTN_HEADS // 2
    rows = ATTN_NB * BLK
    st_shape = jax.ShapeDtypeStruct((nchunk, SSM_GROUPS * D_STATE, GROUP_W), BF16)
    st_block = (Q, SSM_GROUPS * D_STATE, GROUP_W)
    return pl.pallas_call(
        lambda *refs: _attn_kernel(nblk, nstep, *refs),
        grid=(nstep + 2,),
        in_specs=[
            pl.BlockSpec(memory_space=pltpu.SMEM),
            pl.BlockSpec((3 * BLK, BLK), const),
            pl.BlockSpec((npair, 2 * BLK), const),
            pl.BlockSpec((rows, D_ATTN), cur(0)),
            pl.BlockSpec((BLK, kw), prev(0)), pl.BlockSpec((rows, kw), cur(0)), pl.BlockSpec((BLK, kw), nxt(0)),
            pl.BlockSpec((BLK, vw), prev(1)), pl.BlockSpec((rows, vw), cur(1)), pl.BlockSpec((BLK, vw), nxt(1)),
            pl.BlockSpec((rows, D_ATTN), cur(2)),
            pl.BlockSpec((1, LANES), const),
            pl.BlockSpec((Q, 2, 2 * LANES, D_SSM), lambda g: (0, 0, 0, 0), pipeline_mode=pl.Buffered(1)),
            pl.BlockSpec((CHUNK, CHUNK), const),
            pl.BlockSpec((rows, D_SSM), cur(2)),
            pl.BlockSpec((rows, SSM_GROUPS * D_STATE), cur(2)),
            pl.BlockSpec((CHUNK, LANES), lambda g: (0, step(2)(g))),
            pl.BlockSpec((rows, D_SSM), lambda g: (mirrored(2)(g), 0)),
            pl.BlockSpec((rows, SSM_GROUPS * D_STATE), lambda g: (mirrored(2)(g), 0)),
            pl.BlockSpec((CHUNK, LANES), lambda g: (0, mirrored(2)(g))),
        ],
        out_specs=[pl.BlockSpec((rows, D_ATTN), cur(2)),
                   pl.BlockSpec(st_block, lambda g: (step(2)(g), 0, 0)),
                   pl.BlockSpec(st_block, lambda g: (mirrored(2)(g), 0, 0))],
        out_shape=[jax.ShapeDtypeStruct((n, D_ATTN), BF16), st_shape, st_shape],
        scratch_shapes=[pltpu.VMEM((3, ATTN_HEADS, 3 * BLK, BLK), F32),
                        pltpu.VMEM((2, ATTN_NB, npair, 3 * BLK, 2 * BLK), BF16),
                        pltpu.VMEM((2, ATTN_NB, npair, 8, 2 * BLK), F32),
                        pltpu.VMEM((2, ATTN_NB, D_ATTN, BLK), F32),
                        pltpu.VMEM((rows, D_ATTN), F32),
                        pltpu.VMEM((SSM_GROUPS * D_STATE, GROUP_W), F32),
                        pltpu.VMEM((SSM_GROUPS * D_STATE, GROUP_W), F32)],
        compiler_params=pltpu.CompilerParams(dimension_semantics=("arbitrary",),
                                             vmem_limit_bytes=VMEM_LIMIT),
        name="attn_scan",
    )(rel_bias, bucket_t, sink_x, q, kd, kd, kd, v, v, v, ga,
      a_log, e, tril, xs, bm, dt, xs, bm, dt)


def _arrange_w_in(w_in):
    wa = w_in[:, 0:WA_COLS].astype(BF16)
    wdt = jnp.pad(w_in[:, WA_COLS:WB_START], ((0, 0), (0, LANES - 2 * SSM_HEADS))).astype(BF16)
    wb = w_in[:, WB_START:WB_START + WB_COLS].astype(BF16)
    wk = wb[:, D_ATTN:OFF_V].reshape(D_MODEL, KV_HEADS, 1, ATTN_HEAD_DIM)
    wkd = jnp.broadcast_to(wk, (D_MODEL, KV_HEADS, 2, ATTN_HEAD_DIM)).reshape(D_MODEL, -1)
    return wa, wb, wkd, wdt


def _expand_matrices():
    e = np.zeros((Q, 2, 2 * LANES, D_SSM), np.float32)
    for c in range(Q):
        for d in range(2):
            for h in range(SSM_HEADS):
                r = (c * 2 + d) * SSM_HEADS + h
                e[c, d, r, h * SSM_HEAD_DIM:(h + 1) * SSM_HEAD_DIM] = 1.0
                e[c, d, LANES + r, h * SSM_HEAD_DIM:(h + 1) * SSM_HEAD_DIM] = 1.0
    return e


def _bcast_matrices():
    m = np.zeros((Q, 2 * LANES, SSM_HEADS * CHUNK), np.float32)
    for c in range(Q):
        for h in range(SSM_HEADS):
            r = (c * 2 + 1) * SSM_HEADS + h
            m[c, r, h * CHUNK:(h + 1) * CHUNK] = 1.0
            m[c, LANES + r, h * CHUNK:(h + 1) * CHUNK] = 1.0
    return m


def kernel(x, norm_in_g, w_in, conv_w, conv_b, dt_bias, a_log, d_skip, ssd_norm_g, rel_bias, sink,
           attn_norm_g, w_out, final_norm_g):
    batch, seq, _ = x.shape
    assert w_out.shape[0] == 1 and seq % (Q * CHUNK) == 0 and seq % TM == 0
    x2 = x.reshape(batch * seq, D_MODEL)

    weights = _arrange_w_in(w_in)
    e = jnp.asarray(_expand_matrices(), BF16)
    tril = jnp.asarray(np.tril(np.ones((CHUNK, CHUNK), np.float32)), BF16)
    a_log2 = jnp.tile(a_log.reshape(1, 2 * SSM_HEADS).astype(F32), (1, Q))
    dt_bias_x = jnp.pad(dt_bias.reshape(1, 2 * SSM_HEADS).astype(F32), ((0, 0), (0, LANES - 2 * SSM_HEADS)))
    gains = jnp.concatenate([ssd_norm_g, attn_norm_g]).astype(F32)
    w_out_g = (w_out[0] * gains[:, None]).astype(BF16)
    sink_x = jnp.repeat(sink.astype(F32), BLK).reshape(ATTN_HEADS // 2, 2 * BLK)

    z, xs, bm, cm, dt, q, kd, v, ga = _inproj(
        x2, norm_in_g.reshape(1, D_MODEL), weights, conv_w, conv_b.reshape(1, CONV_CH),
        dt_bias_x, seq)
    y_attn, sf, sb = _attn(rel_bias.astype(F32), _bucket_table_t(), sink_x,
                           q, kd, v, ga, a_log2, e, tril, xs, bm, dt, batch, seq)
    out = _ssd_out(a_log2, d_skip.astype(F32), jnp.asarray(_bcast_matrices(), BF16),
                   jnp.stack([tril, tril.T]), xs, bm, cm, dt, z, sf, sb, y_attn, x2, w_out_g,
                   final_norm_g.reshape(1, D_MODEL))
    return out.reshape(batch, seq, D_MODEL)
```

```python
import math

import numpy as np
import jax
import jax.numpy as jnp
from jax import lax
from jax.experimental import pallas as pl
from jax.experimental.pallas import tpu as pltpu

F32 = jnp.float32
BF16 = jnp.bfloat16

D_MODEL = 1024
D_SSM = 1024
D_ATTN = 1024
D_MIX = D_SSM + D_ATTN
SSM_HEAD_DIM = 64
SSM_HEADS = D_SSM // SSM_HEAD_DIM
SSM_GROUPS = 2
GROUP_W = D_SSM // SSM_GROUPS
D_STATE = 128
D_CONV = 5
CHUNK = 128
CONV_CH = D_SSM + 2 * SSM_GROUPS * D_STATE
ATTN_HEAD_DIM = 64
ATTN_HEADS = D_ATTN // ATTN_HEAD_DIM
KV_HEADS = 4
WINDOW = 128
BLK = 128
NUM_BUCKETS = 32
MAX_DISTANCE = 128
MAX_EXACT = 8
EPS = 1e-6
NEG_INF = -1e30
SCALE = ATTN_HEAD_DIM ** -0.5
LOG2E = math.log2(math.e)

LANES = 128
HALO = 8
TM = 512
CONV_ROWS = 64
ATTN_NB = 4
Q = 4
VMEM_LIMIT = 56 * 1024 * 1024

WA_COLS = D_SSM + CONV_CH
OFF_Z, OFF_XBC = 0, D_SSM
WB_START = WA_COLS + 2 * SSM_HEADS
OFF_Q = 0
OFF_V = D_ATTN + KV_HEADS * ATTN_HEAD_DIM
OFF_GA = OFF_V + KV_HEADS * ATTN_HEAD_DIM
WB_COLS = OFF_GA + D_ATTN
assert TM == Q * CHUNK and Q * 2 * SSM_HEADS == LANES


def _silu(v):
    return v * jax.nn.sigmoid(v)


def _softplus(v):
    return jnp.maximum(v, 0.0) + jnp.log1p(jnp.exp(-jnp.abs(v)))


def _inproj_kernel(tiles_per_seq, x_ref, g_ref, wa_ref, wb_ref, wkd_ref, wdt_ref, cw_ref, cb_ref, dtb_ref,
                   z_ref, xs_ref, bm_ref, cm_ref, dt_ref, q_ref, kd_ref, v_ref, ga_ref, pad_ref):
    i = pl.program_id(0)

    @pl.when(i == 0)
    def _():
        pad_ref[...] = jnp.zeros_like(pad_ref)

    j = lax.rem(i, tiles_per_seq)
    g = g_ref[...]

    def proj(hv, w_ref, off, n):
        return jnp.dot(hv, w_ref[:, off:off + n], preferred_element_type=F32)

    xv = x_ref[...]
    ms = jnp.mean(xv * xv, axis=-1, keepdims=True)
    h = (xv * lax.rsqrt(ms + EPS) * g).astype(BF16)

    s_cur = lax.rem(i, 3)
    s_prev = lax.rem(i + 2, 3)
    s_next = lax.rem(i + 1, 3)
    xbc = proj(h, wa_ref, OFF_XBC, CONV_CH)
    pad_ref[s_cur, HALO:HALO + TM, :] = xbc
    pad_ref[s_prev, HALO + TM:2 * HALO + TM, :] = xbc[0:HALO] * jnp.where(j > 0, 1.0, 0.0)
    pad_ref[s_next, 0:HALO, :] = xbc[TM - HALO:TM] * jnp.where(j < tiles_per_seq - 1, 1.0, 0.0)

    z_ref[...] = _silu(proj(h, wa_ref, OFF_Z, D_SSM)).astype(BF16)
    q_ref[...] = (proj(h, wb_ref, OFF_Q, D_ATTN) * (SCALE * LOG2E)).astype(BF16)
    kd_ref[...] = proj(h, wkd_ref, 0, 2 * KV_HEADS * ATTN_HEAD_DIM).astype(BF16)
    v_ref[...] = proj(h, wb_ref, OFF_V, KV_HEADS * ATTN_HEAD_DIM).astype(BF16)
    ga_ref[...] = _silu(proj(h, wb_ref, OFF_GA, D_ATTN)).astype(BF16)
    dt_lane = lax.broadcasted_iota(jnp.int32, (TM, LANES), 1) < 2 * SSM_HEADS
    dt = jnp.where(dt_lane, _softplus(proj(h, wdt_ref, 0, LANES) + dtb_ref[...]), 0.0)
    dtc = dt[0:CHUNK]
    for c in range(1, Q):
        dtc = dtc + pltpu.roll(dt[c * CHUNK:(c + 1) * CHUNK], 2 * SSM_HEADS * c, axis=1)
    dt_ref[...] = dtc

    base = HALO - (D_CONV - 1) // 2
    for cc in range(CONV_CH // LANES):
        sl = slice(cc * LANES, (cc + 1) * LANES)
        if cc < D_SSM // LANES:
            dst, o = xs_ref, cc * LANES
        elif cc < (D_SSM + SSM_GROUPS * D_STATE) // LANES:
            dst, o = bm_ref, cc * LANES - D_SSM
        else:
            dst, o = cm_ref, cc * LANES - D_SSM - SSM_GROUPS * D_STATE
        for r0 in range(0, TM, CONV_ROWS):
            acc = cb_ref[:, sl] + cw_ref[0:1, sl] * pad_ref[s_prev, base + r0:base + r0 + CONV_ROWS, sl]
            for k in range(1, D_CONV):
                acc = acc + cw_ref[k:k + 1, sl] * pad_ref[s_prev, base + r0 + k:base + r0 + k + CONV_ROWS, sl]
            dst[r0:r0 + CONV_ROWS, o:o + LANES] = _silu(acc).astype(BF16)


def _inproj(x2, norm_in_g, weights, conv_w, conv_b, dt_bias, seq):
    n = x2.shape[0]
    tiles_per_seq = seq // TM
    ntile = n // TM
    cur = lambda i: (jnp.minimum(i, ntile - 1), 0)
    lag = lambda i: (jnp.maximum(i - 1, 0), 0)
    const = lambda i: (0, 0)
    out_cols = (D_SSM, D_SSM, SSM_GROUPS * D_STATE, SSM_GROUPS * D_STATE,
                D_ATTN, 2 * KV_HEADS * ATTN_HEAD_DIM, KV_HEADS * ATTN_HEAD_DIM, D_ATTN)
    out_maps = (cur, lag, lag, lag, cur, cur, cur, cur)
    out_specs = [pl.BlockSpec((TM, c), m) for c, m in zip(out_cols, out_maps)]
    out_shape = [jax.ShapeDtypeStruct((n, c), BF16) for c in out_cols]
    out_specs.insert(4, pl.BlockSpec((CHUNK, LANES), lambda i: (0, jnp.minimum(i, ntile - 1))))
    out_shape.insert(4, jax.ShapeDtypeStruct((CHUNK, ntile * LANES), F32))
    return pl.pallas_call(
        lambda *refs: _inproj_kernel(tiles_per_seq, *refs),
        grid=(ntile + 1,),
        in_specs=[
            pl.BlockSpec((TM, D_MODEL), cur),
            pl.BlockSpec((1, D_MODEL), const),
            pl.BlockSpec((D_MODEL, WA_COLS), const, pipeline_mode=pl.Buffered(1)),
            pl.BlockSpec((D_MODEL, WB_COLS), const, pipeline_mode=pl.Buffered(1)),
            pl.BlockSpec((D_MODEL, 2 * KV_HEADS * ATTN_HEAD_DIM), const, pipeline_mode=pl.Buffered(1)),
            pl.BlockSpec((D_MODEL, LANES), const, pipeline_mode=pl.Buffered(1)),
            pl.BlockSpec((D_CONV, CONV_CH), const),
            pl.BlockSpec((1, CONV_CH), const),
            pl.BlockSpec((1, LANES), const),
        ],
        out_specs=out_specs,
        out_shape=out_shape,
        scratch_shapes=[pltpu.VMEM((3, TM + 2 * HALO, CONV_CH), F32)],
        compiler_params=pltpu.CompilerParams(dimension_semantics=("arbitrary",),
                                             vmem_limit_bytes=VMEM_LIMIT),
        name="inproj",
    )(x2, norm_in_g, *weights, conv_w, conv_b, dt_bias)


def _split_terms(v, n):
    terms, r = [], v
    for _ in range(n):
        t = r.astype(BF16)
        terms.append(t)
        r = r - t.astype(F32)
    return terms


def _chunk_cumsum(tril, a):
    t0, t1, t2 = _split_terms(a, 3)
    return (jnp.dot(tril, t0, preferred_element_type=F32) + jnp.dot(tril, t1, preferred_element_type=F32)
            + jnp.dot(tril, t2, preferred_element_type=F32))


def _hi_lo(v):
    return jnp.concatenate(_split_terms(v, 2), axis=1)


def _fwd_lanes():
    lane = lax.broadcasted_iota(jnp.int32, (CHUNK, LANES), 1)
    return jnp.bitwise_and(lane, 2 * SSM_HEADS - 1) < SSM_HEADS


def _states_units(alog_ref, e_ref, tril_ref, xsf_ref, bf_ref, dtf_ref, xsb_ref, bb_ref, dtb_ref,
                  sf_ref, sb_ref, cf_ref, cb_ref):
    a_neg = -jnp.exp(alog_ref[...])
    tril = tril_ref[...]
    fwd_lane = _fwd_lanes()

    def prep(dt_ref):
        dt = dt_ref[...]
        a = dt * a_neg
        cs = _chunk_cumsum(tril, a)
        tot = cs[CHUNK - 1:CHUNK, :]
        w = dt * jnp.exp(jnp.where(fwd_lane, tot - cs, cs - a))
        return _hi_lo(jnp.concatenate([w, jnp.broadcast_to(jnp.exp(tot), (8, LANES))], axis=0))

    def chunk(c, d, cat, xs_ref, b_ref, out_ref, carry_ref):
        rows = slice(c * CHUNK, (c + 1) * CHUNK)
        wx = jnp.dot(cat, e_ref[c, d], preferred_element_type=F32)
        xd = (xs_ref[rows, :].astype(F32) * wx[0:CHUNK]).astype(BF16)
        dec = wx[CHUNK:CHUNK + 1, :]
        for g in range(SSM_GROUPS):
            gr = slice(g * D_STATE, (g + 1) * D_STATE)
            gc = slice(g * GROUP_W, (g + 1) * GROUP_W)
            st = lax.dot_general(b_ref[rows, gr], xd[:, gc], (((0,), (0,)), ((), ())),
                                 preferred_element_type=F32)
            prev = carry_ref[gr, :]
            out_ref[c, gr, :] = prev.astype(BF16)
            carry_ref[gr, :] = prev * dec[:, gc] + st

    cat_f = prep(dtf_ref)
    cat_b = prep(dtb_ref)
    units = []
    for c in range(Q):
        units.append(lambda c=c: chunk(c, 0, cat_f, xsf_ref, bf_ref, sf_ref, cf_ref))
        units.append(lambda c=c: chunk(Q - 1 - c, 1, cat_b, xsb_ref, bb_ref, sb_ref, cb_ref))
    return units


def _ssd_out_kernel(alog_ref, dskip_ref, selb_ref, tril_ref, xs_ref, bm_ref, cm_ref, dt_ref,
                    z_ref, sf_ref, sb_ref, ya_ref, x_ref, w_ref, fg_ref, o_ref, ys_ref, vs_ref):
    li = lax.broadcasted_iota(jnp.int32, (CHUNK, CHUNK), 0)
    si = lax.broadcasted_iota(jnp.int32, (CHUNK, CHUNK), 1)
    lower = li >= si
    eye = li == si
    eyeb = jnp.where(eye, 1.0, 0.0).astype(BF16)
    left = lax.broadcasted_iota(jnp.int32, (CHUNK, LANES), 1) < SSM_HEAD_DIM

    dt = dt_ref[...]
    a = dt * (-LOG2E * jnp.exp(alog_ref[...]))
    colq = jnp.where(_fwd_lanes(), _chunk_cumsum(tril_ref[0], a), _chunk_cumsum(tril_ref[1], a))
    rowq = (colq - jnp.log2(dt)).T
    dtr = dt.T
    catq = _hi_lo(colq)

    for c in range(Q):
        rows = slice(c * CHUNK, (c + 1) * CHUNK)
        for g in range(SSM_GROUPS):
            gr = slice(g * D_STATE, (g + 1) * D_STATE)
            cmg = cm_ref[rows, gr]
            cb = lax.dot_general(cmg, bm_ref[rows, gr], (((1,), (1,)), ((), ())),
                                 preferred_element_type=F32)
            cboff = jnp.where(eye, 0.0, cb).astype(BF16)
            cbdiag = jnp.sum(jnp.where(eye, cb, 0.0), axis=0, keepdims=True)
            ssq = jnp.zeros((CHUNK, 1), F32)
            for pp in range(GROUP_W // LANES):
                hd0 = g * (SSM_HEADS // SSM_GROUPS) + pp * 2
                cols = slice(hd0 * SSM_HEAD_DIM, (hd0 + 2) * SSM_HEAD_DIM)
                if pp % 2 == 0:
                    c2 = slice(pp * LANES, (pp + 2) * LANES)
                    csf = jnp.dot(cmg, sf_ref[c, gr, c2], preferred_element_type=F32)
                    csb = jnp.dot(cmg, sb_ref[c, gr, c2], preferred_element_type=F32)
                pc = slice((pp % 2) * LANES, (pp % 2 + 1) * LANES)
                colb = jnp.dot(catq, selb_ref[c, :, hd0 * CHUNK:(hd0 + 2) * CHUNK],
                               preferred_element_type=F32)
                scores, colf_bc, colb_bc = [], [], []
                for hh in range(2):
                    h = c * 2 * SSM_HEADS + hd0 + hh
                    hb = h + SSM_HEADS
                    colf_bc.append(jnp.broadcast_to(colq[:, h:h + 1], (CHUNK, CHUNK)))
                    colb_bc.append(colb[:, hh * CHUNK:(hh + 1) * CHUNK])
                    arg = jnp.where(lower, colf_bc[hh] - rowq[h:h + 1, :], colb_bc[hh] - rowq[hb:hb + 1, :])
                    diag = (cbdiag * (dtr[h:h + 1, :] + dtr[hb:hb + 1, :]) + dskip_ref[hd0 + hh]).astype(BF16)
                    scores.append(cboff * jnp.exp2(arg).astype(BF16) + eyeb * diag)
                xt = xs_ref[rows, cols]
                zero = jnp.zeros_like(xt)
                rhs = jnp.concatenate([jnp.where(left, xt, zero), jnp.where(left, zero, xt)], axis=0)
                y_diag = jnp.dot(jnp.concatenate(scores, axis=1), rhs, preferred_element_type=F32)
                y_off = (csf[:, pc] * jnp.exp2(jnp.where(left, colf_bc[0], colf_bc[1]))
                         + csb[:, pc] * jnp.exp2(jnp.where(left, colb_bc[0], colb_bc[1])))
                vg = (y_diag + y_off) * z_ref[rows, cols].astype(F32)
                vs_ref[rows, cols] = vg
                ssq = ssq + jnp.sum(vg * vg, axis=-1, keepdims=True)
            gc = slice(g * GROUP_W, (g + 1) * GROUP_W)
            ys_ref[rows, gc] = (vs_ref[rows, gc] * lax.rsqrt(ssq * (1.0 / GROUP_W) + EPS)).astype(BF16)

        acc = (jnp.dot(ys_ref[rows, :], w_ref[0:D_SSM, :], preferred_element_type=F32)
               + jnp.dot(ya_ref[rows, :], w_ref[D_SSM:D_MIX, :], preferred_element_type=F32))
        hres = x_ref[rows, :] + acc
        ms = jnp.mean(hres * hres, axis=-1, keepdims=True)
        o_ref[rows, :] = hres * lax.rsqrt(ms + EPS) * fg_ref[...]


def _ssd_out(a_log, d_skip, selb, tril, xs, bm, cm, dt, z, sf, sb, y_attn, x2, w_out, final_norm_g):
    n = xs.shape[0]
    rows = Q * CHUNK
    row = lambda i: (i, 0)
    const = lambda i: (0, 0)
    return pl.pallas_call(
        _ssd_out_kernel,
        grid=(n // rows,),
        in_specs=[
            pl.BlockSpec((1, LANES), const),
            pl.BlockSpec(memory_space=pltpu.SMEM),
            pl.BlockSpec((Q, 2 * LANES, SSM_HEADS * CHUNK), lambda i: (0, 0, 0), pipeline_mode=pl.Buffered(1)),
            pl.BlockSpec((2, CHUNK, CHUNK), lambda i: (0, 0, 0)),
            pl.BlockSpec((rows, D_SSM), row),
            pl.BlockSpec((rows, SSM_GROUPS * D_STATE), row),
            pl.BlockSpec((rows, SSM_GROUPS * D_STATE), row),
            pl.BlockSpec((CHUNK, LANES), lambda i: (0, i)),
            pl.BlockSpec((rows, D_SSM), row),
            pl.BlockSpec((Q, SSM_GROUPS * D_STATE, GROUP_W), lambda i: (i, 0, 0)),
            pl.BlockSpec((Q, SSM_GROUPS * D_STATE, GROUP_W), lambda i: (i, 0, 0)),
            pl.BlockSpec((rows, D_ATTN), row),
            pl.BlockSpec((rows, D_MODEL), row),
            pl.BlockSpec((D_MIX, D_MODEL), const, pipeline_mode=pl.Buffered(1)),
            pl.BlockSpec((1, D_MODEL), const),
        ],
        out_specs=pl.BlockSpec((rows, D_MODEL), row),
        out_shape=jax.ShapeDtypeStruct((n, D_MODEL), F32),
        scratch_shapes=[pltpu.VMEM((rows, D_SSM), BF16), pltpu.VMEM((rows, D_SSM), F32)],
        compiler_params=pltpu.CompilerParams(dimension_semantics=("parallel",),
                                             vmem_limit_bytes=VMEM_LIMIT),
        name="ssd_out",
    )(a_log, d_skip, selb, tril, xs, bm, cm, dt, z, sf, sb, y_attn, x2, w_out, final_norm_g)


def _bucket_table_t():
    rel = jnp.arange(3 * BLK)[:, None] - BLK - jnp.arange(BLK)[None, :]
    half = NUM_BUCKETS // 2
    ret = (rel > 0).astype(jnp.int32) * half
    n = jnp.abs(rel)
    nf = jnp.maximum(n, 1).astype(F32)
    large = MAX_EXACT + (jnp.log(nf / MAX_EXACT) / math.log(MAX_DISTANCE / MAX_EXACT)
                         * (half - MAX_EXACT)).astype(jnp.int32)
    large = jnp.minimum(large, half - 1)
    return (ret + jnp.where(n < MAX_EXACT, n, large)).astype(jnp.int32)


def _attn_kernel(nblk, nstep, rb_ref, bucket_ref, sink_ref, q_ref, kp_ref, kc_ref, kn_ref,
                 vp_ref, vc_ref, vn_ref, ga_ref,
                 alog_ref, e_ref, tril_ref, xsf_ref, bf_ref, dtf_ref, xsb_ref, bb_ref, dtb_ref,
                 y_ref, sf_ref, sb_ref,
                 tab_ref, pe_ref, mk_ref, ot_ref, vs_ref, cf_ref, cb_ref):
    g = pl.program_id(0)

    @pl.when((g == 0) | (lax.rem(g + (nblk // ATTN_NB) - 2, nblk // ATTN_NB) == 0))
    def _():
        cf_ref[...] = jnp.zeros_like(cf_ref)
        cb_ref[...] = jnp.zeros_like(cb_ref)

    @pl.when(g == 0)
    def _init():
        pe_ref[...] = jnp.zeros_like(pe_ref)
        mk_ref[...] = jnp.zeros_like(mk_ref)
        ot_ref[...] = jnp.zeros_like(ot_ref)
        bk = bucket_ref[...]
        t = lax.broadcasted_iota(jnp.int32, (3 * BLK, BLK), 0)
        qq = lax.broadcasted_iota(jnp.int32, (3 * BLK, BLK), 1)
        in_window = jnp.abs(t - BLK - qq) <= WINDOW

        def per_head(h, carry):
            acc = jnp.zeros((3 * BLK, BLK), F32)
            for b in range(NUM_BUCKETS):
                acc = jnp.where(bk == b, rb_ref[b, h], acc)
            tab = jnp.where(in_window, acc * LOG2E, NEG_INF)
            tab_ref[0, h] = tab
            tab_ref[1, h] = jnp.where(t >= BLK, tab, NEG_INF)
            tab_ref[2, h] = jnp.where(t < 2 * BLK, tab, NEG_INF)
            return carry

        lax.fori_loop(0, ATTN_HEADS, per_head, 0)

    left = lax.broadcasted_iota(jnp.int32, (BLK, LANES), 1) < ATTN_HEAD_DIM
    left3 = lax.broadcasted_iota(jnp.int32, (3 * BLK, LANES), 1) < ATTN_HEAD_DIM
    s2 = sink_ref[...] * LOG2E

    def run(even, odd):
        first_blk = jnp.minimum(g, nstep - 1) * ATTN_NB
        kall = jnp.concatenate([kp_ref[...], kc_ref[...], kn_ref[...]], axis=0)
        vall = jnp.concatenate([vp_ref[...], vc_ref[...], vn_ref[...]], axis=0)
        one = jnp.ones((3 * BLK, LANES), BF16)

        def stage_a(j, p, var):
            kd = kall[j * BLK:(j + 3) * BLK, (p // 2) * LANES:(p // 2 + 1) * LANES]
            qt = q_ref[j * BLK:(j + 1) * BLK, p * LANES:(p + 1) * LANES]
            zero = jnp.zeros_like(qt)
            rhs_t = jnp.concatenate([jnp.where(left, qt, zero), jnp.where(left, zero, qt)], axis=0)
            lg = lax.dot_general(kd, rhs_t, (((1,), (1,)), ((), ())), preferred_element_type=F32)
            lg = lg + jnp.concatenate([tab_ref[var, 2 * p], tab_ref[var, 2 * p + 1]], axis=1)
            for kb in range(3):
                blk = lg[kb * BLK:(kb + 1) * BLK]
                mk = jnp.max(blk, axis=0, keepdims=True)
                pe_ref[even, j, p, kb * BLK:(kb + 1) * BLK, :] = jnp.exp2(blk - mk).astype(BF16)
                mk_ref[even, j, p, kb:kb + 1, :] = mk

        def stage_b(j, p):
            kv = p // 2
            vt = vall[j * BLK:(j + 3) * BLK, (kv // 2) * LANES:(kv // 2 + 1) * LANES]
            vmod = jnp.where(left3, vt, one) if kv % 2 == 0 else jnp.where(left3, one, vt)
            r0 = (kv % 2) * ATTN_HEAD_DIM
            d0 = ATTN_HEAD_DIM - r0
            mk = mk_ref[odd, j, p]
            sp = s2[p:p + 1, :]
            m = jnp.maximum(jnp.max(mk[0:3], axis=0, keepdims=True), sp)
            resc = jnp.exp2(mk[0:3] - m).astype(BF16)
            pes = jnp.concatenate([pe_ref[odd, j, p, kb * BLK:(kb + 1) * BLK, :] * resc[kb:kb + 1]
                                   for kb in range(3)], axis=0)
            o2 = lax.dot_general(vmod, pes, (((0,), (0,)), ((), ())), preferred_element_type=F32)
            o = o2[r0:r0 + ATTN_HEAD_DIM, :] / (o2[d0:d0 + 1, :] + jnp.exp2(sp - m))
            ot_ref[odd, j, 2 * p * ATTN_HEAD_DIM:(2 * p + 1) * ATTN_HEAD_DIM, :] = o[:, 0:BLK]
            ot_ref[odd, j, (2 * p + 1) * ATTN_HEAD_DIM:(2 * p + 2) * ATTN_HEAD_DIM, :] = o[:, BLK:2 * BLK]

        def stage_c(j, p, ssq):
            rows = slice(j * BLK, (j + 1) * BLK)
            cols = slice(p * LANES, (p + 1) * LANES)
            v = ot_ref[even, j, p * LANES:(p + 1) * LANES, :].T * ga_ref[rows, cols].astype(F32)
            vs_ref[rows, cols] = v
            return ssq + jnp.sum(v * v, axis=-1, keepdims=True)

        scan_units = _states_units(alog_ref, e_ref, tril_ref, xsf_ref, bf_ref, dtf_ref, xsb_ref, bb_ref,
                                   dtb_ref, sf_ref, sb_ref, cf_ref, cb_ref)
        per_block = len(scan_units) // ATTN_NB
        npair = ATTN_HEADS // 2

        for j in range(ATTN_NB):
            pos = lax.rem(first_blk + j, nblk)
            var = jnp.where(pos == 0, 1, jnp.where(pos == nblk - 1, 2, 0))
            rows = slice(j * BLK, (j + 1) * BLK)
            ssq = jnp.zeros((BLK, 1), F32)
            for p in range(ATTN_HEADS // 2):
                stage_a(j, p, var)
                stage_b(j, p)
                ssq = stage_c(j, p, ssq)
                if p % (npair // per_block) == 2:
                    scan_units.pop(0)()
            y_ref[rows, :] = (vs_ref[rows, :] * lax.rsqrt(ssq * (1.0 / D_ATTN) + EPS)).astype(BF16)

    @pl.when(lax.rem(g, 2) == 0)
    def _():
        run(0, 1)

    @pl.when(lax.rem(g, 2) == 1)
    def _():
        run(1, 0)


def _attn(rel_bias, bucket_t, sink_x, q, kd, v, ga, a_log, e, tril, xs, bm, dt, batch, seq):
    n = q.shape[0]
    nblk = seq // BLK
    assert nblk % ATTN_NB == 0 and ATTN_NB * BLK == Q * CHUNK
    nstep = batch * nblk // ATTN_NB
    per_seq = nblk // ATTN_NB
    nchunk = batch * seq // CHUNK

    def mirrored(lag):
        def index(g):
            s = step(lag)(g)
            return s - lax.rem(s, per_seq) + per_seq - 1 - lax.rem(s, per_seq)
        return index

    def step(lag):
        return lambda g: jnp.clip(g - lag, 0, nstep - 1)

    def cur(lag):
        return lambda g: (step(lag)(g), 0)

    def prev(lag):
        def index(g):
            b = step(lag)(g) * ATTN_NB
            return (b - jnp.where(lax.rem(b, nblk) > 0, 1, 0), 0)
        return index

    def nxt(lag):
        def index(g):
            b = step(lag)(g) * ATTN_NB + ATTN_NB - 1
            return (b + jnp.where(lax.rem(b, nblk) < nblk - 1, 1, 0), 0)
        return index

    const = lambda g: (0, 0)
    kw = 2 * KV_HEADS * ATTN_HEAD_DIM
    vw = KV_HEADS * ATTN_HEAD_DIM
    npair = ATTN_HEADS // 2
    rows = ATTN_NB * BLK
    st_shape = jax.ShapeDtypeStruct((nchunk, SSM_GROUPS * D_STATE, GROUP_W), BF16)
    st_block = (Q, SSM_GROUPS * D_STATE, GROUP_W)
    return pl.pallas_call(
        lambda *refs: _attn_kernel(nblk, nstep, *refs),
        grid=(nstep + 2,),
        in_specs=[
            pl.BlockSpec(memory_space=pltpu.SMEM),
            pl.BlockSpec((3 * BLK, BLK), const),
            pl.BlockSpec((npair, 2 * BLK), const),
            pl.BlockSpec((rows, D_ATTN), cur(0)),
            pl.BlockSpec((BLK, kw), prev(0)), pl.BlockSpec((rows, kw), cur(0)), pl.BlockSpec((BLK, kw), nxt(0)),
            pl.BlockSpec((BLK, vw), prev(1)), pl.BlockSpec((rows, vw), cur(1)), pl.BlockSpec((BLK, vw), nxt(1)),
            pl.BlockSpec((rows, D_ATTN), cur(2)),
            pl.BlockSpec((1, LANES), const),
            pl.BlockSpec((Q, 2, 2 * LANES, D_SSM), lambda g: (0, 0, 0, 0), pipeline_mode=pl.Buffered(1)),
            pl.BlockSpec((CHUNK, CHUNK), const),
            pl.BlockSpec((rows, D_SSM), cur(2)),
            pl.BlockSpec((rows, SSM_GROUPS * D_STATE), cur(2)),
            pl.BlockSpec((CHUNK, LANES), lambda g: (0, step(2)(g))),
            pl.BlockSpec((rows, D_SSM), lambda g: (mirrored(2)(g), 0)),
            pl.BlockSpec((rows, SSM_GROUPS * D_STATE), lambda g: (mirrored(2)(g), 0)),
            pl.BlockSpec((CHUNK, LANES), lambda g: (0, mirrored(2)(g))),
        ],
        out_specs=[pl.BlockSpec((rows, D_ATTN), cur(2)),
                   pl.BlockSpec(st_block, lambda g: (step(2)(g), 0, 0)),
                   pl.BlockSpec(st_block, lambda g: (mirrored(2)(g), 0, 0))],
        out_shape=[jax.ShapeDtypeStruct((n, D_ATTN), BF16), st_shape, st_shape],
        scratch_shapes=[pltpu.VMEM((3, ATTN_HEADS, 3 * BLK, BLK), F32),
                        pltpu.VMEM((2, ATTN_NB, npair, 3 * BLK, 2 * BLK), BF16),
                        pltpu.VMEM((2, ATTN_NB, npair, 8, 2 * BLK), F32),
                        pltpu.VMEM((2, ATTN_NB, D_ATTN, BLK), F32),
                        pltpu.VMEM((rows, D_ATTN), F32),
                        pltpu.VMEM((SSM_GROUPS * D_STATE, GROUP_W), F32),
                        pltpu.VMEM((SSM_GROUPS * D_STATE, GROUP_W), F32)],
        compiler_params=pltpu.CompilerParams(dimension_semantics=("arbitrary",),
                                             vmem_limit_bytes=VMEM_LIMIT),
        name="attn_scan",
    )(rel_bias, bucket_t, sink_x, q, kd, kd, kd, v, v, v, ga,
      a_log, e, tril, xs, bm, dt, xs, bm, dt)


def _arrange_w_in(w_in):
    wa = w_in[:, 0:WA_COLS].astype(BF16)
    wdt = jnp.pad(w_in[:, WA_COLS:WB_START], ((0, 0), (0, LANES - 2 * SSM_HEADS))).astype(BF16)
    wb = w_in[:, WB_START:WB_START + WB_COLS].astype(BF16)
    wk = wb[:, D_ATTN:OFF_V].reshape(D_MODEL, KV_HEADS, 1, ATTN_HEAD_DIM)
    wkd = jnp.broadcast_to(wk, (D_MODEL, KV_HEADS, 2, ATTN_HEAD_DIM)).reshape(D_MODEL, -1)
    return wa, wb, wkd, wdt


def _expand_matrices():
    e = np.zeros((Q, 2, 2 * LANES, D_SSM), np.float32)
    for c in range(Q):
        for d in range(2):
            for h in range(SSM_HEADS):
                r = (c * 2 + d) * SSM_HEADS + h
                e[c, d, r, h * SSM_HEAD_DIM:(h + 1) * SSM_HEAD_DIM] = 1.0
                e[c, d, LANES + r, h * SSM_HEAD_DIM:(h + 1) * SSM_HEAD_DIM] = 1.0
    return e


def _bcast_matrices():
    m = np.zeros((Q, 2 * LANES, SSM_HEADS * CHUNK), np.float32)
    for c in range(Q):
        for h in range(SSM_HEADS):
            r = (c * 2 + 1) * SSM_HEADS + h
            m[c, r, h * CHUNK:(h + 1) * CHUNK] = 1.0
            m[c, LANES + r, h * CHUNK:(h + 1) * CHUNK] = 1.0
    return m


def kernel(x, norm_in_g, w_in, conv_w, conv_b, dt_bias, a_log, d_skip, ssd_norm_g, rel_bias, sink,
           attn_norm_g, w_out, final_norm_g):
    batch, seq, _ = x.shape
    assert w_out.shape[0] == 1 and seq % (Q * CHUNK) == 0 and seq % TM == 0
    x2 = x.reshape(batch * seq, D_MODEL)

    weights = _arrange_w_in(w_in)
    e = jnp.asarray(_expand_matrices(), BF16)
    tril = jnp.asarray(np.tril(np.ones((CHUNK, CHUNK), np.float32)), BF16)
    a_log2 = jnp.tile(a_log.reshape(1, 2 * SSM_HEADS).astype(F32), (1, Q))
    dt_bias_x = jnp.pad(dt_bias.reshape(1, 2 * SSM_HEADS).astype(F32), ((0, 0), (0, LANES - 2 * SSM_HEADS)))
    gains = jnp.concatenate([ssd_norm_g, attn_norm_g]).astype(F32)
    w_out_g = (w_out[0] * gains[:, None]).astype(BF16)
    sink_x = jnp.repeat(sink.astype(F32), BLK).reshape(ATTN_HEADS // 2, 2 * BLK)

    z, xs, bm, cm, dt, q, kd, v, ga = _inproj(
        x2, norm_in_g.reshape(1, D_MODEL), weights, conv_w, conv_b.reshape(1, CONV_CH),
        dt_bias_x, seq)
    y_attn, sf, sb = _attn(rel_bias.astype(F32), _bucket_table_t(), sink_x,
                           q, kd, v, ga, a_log2, e, tril, xs, bm, dt, batch, seq)
    out = _ssd_out(a_log2, d_skip.astype(F32), jnp.asarray(_bcast_matrices(), BF16),
                   jnp.stack([tril, tril.T]), xs, bm, cm, dt, z, sf, sb, y_attn, x2, w_out_g,
                   final_norm_g.reshape(1, D_MODEL))
    return out.reshape(batch, seq, D_MODEL)
```

```python
import math

import numpy as np
import jax
import jax.numpy as jnp
from jax import lax
from jax.experimental import pallas as pl
from jax.experimental.pallas import tpu as pltpu

F32 = jnp.float32
BF16 = jnp.bfloat16

D_MODEL = 1024
D_SSM = 1024
D_ATTN = 1024
D_MIX = D_SSM + D_ATTN
SSM_HEAD_DIM = 64
SSM_HEADS = D_SSM // SSM_HEAD_DIM
SSM_GROUPS = 2
GROUP_W = D_SSM // SSM_GROUPS
D_STATE = 128
D_CONV = 5
CHUNK = 128
CONV_CH = D_SSM + 2 * SSM_GROUPS * D_STATE
ATTN_HEAD_DIM = 64
ATTN_HEADS = D_ATTN // ATTN_HEAD_DIM
KV_HEADS = 4
WINDOW = 128
BLK = 128
NUM_BUCKETS = 32
MAX_DISTANCE = 128
MAX_EXACT = 8
EPS = 1e-6
NEG_INF = -1e30
SCALE = ATTN_HEAD_DIM ** -0.5
LOG2E = math.log2(math.e)

LANES = 128
HALO = 8
TM = 512
CONV_ROWS = 64
ATTN_NB = 4
Q = 4
VMEM_LIMIT = 56 * 1024 * 1024

WA_COLS = D_SSM + CONV_CH
OFF_Z, OFF_XBC = 0, D_SSM
WB_START = WA_COLS + 2 * SSM_HEADS
OFF_Q = 0
OFF_V = D_ATTN + KV_HEADS * ATTN_HEAD_DIM
OFF_GA = OFF_V + KV_HEADS * ATTN_HEAD_DIM
WB_COLS = OFF_GA + D_ATTN
assert TM == Q * CHUNK and Q * 2 * SSM_HEADS == LANES


def _silu(v):
    return v * jax.nn.sigmoid(v)


def _softplus(v):
    return jnp.maximum(v, 0.0) + jnp.log1p(jnp.exp(-jnp.abs(v)))


def _inproj_kernel(tiles_per_seq, x_ref, g_ref, wa_ref, wb_ref, wkd_ref, wdt_ref, cw_ref, cb_ref, dtb_ref,
                   z_ref, xs_ref, bm_ref, cm_ref, dt_ref, q_ref, kd_ref, v_ref, ga_ref, pad_ref):
    i = pl.program_id(0)

    @pl.when(i == 0)
    def _():
        pad_ref[...] = jnp.zeros_like(pad_ref)

    j = lax.rem(i, tiles_per_seq)
    g = g_ref[...]

    def proj(hv, w_ref, off, n):
        return jnp.dot(hv, w_ref[:, off:off + n], preferred_element_type=F32)

    xv = x_ref[...]
    ms = jnp.mean(xv * xv, axis=-1, keepdims=True)
    h = (xv * lax.rsqrt(ms + EPS) * g).astype(BF16)

    s_cur = lax.rem(i, 3)
    s_prev = lax.rem(i + 2, 3)
    s_next = lax.rem(i + 1, 3)
    xbc = proj(h, wa_ref, OFF_XBC, CONV_CH)
    pad_ref[s_cur, HALO:HALO + TM, :] = xbc
    pad_ref[s_prev, HALO + TM:2 * HALO + TM, :] = xbc[0:HALO] * jnp.where(j > 0, 1.0, 0.0)
    pad_ref[s_next, 0:HALO, :] = xbc[TM - HALO:TM] * jnp.where(j < tiles_per_seq - 1, 1.0, 0.0)

    z_ref[...] = _silu(proj(h, wa_ref, OFF_Z, D_SSM)).astype(BF16)
    q_ref[...] = (proj(h, wb_ref, OFF_Q, D_ATTN) * (SCALE * LOG2E)).astype(BF16)
    kd_ref[...] = proj(h, wkd_ref, 0, 2 * KV_HEADS * ATTN_HEAD_DIM).astype(BF16)
    v_ref[...] = proj(h, wb_ref, OFF_V, KV_HEADS * ATTN_HEAD_DIM).astype(BF16)
    ga_ref[...] = _silu(proj(h, wb_ref, OFF_GA, D_ATTN)).astype(BF16)
    dt_lane = lax.broadcasted_iota(jnp.int32, (TM, LANES), 1) < 2 * SSM_HEADS
    dt = jnp.where(dt_lane, _softplus(proj(h, wdt_ref, 0, LANES) + dtb_ref[...]), 0.0)
    dtc = dt[0:CHUNK]
    for c in range(1, Q):
        dtc = dtc + pltpu.roll(dt[c * CHUNK:(c + 1) * CHUNK], 2 * SSM_HEADS * c, axis=1)
    dt_ref[...] = dtc

    base = HALO - (D_CONV - 1) // 2
    for cc in range(CONV_CH // LANES):
        sl = slice(cc * LANES, (cc + 1) * LANES)
        if cc < D_SSM // LANES:
            dst, o = xs_ref, cc * LANES
        elif cc < (D_SSM + SSM_GROUPS * D_STATE) // LANES:
            dst, o = bm_ref, cc * LANES - D_SSM
        else:
            dst, o = cm_ref, cc * LANES - D_SSM - SSM_GROUPS * D_STATE
        for r0 in range(0, TM, CONV_ROWS):
            acc = cb_ref[:, sl] + cw_ref[0:1, sl] * pad_ref[s_prev, base + r0:base + r0 + CONV_ROWS, sl]
            for k in range(1, D_CONV):
                acc = acc + cw_ref[k:k + 1, sl] * pad_ref[s_prev, base + r0 + k:base + r0 + k + CONV_ROWS, sl]
            dst[r0:r0 + CONV_ROWS, o:o + LANES] = _silu(acc).astype(BF16)


def _inproj(x2, norm_in_g, weights, conv_w, conv_b, dt_bias, seq):
    n = x2.shape[0]
    tiles_per_seq = seq // TM
    ntile = n // TM
    cur = lambda i: (jnp.minimum(i, ntile - 1), 0)
    lag = lambda i: (jnp.maximum(i - 1, 0), 0)
    const = lambda i: (0, 0)
    out_cols = (D_SSM, D_SSM, SSM_GROUPS * D_STATE, SSM_GROUPS * D_STATE,
                D_ATTN, 2 * KV_HEADS * ATTN_HEAD_DIM, KV_HEADS * ATTN_HEAD_DIM, D_ATTN)
    out_maps = (cur, lag, lag, lag, cur, cur, cur, cur)
    out_specs = [pl.BlockSpec((TM, c), m) for c, m in zip(out_cols, out_maps)]
    out_shape = [jax.ShapeDtypeStruct((n, c), BF16) for c in out_cols]
    out_specs.insert(4, pl.BlockSpec((CHUNK, LANES), lambda i: (0, jnp.minimum(i, ntile - 1))))
    out_shape.insert(4, jax.ShapeDtypeStruct((CHUNK, ntile * LANES), F32))
    return pl.pallas_call(
        lambda *refs: _inproj_kernel(tiles_per_seq, *refs),
        grid=(ntile + 1,),
        in_specs=[
            pl.BlockSpec((TM, D_MODEL), cur),
            pl.BlockSpec((1, D_MODEL), const),
            pl.BlockSpec((D_MODEL, WA_COLS), const, pipeline_mode=pl.Buffered(1)),
            pl.BlockSpec((D_MODEL, WB_COLS), const, pipeline_mode=pl.Buffered(1)),
            pl.BlockSpec((D_MODEL, 2 * KV_HEADS * ATTN_HEAD_DIM), const, pipeline_mode=pl.Buffered(1)),
            pl.BlockSpec((D_MODEL, LANES), const, pipeline_mode=pl.Buffered(1)),
            pl.BlockSpec((D_CONV, CONV_CH), const),
            pl.BlockSpec((1, CONV_CH), const),
            pl.BlockSpec((1, LANES), const),
        ],
        out_specs=out_specs,
        out_shape=out_shape,
        scratch_shapes=[pltpu.VMEM((3, TM + 2 * HALO, CONV_CH), F32)],
        compiler_params=pltpu.CompilerParams(dimension_semantics=("arbitrary",),
                                             vmem_limit_bytes=VMEM_LIMIT),
        name="inproj",
    )(x2, norm_in_g, *weights, conv_w, conv_b, dt_bias)


def _split_terms(v, n):
    terms, r = [], v
    for _ in range(n):
        t = r.astype(BF16)
        terms.append(t)
        r = r - t.astype(F32)
    return terms


def _chunk_cumsum(tril, a):
    t0, t1, t2 = _split_terms(a, 3)
    return (jnp.dot(tril, t0, preferred_element_type=F32) + jnp.dot(tril, t1, preferred_element_type=F32)
            + jnp.dot(tril, t2, preferred_element_type=F32))


def _hi_lo(v):
    return jnp.concatenate(_split_terms(v, 2), axis=1)


def _fwd_lanes():
    lane = lax.broadcasted_iota(jnp.int32, (CHUNK, LANES), 1)
    return jnp.bitwise_and(lane, 2 * SSM_HEADS - 1) < SSM_HEADS


def _states_units(alog_ref, e_ref, tril_ref, xsf_ref, bf_ref, dtf_ref, xsb_ref, bb_ref, dtb_ref,
                  sf_ref, sb_ref, cf_ref, cb_ref):
    a_neg = -jnp.exp(alog_ref[...])
    tril = tril_ref[...]
    fwd_lane = _fwd_lanes()

    def prep(dt_ref):
        dt = dt_ref[...]
        a = dt * a_neg
        cs = _chunk_cumsum(tril, a)
        tot = cs[CHUNK - 1:CHUNK, :]
        w = dt * jnp.exp(jnp.where(fwd_lane, tot - cs, cs - a))
        return _hi_lo(jnp.concatenate([w, jnp.broadcast_to(jnp.exp(tot), (8, LANES))], axis=0))

    def chunk(c, d, cat, xs_ref, b_ref, out_ref, carry_ref):
        rows = slice(c * CHUNK, (c + 1) * CHUNK)
        wx = jnp.dot(cat, e_ref[c, d], preferred_element_type=F32)
        xd = (xs_ref[rows, :].astype(F32) * wx[0:CHUNK]).astype(BF16)
        dec = wx[CHUNK:CHUNK + 1, :]
        for g in range(SSM_GROUPS):
            gr = slice(g * D_STATE, (g + 1) * D_STATE)
            gc = slice(g * GROUP_W, (g + 1) * GROUP_W)
            st = lax.dot_general(b_ref[rows, gr], xd[:, gc], (((0,), (0,)), ((), ())),
                                 preferred_element_type=F32)
            prev = carry_ref[gr, :]
            out_ref[c, gr, :] = prev.astype(BF16)
            carry_ref[gr, :] = prev * dec[:, gc] + st

    cat_f = prep(dtf_ref)
    cat_b = prep(dtb_ref)
    units = []
    for c in range(Q):
        units.append(lambda c=c: chunk(c, 0, cat_f, xsf_ref, bf_ref, sf_ref, cf_ref))
        units.append(lambda c=c: chunk(Q - 1 - c, 1, cat_b, xsb_ref, bb_ref, sb_ref, cb_ref))
    return units


def _ssd_out_kernel(alog_ref, dskip_ref, selb_ref, tril_ref, xs_ref, bm_ref, cm_ref, dt_ref,
                    z_ref, sf_ref, sb_ref, ya_ref, x_ref, w_ref, fg_ref, o_ref, ys_ref, vs_ref):
    li = lax.broadcasted_iota(jnp.int32, (CHUNK, CHUNK), 0)
    si = lax.broadcasted_iota(jnp.int32, (CHUNK, CHUNK), 1)
    lower = li >= si
    eye = li == si
    eyeb = jnp.where(eye, 1.0, 0.0).astype(BF16)
    left = lax.broadcasted_iota(jnp.int32, (CHUNK, LANES), 1) < SSM_HEAD_DIM

    dt = dt_ref[...]
    a = dt * (-LOG2E * jnp.exp(alog_ref[...]))
    colq = jnp.where(_fwd_lanes(), _chunk_cumsum(tril_ref[0], a), _chunk_cumsum(tril_ref[1], a))
    rowq = (colq - jnp.log2(dt)).T
    dtr = dt.T
    catq = _hi_lo(colq)

    def outproj(rows):
        acc = (jnp.dot(ys_ref[rows, :], w_ref[0:D_SSM, :], preferred_element_type=F32)
               + jnp.dot(ya_ref[rows, :], w_ref[D_SSM:D_MIX, :], preferred_element_type=F32))
        hres = x_ref[rows, :] + acc
        ms = jnp.mean(hres * hres, axis=-1, keepdims=True)
        o_ref[rows, :] = hres * lax.rsqrt(ms + EPS) * fg_ref[...]

    pending = None
    for c in range(Q):
        rows = slice(c * CHUNK, (c + 1) * CHUNK)
        for g in range(SSM_GROUPS):
            gr = slice(g * D_STATE, (g + 1) * D_STATE)
            cmg = cm_ref[rows, gr]
            cb = lax.dot_general(cmg, bm_ref[rows, gr], (((1,), (1,)), ((), ())),
                                 preferred_element_type=F32)
            cboff = jnp.where(eye, 0.0, cb).astype(BF16)
            cbdiag = jnp.sum(jnp.where(eye, cb, 0.0), axis=0, keepdims=True)
            ssq = jnp.zeros((CHUNK, 1), F32)
            for pp in range(GROUP_W // LANES):
                hd0 = g * (SSM_HEADS // SSM_GROUPS) + pp * 2
                cols = slice(hd0 * SSM_HEAD_DIM, (hd0 + 2) * SSM_HEAD_DIM)
                if pp % 2 == 0:
                    c2 = slice(pp * LANES, (pp + 2) * LANES)
                    csf = jnp.dot(cmg, sf_ref[c, gr, c2], preferred_element_type=F32)
                    csb = jnp.dot(cmg, sb_ref[c, gr, c2], preferred_element_type=F32)
                pc = slice((pp % 2) * LANES, (pp % 2 + 1) * LANES)
                colb = jnp.dot(catq, selb_ref[c, :, hd0 * CHUNK:(hd0 + 2) * CHUNK],
                               preferred_element_type=F32)
                scores, colf_bc, colb_bc = [], [], []
                for hh in range(2):
                    h = c * 2 * SSM_HEADS + hd0 + hh
                    hb = h + SSM_HEADS
                    colf_bc.append(jnp.broadcast_to(colq[:, h:h + 1], (CHUNK, CHUNK)))
                    colb_bc.append(colb[:, hh * CHUNK:(hh + 1) * CHUNK])
                    arg = jnp.where(lower, colf_bc[hh] - rowq[h:h + 1, :], colb_bc[hh] - rowq[hb:hb + 1, :])
                    diag = (cbdiag * (dtr[h:h + 1, :] + dtr[hb:hb + 1, :]) + dskip_ref[hd0 + hh]).astype(BF16)
                    scores.append(cboff * jnp.exp2(arg).astype(BF16) + eyeb * diag)
                xt = xs_ref[rows, cols]
                zero = jnp.zeros_like(xt)
                rhs = jnp.concatenate([jnp.where(left, xt, zero), jnp.where(left, zero, xt)], axis=0)
                y_diag = jnp.dot(jnp.concatenate(scores, axis=1), rhs, preferred_element_type=F32)
                y_off = (csf[:, pc] * jnp.exp2(jnp.where(left, colf_bc[0], colf_bc[1]))
                         + csb[:, pc] * jnp.exp2(jnp.where(left, colb_bc[0], colb_bc[1])))
                vg = (y_diag + y_off) * z_ref[rows, cols].astype(F32)
                vs_ref[rows, cols] = vg
                ssq = ssq + jnp.sum(vg * vg, axis=-1, keepdims=True)
                if g == 0 and pp == 1 and pending is not None:
                    pending()
            gc = slice(g * GROUP_W, (g + 1) * GROUP_W)
            ys_ref[rows, gc] = (vs_ref[rows, gc] * lax.rsqrt(ssq * (1.0 / GROUP_W) + EPS)).astype(BF16)

        pending = lambda rows=rows: outproj(rows)
    pending()


def _ssd_out(a_log, d_skip, selb, tril, xs, bm, cm, dt, z, sf, sb, y_attn, x2, w_out, final_norm_g):
    n = xs.shape[0]
    rows = Q * CHUNK
    row = lambda i: (i, 0)
    const = lambda i: (0, 0)
    return pl.pallas_call(
        _ssd_out_kernel,
        grid=(n // rows,),
        in_specs=[
            pl.BlockSpec((1, LANES), const),
            pl.BlockSpec(memory_space=pltpu.SMEM),
            pl.BlockSpec((Q, 2 * LANES, SSM_HEADS * CHUNK), lambda i: (0, 0, 0), pipeline_mode=pl.Buffered(1)),
            pl.BlockSpec((2, CHUNK, CHUNK), lambda i: (0, 0, 0)),
            pl.BlockSpec((rows, D_SSM), row),
            pl.BlockSpec((rows, SSM_GROUPS * D_STATE), row),
            pl.BlockSpec((rows, SSM_GROUPS * D_STATE), row),
            pl.BlockSpec((CHUNK, LANES), lambda i: (0, i)),
            pl.BlockSpec((rows, D_SSM), row),
            pl.BlockSpec((Q, SSM_GROUPS * D_STATE, GROUP_W), lambda i: (i, 0, 0)),
            pl.BlockSpec((Q, SSM_GROUPS * D_STATE, GROUP_W), lambda i: (i, 0, 0)),
            pl.BlockSpec((rows, D_ATTN), row),
            pl.BlockSpec((rows, D_MODEL), row),
            pl.BlockSpec((D_MIX, D_MODEL), const, pipeline_mode=pl.Buffered(1)),
            pl.BlockSpec((1, D_MODEL), const),
        ],
        out_specs=pl.BlockSpec((rows, D_MODEL), row),
        out_shape=jax.ShapeDtypeStruct((n, D_MODEL), F32),
        scratch_shapes=[pltpu.VMEM((rows, D_SSM), BF16), pltpu.VMEM((rows, D_SSM), F32)],
        compiler_params=pltpu.CompilerParams(dimension_semantics=("parallel",),
                                             vmem_limit_bytes=VMEM_LIMIT),
        name="ssd_out",
    )(a_log, d_skip, selb, tril, xs, bm, cm, dt, z, sf, sb, y_attn, x2, w_out, final_norm_g)


def _bucket_table_t():
    rel = jnp.arange(3 * BLK)[:, None] - BLK - jnp.arange(BLK)[None, :]
    half = NUM_BUCKETS // 2
    ret = (rel > 0).astype(jnp.int32) * half
    n = jnp.abs(rel)
    nf = jnp.maximum(n, 1).astype(F32)
    large = MAX_EXACT + (jnp.log(nf / MAX_EXACT) / math.log(MAX_DISTANCE / MAX_EXACT)
                         * (half - MAX_EXACT)).astype(jnp.int32)
    large = jnp.minimum(large, half - 1)
    return (ret + jnp.where(n < MAX_EXACT, n, large)).astype(jnp.int32)


def _attn_kernel(nblk, nstep, rb_ref, bucket_ref, sink_ref, q_ref, kp_ref, kc_ref, kn_ref,
                 vp_ref, vc_ref, vn_ref, ga_ref,
                 alog_ref, e_ref, tril_ref, xsf_ref, bf_ref, dtf_ref, xsb_ref, bb_ref, dtb_ref,
                 y_ref, sf_ref, sb_ref,
                 tab_ref, pe_ref, mk_ref, ot_ref, vs_ref, cf_ref, cb_ref):
    g = pl.program_id(0)

    @pl.when((g == 0) | (lax.rem(g + (nblk // ATTN_NB) - 2, nblk // ATTN_NB) == 0))
    def _():
        cf_ref[...] = jnp.zeros_like(cf_ref)
        cb_ref[...] = jnp.zeros_like(cb_ref)

    @pl.when(g == 0)
    def _init():
        pe_ref[...] = jnp.zeros_like(pe_ref)
        mk_ref[...] = jnp.zeros_like(mk_ref)
        ot_ref[...] = jnp.zeros_like(ot_ref)
        bk = bucket_ref[...]
        t = lax.broadcasted_iota(jnp.int32, (3 * BLK, BLK), 0)
        qq = lax.broadcasted_iota(jnp.int32, (3 * BLK, BLK), 1)
        in_window = jnp.abs(t - BLK - qq) <= WINDOW

        def per_head(h, carry):
            acc = jnp.zeros((3 * BLK, BLK), F32)
            for b in range(NUM_BUCKETS):
                acc = jnp.where(bk == b, rb_ref[b, h], acc)
            tab = jnp.where(in_window, acc * LOG2E, NEG_INF)
            tab_ref[0, h] = tab
            tab_ref[1, h] = jnp.where(t >= BLK, tab, NEG_INF)
            tab_ref[2, h] = jnp.where(t < 2 * BLK, tab, NEG_INF)
            return carry

        lax.fori_loop(0, ATTN_HEADS, per_head, 0)

    left = lax.broadcasted_iota(jnp.int32, (BLK, LANES), 1) < ATTN_HEAD_DIM
    left3 = lax.broadcasted_iota(jnp.int32, (3 * BLK, LANES), 1) < ATTN_HEAD_DIM
    s2 = sink_ref[...] * LOG2E

    def run(even, odd):
        first_blk = jnp.minimum(g, nstep - 1) * ATTN_NB
        kall = jnp.concatenate([kp_ref[...], kc_ref[...], kn_ref[...]], axis=0)
        vall = jnp.concatenate([vp_ref[...], vc_ref[...], vn_ref[...]], axis=0)
        one = jnp.ones((3 * BLK, LANES), BF16)

        def stage_a(j, p, var):
            kd = kall[j * BLK:(j + 3) * BLK, (p // 2) * LANES:(p // 2 + 1) * LANES]
            qt = q_ref[j * BLK:(j + 1) * BLK, p * LANES:(p + 1) * LANES]
            zero = jnp.zeros_like(qt)
            rhs_t = jnp.concatenate([jnp.where(left, qt, zero), jnp.where(left, zero, qt)], axis=0)
            lg = lax.dot_general(kd, rhs_t, (((1,), (1,)), ((), ())), preferred_element_type=F32)
            lg = lg + jnp.concatenate([tab_ref[var, 2 * p], tab_ref[var, 2 * p + 1]], axis=1)
            for kb in range(3):
                blk = lg[kb * BLK:(kb + 1) * BLK]
                mk = jnp.max(blk, axis=0, keepdims=True)
                pe_ref[even, j, p, kb * BLK:(kb + 1) * BLK, :] = jnp.exp2(blk - mk).astype(BF16)
                mk_ref[even, j, p, kb:kb + 1, :] = mk

        def stage_b(j, p):
            kv = p // 2
            vt = vall[j * BLK:(j + 3) * BLK, (kv // 2) * LANES:(kv // 2 + 1) * LANES]
            vmod = jnp.where(left3, vt, one) if kv % 2 == 0 else jnp.where(left3, one, vt)
            r0 = (kv % 2) * ATTN_HEAD_DIM
            d0 = ATTN_HEAD_DIM - r0
            mk = mk_ref[odd, j, p]
            sp = s2[p:p + 1, :]
            m = jnp.maximum(jnp.max(mk[0:3], axis=0, keepdims=True), sp)
            resc = jnp.exp2(mk[0:3] - m).astype(BF16)
            pes = jnp.concatenate([pe_ref[odd, j, p, kb * BLK:(kb + 1) * BLK, :] * resc[kb:kb + 1]
                                   for kb in range(3)], axis=0)
            o2 = lax.dot_general(vmod, pes, (((0,), (0,)), ((), ())), preferred_element_type=F32)
            o = o2[r0:r0 + ATTN_HEAD_DIM, :] / (o2[d0:d0 + 1, :] + jnp.exp2(sp - m))
            ot_ref[odd, j, 2 * p * ATTN_HEAD_DIM:(2 * p + 1) * ATTN_HEAD_DIM, :] = o[:, 0:BLK]
            ot_ref[odd, j, (2 * p + 1) * ATTN_HEAD_DIM:(2 * p + 2) * ATTN_HEAD_DIM, :] = o[:, BLK:2 * BLK]

        def stage_c(j, p, ssq):
            rows = slice(j * BLK, (j + 1) * BLK)
            cols = slice(p * LANES, (p + 1) * LANES)
            v = ot_ref[even, j, p * LANES:(p + 1) * LANES, :].T * ga_ref[rows, cols].astype(F32)
            vs_ref[rows, cols] = v
            return ssq + jnp.sum(v * v, axis=-1, keepdims=True)

        scan_units = _states_units(alog_ref, e_ref, tril_ref, xsf_ref, bf_ref, dtf_ref, xsb_ref, bb_ref,
                                   dtb_ref, sf_ref, sb_ref, cf_ref, cb_ref)
        per_block = len(scan_units) // ATTN_NB
        npair = ATTN_HEADS // 2

        for j in range(ATTN_NB):
            pos = lax.rem(first_blk + j, nblk)
            var = jnp.where(pos == 0, 1, jnp.where(pos == nblk - 1, 2, 0))
            rows = slice(j * BLK, (j + 1) * BLK)
            ssq = jnp.zeros((BLK, 1), F32)
            for p in range(ATTN_HEADS // 2):
                stage_a(j, p, var)
                stage_b(j, p)
                ssq = stage_c(j, p, ssq)
                if p % (npair // per_block) == 2:
                    scan_units.pop(0)()
            y_ref[rows, :] = (vs_ref[rows, :] * lax.rsqrt(ssq * (1.0 / D_ATTN) + EPS)).astype(BF16)

    @pl.when(lax.rem(g, 2) == 0)
    def _():
        run(0, 1)

    @pl.when(lax.rem(g, 2) == 1)
    def _():
        run(1, 0)


def _attn(rel_bias, bucket_t, sink_x, q, kd, v, ga, a_log, e, tril, xs, bm, dt, batch, seq):
    n = q.shape[0]
    nblk = seq // BLK
    assert nblk % ATTN_NB == 0 and ATTN_NB * BLK == Q * CHUNK
    nstep = batch * nblk // ATTN_NB
    per_seq = nblk // ATTN_NB
    nchunk = batch * seq // CHUNK

    def mirrored(lag):
        def index(g):
            s = step(lag)(g)
            return s - lax.rem(s, per_seq) + per_seq - 1 - lax.rem(s, per_seq)
        return index

    def step(lag):
        return lambda g: jnp.clip(g - lag, 0, nstep - 1)

    def cur(lag):
        return lambda g: (step(lag)(g), 0)

    def prev(lag):
        def index(g):
            b = step(lag)(g) * ATTN_NB
            return (b - jnp.where(lax.rem(b, nblk) > 0, 1, 0), 0)
        return index

    def nxt(lag):
        def index(g):
            b = step(lag)(g) * ATTN_NB + ATTN_NB - 1
            return (b + jnp.where(lax.rem(b, nblk) < nblk - 1, 1, 0), 0)
        return index

    const = lambda g: (0, 0)
    kw = 2 * KV_HEADS * ATTN_HEAD_DIM
    vw = KV_HEADS * ATTN_HEAD_DIM
    npair = ATTN_HEADS // 2
    rows = ATTN_NB * BLK
    st_shape = jax.ShapeDtypeStruct((nchunk, SSM_GROUPS * D_STATE, GROUP_W), BF16)
    st_block = (Q, SSM_GROUPS * D_STATE, GROUP_W)
    return pl.pallas_call(
        lambda *refs: _attn_kernel(nblk, nstep, *refs),
        grid=(nstep + 2,),
        in_specs=[
            pl.BlockSpec(memory_space=pltpu.SMEM),
            pl.BlockSpec((3 * BLK, BLK), const),
            pl.BlockSpec((npair, 2 * BLK), const),
            pl.BlockSpec((rows, D_ATTN), cur(0)),
            pl.BlockSpec((BLK, kw), prev(0)), pl.BlockSpec((rows, kw), cur(0)), pl.BlockSpec((BLK, kw), nxt(0)),
            pl.BlockSpec((BLK, vw), prev(1)), pl.BlockSpec((rows, vw), cur(1)), pl.BlockSpec((BLK, vw), nxt(1)),
            pl.BlockSpec((rows, D_ATTN), cur(2)),
            pl.BlockSpec((1, LANES), const),
            pl.BlockSpec((Q, 2, 2 * LANES, D_SSM), lambda g: (0, 0, 0, 0), pipeline_mode=pl.Buffered(1)),
            pl.BlockSpec((CHUNK, CHUNK), const),
            pl.BlockSpec((rows, D_SSM), cur(2)),
            pl.BlockSpec((rows, SSM_GROUPS * D_STATE), cur(2)),
            pl.BlockSpec((CHUNK, LANES), lambda g: (0, step(2)(g))),
            pl.BlockSpec((rows, D_SSM), lambda g: (mirrored(2)(g), 0)),
            pl.BlockSpec((rows, SSM_GROUPS * D_STATE), lambda g: (mirrored(2)(g), 0)),
            pl.BlockSpec((CHUNK, LANES), lambda g: (0, mirrored(2)(g))),
        ],
        out_specs=[pl.BlockSpec((rows, D_ATTN), cur(2)),
                   pl.BlockSpec(st_block, lambda g: (step(2)(g), 0, 0)),
                   pl.BlockSpec(st_block, lambda g: (mirrored(2)(g), 0, 0))],
        out_shape=[jax.ShapeDtypeStruct((n, D_ATTN), BF16), st_shape, st_shape],
        scratch_shapes=[pltpu.VMEM((3, ATTN_HEADS, 3 * BLK, BLK), F32),
                        pltpu.VMEM((2, ATTN_NB, npair, 3 * BLK, 2 * BLK), BF16),
                        pltpu.VMEM((2, ATTN_NB, npair, 8, 2 * BLK), F32),
                        pltpu.VMEM((2, ATTN_NB, D_ATTN, BLK), F32),
                        pltpu.VMEM((rows, D_ATTN), F32),
                        pltpu.VMEM((SSM_GROUPS * D_STATE, GROUP_W), F32),
                        pltpu.VMEM((SSM_GROUPS * D_STATE, GROUP_W), F32)],
        compiler_params=pltpu.CompilerParams(dimension_semantics=("arbitrary",),
                                             vmem_limit_bytes=VMEM_LIMIT),
        name="attn_scan",
    )(rel_bias, bucket_t, sink_x, q, kd, kd, kd, v, v, v, ga,
      a_log, e, tril, xs, bm, dt, xs, bm, dt)


def _arrange_w_in(w_in):
    wa = w_in[:, 0:WA_COLS].astype(BF16)
    wdt = jnp.pad(w_in[:, WA_COLS:WB_START], ((0, 0), (0, LANES - 2 * SSM_HEADS))).astype(BF16)
    wb = w_in[:, WB_START:WB_START + WB_COLS].astype(BF16)
    wk = wb[:, D_ATTN:OFF_V].reshape(D_MODEL, KV_HEADS, 1, ATTN_HEAD_DIM)
    wkd = jnp.broadcast_to(wk, (D_MODEL, KV_HEADS, 2, ATTN_HEAD_DIM)).reshape(D_MODEL, -1)
    return wa, wb, wkd, wdt


def _expand_matrices():
    e = np.zeros((Q, 2, 2 * LANES, D_SSM), np.float32)
    for c in range(Q):
        for d in range(2):
            for h in range(SSM_HEADS):
                r = (c * 2 + d) * SSM_HEADS + h
                e[c, d, r, h * SSM_HEAD_DIM:(h + 1) * SSM_HEAD_DIM] = 1.0
                e[c, d, LANES + r, h * SSM_HEAD_DIM:(h + 1) * SSM_HEAD_DIM] = 1.0
    return e


def _bcast_matrices():
    m = np.zeros((Q, 2 * LANES, SSM_HEADS * CHUNK), np.float32)
    for c in range(Q):
        for h in range(SSM_HEADS):
            r = (c * 2 + 1) * SSM_HEADS + h
            m[c, r, h * CHUNK:(h + 1) * CHUNK] = 1.0
            m[c, LANES + r, h * CHUNK:(h + 1) * CHUNK] = 1.0
    return m


def kernel(x, norm_in_g, w_in, conv_w, conv_b, dt_bias, a_log, d_skip, ssd_norm_g, rel_bias, sink,
           attn_norm_g, w_out, final_norm_g):
    batch, seq, _ = x.shape
    assert w_out.shape[0] == 1 and seq % (Q * CHUNK) == 0 and seq % TM == 0
    x2 = x.reshape(batch * seq, D_MODEL)

    weights = _arrange_w_in(w_in)
    e = jnp.asarray(_expand_matrices(), BF16)
    tril = jnp.asarray(np.tril(np.ones((CHUNK, CHUNK), np.float32)), BF16)
    a_log2 = jnp.tile(a_log.reshape(1, 2 * SSM_HEADS).astype(F32), (1, Q))
    dt_bias_x = jnp.pad(dt_bias.reshape(1, 2 * SSM_HEADS).astype(F32), ((0, 0), (0, LANES - 2 * SSM_HEADS)))
    gains = jnp.concatenate([ssd_norm_g, attn_norm_g]).astype(F32)
    w_out_g = (w_out[0] * gains[:, None]).astype(BF16)
    sink_x = jnp.repeat(sink.astype(F32), BLK).reshape(ATTN_HEADS // 2, 2 * BLK)

    z, xs, bm, cm, dt, q, kd, v, ga = _inproj(
        x2, norm_in_g.reshape(1, D_MODEL), weights, conv_w, conv_b.reshape(1, CONV_CH),
        dt_bias_x, seq)
    y_attn, sf, sb = _attn(rel_bias.astype(F32), _bucket_table_t(), sink_x,
                           q, kd, v, ga, a_log2, e, tril, xs, bm, dt, batch, seq)
    out = _ssd_out(a_log2, d_skip.astype(F32), jnp.asarray(_bcast_matrices(), BF16),
                   jnp.stack([tril, tril.T]), xs, bm, cm, dt, z, sf, sb, y_attn, x2, w_out_g,
                   final_norm_g.reshape(1, D_MODEL))
    return out.reshape(batch, seq, D_MODEL)
```

```python
import math

import numpy as np
import jax
import jax.numpy as jnp
from jax import lax
from jax.experimental import pallas as pl
from jax.experimental.pallas import tpu as pltpu

F32 = jnp.float32
BF16 = jnp.bfloat16

D_MODEL = 1024
D_SSM = 1024
D_ATTN = 1024
D_MIX = D_SSM + D_ATTN
SSM_HEAD_DIM = 64
SSM_HEADS = D_SSM // SSM_HEAD_DIM
SSM_GROUPS = 2
GROUP_W = D_SSM // SSM_GROUPS
D_STATE = 128
D_CONV = 5
CHUNK = 128
CONV_CH = D_SSM + 2 * SSM_GROUPS * D_STATE
ATTN_HEAD_DIM = 64
ATTN_HEADS = D_ATTN // ATTN_HEAD_DIM
KV_HEADS = 4
WINDOW = 128
BLK = 128
NUM_BUCKETS = 32
MAX_DISTANCE = 128
MAX_EXACT = 8
EPS = 1e-6
NEG_INF = -1e30
SCALE = ATTN_HEAD_DIM ** -0.5
LOG2E = math.log2(math.e)

LANES = 128
HALO = 8
TM = 512
CONV_ROWS = 64
ATTN_NB = 4
Q = 4
VMEM_LIMIT = 56 * 1024 * 1024

WA_COLS = D_SSM + CONV_CH
OFF_Z, OFF_XBC = 0, D_SSM
WB_START = WA_COLS + 2 * SSM_HEADS
OFF_Q = 0
OFF_V = D_ATTN + KV_HEADS * ATTN_HEAD_DIM
OFF_GA = OFF_V + KV_HEADS * ATTN_HEAD_DIM
WB_COLS = OFF_GA + D_ATTN
assert TM == Q * CHUNK and Q * 2 * SSM_HEADS == LANES


def _silu(v):
    return v * jax.nn.sigmoid(v)


def _softplus(v):
    return jnp.maximum(v, 0.0) + jnp.log1p(jnp.exp(-jnp.abs(v)))


def _inproj_kernel(tiles_per_seq, x_ref, g_ref, wa_ref, wb_ref, wkd_ref, wdt_ref, cw_ref, cb_ref, dtb_ref,
                   z_ref, xs_ref, bm_ref, cm_ref, dt_ref, q_ref, kd_ref, v_ref, ga_ref, pad_ref):
    i = pl.program_id(0)

    @pl.when(i == 0)
    def _():
        pad_ref[...] = jnp.zeros_like(pad_ref)

    j = lax.rem(i, tiles_per_seq)
    g = g_ref[...]

    def proj(hv, w_ref, off, n):
        return jnp.dot(hv, w_ref[:, off:off + n], preferred_element_type=F32)

    xv = x_ref[...]
    ms = jnp.mean(xv * xv, axis=-1, keepdims=True)
    h = (xv * lax.rsqrt(ms + EPS) * g).astype(BF16)

    s_cur = lax.rem(i, 3)
    s_prev = lax.rem(i + 2, 3)
    s_next = lax.rem(i + 1, 3)
    xbc = proj(h, wa_ref, OFF_XBC, CONV_CH)
    pad_ref[s_cur, HALO:HALO + TM, :] = xbc
    pad_ref[s_prev, HALO + TM:2 * HALO + TM, :] = xbc[0:HALO] * jnp.where(j > 0, 1.0, 0.0)
    pad_ref[s_next, 0:HALO, :] = xbc[TM - HALO:TM] * jnp.where(j < tiles_per_seq - 1, 1.0, 0.0)

    z_ref[...] = _silu(proj(h, wa_ref, OFF_Z, D_SSM)).astype(BF16)
    q_ref[...] = (proj(h, wb_ref, OFF_Q, D_ATTN) * (SCALE * LOG2E)).astype(BF16)
    kd_ref[...] = proj(h, wkd_ref, 0, 2 * KV_HEADS * ATTN_HEAD_DIM).astype(BF16)
    v_ref[...] = proj(h, wb_ref, OFF_V, KV_HEADS * ATTN_HEAD_DIM).astype(BF16)
    ga_ref[...] = _silu(proj(h, wb_ref, OFF_GA, D_ATTN)).astype(BF16)
    dt_lane = lax.broadcasted_iota(jnp.int32, (TM, LANES), 1) < 2 * SSM_HEADS
    dt = jnp.where(dt_lane, _softplus(proj(h, wdt_ref, 0, LANES) + dtb_ref[...]), 0.0)
    dtc = dt[0:CHUNK]
    for c in range(1, Q):
        dtc = dtc + pltpu.roll(dt[c * CHUNK:(c + 1) * CHUNK], 2 * SSM_HEADS * c, axis=1)
    dt_ref[...] = dtc

    base = HALO - (D_CONV - 1) // 2
    for cc in range(CONV_CH // LANES):
        sl = slice(cc * LANES, (cc + 1) * LANES)
        if cc < D_SSM // LANES:
            dst, o = xs_ref, cc * LANES
        elif cc < (D_SSM + SSM_GROUPS * D_STATE) // LANES:
            dst, o = bm_ref, cc * LANES - D_SSM
        else:
            dst, o = cm_ref, cc * LANES - D_SSM - SSM_GROUPS * D_STATE
        for r0 in range(0, TM, CONV_ROWS):
            acc = cb_ref[:, sl] + cw_ref[0:1, sl] * pad_ref[s_prev, base + r0:base + r0 + CONV_ROWS, sl]
            for k in range(1, D_CONV):
                acc = acc + cw_ref[k:k + 1, sl] * pad_ref[s_prev, base + r0 + k:base + r0 + k + CONV_ROWS, sl]
            dst[r0:r0 + CONV_ROWS, o:o + LANES] = _silu(acc).astype(BF16)


def _inproj(x2, norm_in_g, weights, conv_w, conv_b, dt_bias, seq):
    n = x2.shape[0]
    tiles_per_seq = seq // TM
    ntile = n // TM
    cur = lambda i: (jnp.minimum(i, ntile - 1), 0)
    lag = lambda i: (jnp.maximum(i - 1, 0), 0)
    const = lambda i: (0, 0)
    out_cols = (D_SSM, D_SSM, SSM_GROUPS * D_STATE, SSM_GROUPS * D_STATE,
                D_ATTN, 2 * KV_HEADS * ATTN_HEAD_DIM, KV_HEADS * ATTN_HEAD_DIM, D_ATTN)
    out_maps = (cur, lag, lag, lag, cur, cur, cur, cur)
    out_specs = [pl.BlockSpec((TM, c), m) for c, m in zip(out_cols, out_maps)]
    out_shape = [jax.ShapeDtypeStruct((n, c), BF16) for c in out_cols]
    out_specs.insert(4, pl.BlockSpec((CHUNK, LANES), lambda i: (0, jnp.minimum(i, ntile - 1))))
    out_shape.insert(4, jax.ShapeDtypeStruct((CHUNK, ntile * LANES), F32))
    return pl.pallas_call(
        lambda *refs: _inproj_kernel(tiles_per_seq, *refs),
        grid=(ntile + 1,),
        in_specs=[
            pl.BlockSpec((TM, D_MODEL), cur),
            pl.BlockSpec((1, D_MODEL), const),
            pl.BlockSpec((D_MODEL, WA_COLS), const, pipeline_mode=pl.Buffered(1)),
            pl.BlockSpec((D_MODEL, WB_COLS), const, pipeline_mode=pl.Buffered(1)),
            pl.BlockSpec((D_MODEL, 2 * KV_HEADS * ATTN_HEAD_DIM), const, pipeline_mode=pl.Buffered(1)),
            pl.BlockSpec((D_MODEL, LANES), const, pipeline_mode=pl.Buffered(1)),
            pl.BlockSpec((D_CONV, CONV_CH), const),
            pl.BlockSpec((1, CONV_CH), const),
            pl.BlockSpec((1, LANES), const),
        ],
        out_specs=out_specs,
        out_shape=out_shape,
        scratch_shapes=[pltpu.VMEM((3, TM + 2 * HALO, CONV_CH), F32)],
        compiler_params=pltpu.CompilerParams(dimension_semantics=("arbitrary",),
                                             vmem_limit_bytes=VMEM_LIMIT),
        name="inproj",
    )(x2, norm_in_g, *weights, conv_w, conv_b, dt_bias)


def _split_terms(v, n):
    terms, r = [], v
    for _ in range(n):
        t = r.astype(BF16)
        terms.append(t)
        r = r - t.astype(F32)
    return terms


def _chunk_cumsum(tril, a):
    t0, t1, t2 = _split_terms(a, 3)
    return (jnp.dot(tril, t0, preferred_element_type=F32) + jnp.dot(tril, t1, preferred_element_type=F32)
            + jnp.dot(tril, t2, preferred_element_type=F32))


def _hi_lo(v):
    return jnp.concatenate(_split_terms(v, 2), axis=1)


def _fwd_lanes():
    lane = lax.broadcasted_iota(jnp.int32, (CHUNK, LANES), 1)
    return jnp.bitwise_and(lane, 2 * SSM_HEADS - 1) < SSM_HEADS


def _states_units(alog_ref, e_ref, tril_ref, xsf_ref, bf_ref, dtf_ref, xsb_ref, bb_ref, dtb_ref,
                  sf_ref, sb_ref, cf_ref, cb_ref):
    a_neg = -jnp.exp(alog_ref[...])
    tril = tril_ref[...]
    fwd_lane = _fwd_lanes()

    def prep(dt_ref):
        dt = dt_ref[...]
        a = dt * a_neg
        cs = _chunk_cumsum(tril, a)
        tot = cs[CHUNK - 1:CHUNK, :]
        w = dt * jnp.exp(jnp.where(fwd_lane, tot - cs, cs - a))
        return _hi_lo(jnp.concatenate([w, jnp.broadcast_to(jnp.exp(tot), (8, LANES))], axis=0))

    def chunk(c, d, cat, xs_ref, b_ref, out_ref, carry_ref):
        rows = slice(c * CHUNK, (c + 1) * CHUNK)
        wx = jnp.dot(cat, e_ref[c, d], preferred_element_type=F32)
        xd = (xs_ref[rows, :].astype(F32) * wx[0:CHUNK]).astype(BF16)
        dec = wx[CHUNK:CHUNK + 1, :]
        for g in range(SSM_GROUPS):
            gr = slice(g * D_STATE, (g + 1) * D_STATE)
            gc = slice(g * GROUP_W, (g + 1) * GROUP_W)
            st = lax.dot_general(b_ref[rows, gr], xd[:, gc], (((0,), (0,)), ((), ())),
                                 preferred_element_type=F32)
            prev = carry_ref[gr, :]
            out_ref[c, gr, :] = prev.astype(BF16)
            carry_ref[gr, :] = prev * dec[:, gc] + st

    cat_f = prep(dtf_ref)
    cat_b = prep(dtb_ref)
    units = []
    for c in range(Q):
        units.append(lambda c=c: chunk(c, 0, cat_f, xsf_ref, bf_ref, sf_ref, cf_ref))
        units.append(lambda c=c: chunk(Q - 1 - c, 1, cat_b, xsb_ref, bb_ref, sb_ref, cb_ref))
    return units


def _ssd_out_kernel(alog_ref, dskip_ref, selb_ref, tril_ref, xs_ref, bm_ref, cm_ref, dt_ref,
                    z_ref, sf_ref, sb_ref, ya_ref, x_ref, w_ref, fg_ref, o_ref, ys_ref, vs_ref):
    li = lax.broadcasted_iota(jnp.int32, (CHUNK, CHUNK), 0)
    si = lax.broadcasted_iota(jnp.int32, (CHUNK, CHUNK), 1)
    lower = li >= si
    eye = li == si
    eyeb = jnp.where(eye, 1.0, 0.0).astype(BF16)
    left = lax.broadcasted_iota(jnp.int32, (CHUNK, LANES), 1) < SSM_HEAD_DIM

    dt = dt_ref[...]
    a = dt * (-LOG2E * jnp.exp(alog_ref[...]))
    colq = jnp.where(_fwd_lanes(), _chunk_cumsum(tril_ref[0], a), _chunk_cumsum(tril_ref[1], a))
    rowq = (colq - jnp.log2(dt)).T
    dtr = dt.T
    catq = _hi_lo(colq)

    def outproj(rows):
        acc = (jnp.dot(ys_ref[rows, :], w_ref[0:D_SSM, :], preferred_element_type=F32)
               + jnp.dot(ya_ref[rows, :], w_ref[D_SSM:D_MIX, :], preferred_element_type=F32))
        hres = x_ref[rows, :] + acc
        ms = jnp.mean(hres * hres, axis=-1, keepdims=True)
        o_ref[rows, :] = hres * lax.rsqrt(ms + EPS) * fg_ref[...]

    pending = None
    for c in range(Q):
        rows = slice(c * CHUNK, (c + 1) * CHUNK)
        for g in range(SSM_GROUPS):
            gr = slice(g * D_STATE, (g + 1) * D_STATE)
            cmg = cm_ref[rows, gr]
            cb = lax.dot_general(cmg, bm_ref[rows, gr], (((1,), (1,)), ((), ())),
                                 preferred_element_type=F32)
            cboff = jnp.where(eye, 0.0, cb).astype(BF16)
            cbdiag = jnp.sum(jnp.where(eye, cb, 0.0), axis=0, keepdims=True)
            ssq = jnp.zeros((CHUNK, 1), F32)
            for pp in range(GROUP_W // LANES):
                hd0 = g * (SSM_HEADS // SSM_GROUPS) + pp * 2
                cols = slice(hd0 * SSM_HEAD_DIM, (hd0 + 2) * SSM_HEAD_DIM)
                if pp % 2 == 0:
                    c2 = slice(pp * LANES, (pp + 2) * LANES)
                    csf = jnp.dot(cmg, sf_ref[c, gr, c2], preferred_element_type=F32)
                    csb = jnp.dot(cmg, sb_ref[c, gr, c2], preferred_element_type=F32)
                pc = slice((pp % 2) * LANES, (pp % 2 + 1) * LANES)
                colb = jnp.dot(catq, selb_ref[c, :, hd0 * CHUNK:(hd0 + 2) * CHUNK],
                               preferred_element_type=F32)
                scores, colf_bc, colb_bc = [], [], []
                for hh in range(2):
                    h = c * 2 * SSM_HEADS + hd0 + hh
                    hb = h + SSM_HEADS
                    colf_bc.append(jnp.broadcast_to(colq[:, h:h + 1], (CHUNK, CHUNK)))
                    colb_bc.append(colb[:, hh * CHUNK:(hh + 1) * CHUNK])
                    arg = jnp.where(lower, colf_bc[hh] - rowq[h:h + 1, :], colb_bc[hh] - rowq[hb:hb + 1, :])
                    diag = (cbdiag * (dtr[h:h + 1, :] + dtr[hb:hb + 1, :]) + dskip_ref[hd0 + hh]).astype(BF16)
                    scores.append(cboff * jnp.exp2(arg).astype(BF16) + eyeb * diag)
                xt = xs_ref[rows, cols]
                zero = jnp.zeros_like(xt)
                rhs = jnp.concatenate([jnp.where(left, xt, zero), jnp.where(left, zero, xt)], axis=0)
                y_diag = jnp.dot(jnp.concatenate(scores, axis=1), rhs, preferred_element_type=F32)
                y_off = (csf[:, pc] * jnp.exp2(jnp.where(left, colf_bc[0], colf_bc[1]))
                         + csb[:, pc] * jnp.exp2(jnp.where(left, colb_bc[0], colb_bc[1])))
                vg = (y_diag + y_off) * z_ref[rows, cols].astype(F32)
                vs_ref[rows, cols] = vg
                ssq = ssq + jnp.sum(vg * vg, axis=-1, keepdims=True)
                if g == 0 and pp == 1 and pending is not None:
                    pending()
                    pending = None
            gc = slice(g * GROUP_W, (g + 1) * GROUP_W)
            ys_ref[rows, gc] = (vs_ref[rows, gc] * lax.rsqrt(ssq * (1.0 / GROUP_W) + EPS)).astype(BF16)

        if c % 2 == 1:
            pending = lambda c=c: outproj(slice((c - 1) * CHUNK, (c + 1) * CHUNK))
    pending()


def _ssd_out(a_log, d_skip, selb, tril, xs, bm, cm, dt, z, sf, sb, y_attn, x2, w_out, final_norm_g):
    n = xs.shape[0]
    rows = Q * CHUNK
    row = lambda i: (i, 0)
    const = lambda i: (0, 0)
    return pl.pallas_call(
        _ssd_out_kernel,
        grid=(n // rows,),
        in_specs=[
            pl.BlockSpec((1, LANES), const),
            pl.BlockSpec(memory_space=pltpu.SMEM),
            pl.BlockSpec((Q, 2 * LANES, SSM_HEADS * CHUNK), lambda i: (0, 0, 0), pipeline_mode=pl.Buffered(1)),
            pl.BlockSpec((2, CHUNK, CHUNK), lambda i: (0, 0, 0)),
            pl.BlockSpec((rows, D_SSM), row),
            pl.BlockSpec((rows, SSM_GROUPS * D_STATE), row),
            pl.BlockSpec((rows, SSM_GROUPS * D_STATE), row),
            pl.BlockSpec((CHUNK, LANES), lambda i: (0, i)),
            pl.BlockSpec((rows, D_SSM), row),
            pl.BlockSpec((Q, SSM_GROUPS * D_STATE, GROUP_W), lambda i: (i, 0, 0)),
            pl.BlockSpec((Q, SSM_GROUPS * D_STATE, GROUP_W), lambda i: (i, 0, 0)),
            pl.BlockSpec((rows, D_ATTN), row),
            pl.BlockSpec((rows, D_MODEL), row),
            pl.BlockSpec((D_MIX, D_MODEL), const, pipeline_mode=pl.Buffered(1)),
            pl.BlockSpec((1, D_MODEL), const),
        ],
        out_specs=pl.BlockSpec((rows, D_MODEL), row),
        out_shape=jax.ShapeDtypeStruct((n, D_MODEL), F32),
        scratch_shapes=[pltpu.VMEM((rows, D_SSM), BF16), pltpu.VMEM((rows, D_SSM), F32)],
        compiler_params=pltpu.CompilerParams(dimension_semantics=("parallel",),
                                             vmem_limit_bytes=VMEM_LIMIT),
        name="ssd_out",
    )(a_log, d_skip, selb, tril, xs, bm, cm, dt, z, sf, sb, y_attn, x2, w_out, final_norm_g)


def _bucket_table_t():
    rel = jnp.arange(3 * BLK)[:, None] - BLK - jnp.arange(BLK)[None, :]
    half = NUM_BUCKETS // 2
    ret = (rel > 0).astype(jnp.int32) * half
    n = jnp.abs(rel)
    nf = jnp.maximum(n, 1).astype(F32)
    large = MAX_EXACT + (jnp.log(nf / MAX_EXACT) / math.log(MAX_DISTANCE / MAX_EXACT)
                         * (half - MAX_EXACT)).astype(jnp.int32)
    large = jnp.minimum(large, half - 1)
    return (ret + jnp.where(n < MAX_EXACT, n, large)).astype(jnp.int32)


def _attn_kernel(nblk, nstep, rb_ref, bucket_ref, sink_ref, q_ref, kp_ref, kc_ref, kn_ref,
                 vp_ref, vc_ref, vn_ref, ga_ref,
                 alog_ref, e_ref, tril_ref, xsf_ref, bf_ref, dtf_ref, xsb_ref, bb_ref, dtb_ref,
                 y_ref, sf_ref, sb_ref,
                 tab_ref, pe_ref, mk_ref, ot_ref, vs_ref, cf_ref, cb_ref):
    g = pl.program_id(0)

    @pl.when((g == 0) | (lax.rem(g + (nblk // ATTN_NB) - 2, nblk // ATTN_NB) == 0))
    def _():
        cf_ref[...] = jnp.zeros_like(cf_ref)
        cb_ref[...] = jnp.zeros_like(cb_ref)

    @pl.when(g == 0)
    def _init():
        pe_ref[...] = jnp.zeros_like(pe_ref)
        mk_ref[...] = jnp.zeros_like(mk_ref)
        ot_ref[...] = jnp.zeros_like(ot_ref)
        bk = bucket_ref[...]
        t = lax.broadcasted_iota(jnp.int32, (3 * BLK, BLK), 0)
        qq = lax.broadcasted_iota(jnp.int32, (3 * BLK, BLK), 1)
        in_window = jnp.abs(t - BLK - qq) <= WINDOW

        def per_head(h, carry):
            acc = jnp.zeros((3 * BLK, BLK), F32)
            for b in range(NUM_BUCKETS):
                acc = jnp.where(bk == b, rb_ref[b, h], acc)
            tab = jnp.where(in_window, acc * LOG2E, NEG_INF)
            tab_ref[0, h] = tab
            tab_ref[1, h] = jnp.where(t >= BLK, tab, NEG_INF)
            tab_ref[2, h] = jnp.where(t < 2 * BLK, tab, NEG_INF)
            return carry

        lax.fori_loop(0, ATTN_HEADS, per_head, 0)

    left = lax.broadcasted_iota(jnp.int32, (BLK, LANES), 1) < ATTN_HEAD_DIM
    left3 = lax.broadcasted_iota(jnp.int32, (3 * BLK, LANES), 1) < ATTN_HEAD_DIM
    s2 = sink_ref[...] * LOG2E

    def run(even, odd):
        first_blk = jnp.minimum(g, nstep - 1) * ATTN_NB
        kall = jnp.concatenate([kp_ref[...], kc_ref[...], kn_ref[...]], axis=0)
        vall = jnp.concatenate([vp_ref[...], vc_ref[...], vn_ref[...]], axis=0)
        one = jnp.ones((3 * BLK, LANES), BF16)

        def stage_a(j, p, var):
            kd = kall[j * BLK:(j + 3) * BLK, (p // 2) * LANES:(p // 2 + 1) * LANES]
            qt = q_ref[j * BLK:(j + 1) * BLK, p * LANES:(p + 1) * LANES]
            zero = jnp.zeros_like(qt)
            rhs_t = jnp.concatenate([jnp.where(left, qt, zero), jnp.where(left, zero, qt)], axis=0)
            lg = lax.dot_general(kd, rhs_t, (((1,), (1,)), ((), ())), preferred_element_type=F32)
            lg = lg + jnp.concatenate([tab_ref[var, 2 * p], tab_ref[var, 2 * p + 1]], axis=1)
            for kb in range(3):
                blk = lg[kb * BLK:(kb + 1) * BLK]
                mk = jnp.max(blk, axis=0, keepdims=True)
                pe_ref[even, j, p, kb * BLK:(kb + 1) * BLK, :] = jnp.exp2(blk - mk).astype(BF16)
                mk_ref[even, j, p, kb:kb + 1, :] = mk

        def stage_b(j, p):
            kv = p // 2
            vt = vall[j * BLK:(j + 3) * BLK, (kv // 2) * LANES:(kv // 2 + 1) * LANES]
            vmod = jnp.where(left3, vt, one) if kv % 2 == 0 else jnp.where(left3, one, vt)
            r0 = (kv % 2) * ATTN_HEAD_DIM
            d0 = ATTN_HEAD_DIM - r0
            mk = mk_ref[odd, j, p]
            sp = s2[p:p + 1, :]
            m = jnp.maximum(jnp.max(mk[0:3], axis=0, keepdims=True), sp)
            resc = jnp.exp2(mk[0:3] - m).astype(BF16)
            pes = jnp.concatenate([pe_ref[odd, j, p, kb * BLK:(kb + 1) * BLK, :] * resc[kb:kb + 1]
                                   for kb in range(3)], axis=0)
            o2 = lax.dot_general(vmod, pes, (((0,), (0,)), ((), ())), preferred_element_type=F32)
            o = o2[r0:r0 + ATTN_HEAD_DIM, :] / (o2[d0:d0 + 1, :] + jnp.exp2(sp - m))
            ot_ref[odd, j, 2 * p * ATTN_HEAD_DIM:(2 * p + 1) * ATTN_HEAD_DIM, :] = o[:, 0:BLK]
            ot_ref[odd, j, (2 * p + 1) * ATTN_HEAD_DIM:(2 * p + 2) * ATTN_HEAD_DIM, :] = o[:, BLK:2 * BLK]

        def stage_c(j, p, ssq):
            rows = slice(j * BLK, (j + 1) * BLK)
            cols = slice(p * LANES, (p + 1) * LANES)
            v = ot_ref[even, j, p * LANES:(p + 1) * LANES, :].T * ga_ref[rows, cols].astype(F32)
            vs_ref[rows, cols] = v
            return ssq + jnp.sum(v * v, axis=-1, keepdims=True)

        scan_units = _states_units(alog_ref, e_ref, tril_ref, xsf_ref, bf_ref, dtf_ref, xsb_ref, bb_ref,
                                   dtb_ref, sf_ref, sb_ref, cf_ref, cb_ref)
        per_block = len(scan_units) // ATTN_NB
        npair = ATTN_HEADS // 2

        for j in range(ATTN_NB):
            pos = lax.rem(first_blk + j, nblk)
            var = jnp.where(pos == 0, 1, jnp.where(pos == nblk - 1, 2, 0))
            rows = slice(j * BLK, (j + 1) * BLK)
            ssq = jnp.zeros((BLK, 1), F32)
            for p in range(ATTN_HEADS // 2):
                stage_a(j, p, var)
                stage_b(j, p)
                ssq = stage_c(j, p, ssq)
                if p % (npair // per_block) == 2:
                    scan_units.pop(0)()
            y_ref[rows, :] = (vs_ref[rows, :] * lax.rsqrt(ssq * (1.0 / D_ATTN) + EPS)).astype(BF16)

    @pl.when(lax.rem(g, 2) == 0)
    def _():
        run(0, 1)

    @pl.when(lax.rem(g, 2) == 1)
    def _():
        run(1, 0)


def _attn(rel_bias, bucket_t, sink_x, q, kd, v, ga, a_log, e, tril, xs, bm, dt, batch, seq):
    n = q.shape[0]
    nblk = seq // BLK
    assert nblk % ATTN_NB == 0 and ATTN_NB * BLK == Q * CHUNK
    nstep = batch * nblk // ATTN_NB
    per_seq = nblk // ATTN_NB
    nchunk = batch * seq // CHUNK

    def mirrored(lag):
        def index(g):
            s = step(lag)(g)
            return s - lax.rem(s, per_seq) + per_seq - 1 - lax.rem(s, per_seq)
        return index

    def step(lag):
        return lambda g: jnp.clip(g - lag, 0, nstep - 1)

    def cur(lag):
        return lambda g: (step(lag)(g), 0)

    def prev(lag):
        def index(g):
            b = step(lag)(g) * ATTN_NB
            return (b - jnp.where(lax.rem(b, nblk) > 0, 1, 0), 0)
        return index

    def nxt(lag):
        def index(g):
            b = step(lag)(g) * ATTN_NB + ATTN_NB - 1
            return (b + jnp.where(lax.rem(b, nblk) < nblk - 1, 1, 0), 0)
        return index

    const = lambda g: (0, 0)
    kw = 2 * KV_HEADS * ATTN_HEAD_DIM
    vw = KV_HEADS * ATTN_HEAD_DIM
    npair = ATTN_HEADS // 2
    rows = ATTN_NB * BLK
    st_shape = jax.ShapeDtypeStruct((nchunk, SSM_GROUPS * D_STATE, GROUP_W), BF16)
    st_block = (Q, SSM_GROUPS * D_STATE, GROUP_W)
    return pl.pallas_call(
        lambda *refs: _attn_kernel(nblk, nstep, *refs),
        grid=(nstep + 2,),
        in_specs=[
            pl.BlockSpec(memory_space=pltpu.SMEM),
            pl.BlockSpec((3 * BLK, BLK), const),
            pl.BlockSpec((npair, 2 * BLK), const),
            pl.BlockSpec((rows, D_ATTN), cur(0)),
            pl.BlockSpec((BLK, kw), prev(0)), pl.BlockSpec((rows, kw), cur(0)), pl.BlockSpec((BLK, kw), nxt(0)),
            pl.BlockSpec((BLK, vw), prev(1)), pl.BlockSpec((rows, vw), cur(1)), pl.BlockSpec((BLK, vw), nxt(1)),
            pl.BlockSpec((rows, D_ATTN), cur(2)),
            pl.BlockSpec((1, LANES), const),
            pl.BlockSpec((Q, 2, 2 * LANES, D_SSM), lambda g: (0, 0, 0, 0), pipeline_mode=pl.Buffered(1)),
            pl.BlockSpec((CHUNK, CHUNK), const),
            pl.BlockSpec((rows, D_SSM), cur(2)),
            pl.BlockSpec((rows, SSM_GROUPS * D_STATE), cur(2)),
            pl.BlockSpec((CHUNK, LANES), lambda g: (0, step(2)(g))),
            pl.BlockSpec((rows, D_SSM), lambda g: (mirrored(2)(g), 0)),
            pl.BlockSpec((rows, SSM_GROUPS * D_STATE), lambda g: (mirrored(2)(g), 0)),
            pl.BlockSpec((CHUNK, LANES), lambda g: (0, mirrored(2)(g))),
        ],
        out_specs=[pl.BlockSpec((rows, D_ATTN), cur(2)),
                   pl.BlockSpec(st_block, lambda g: (step(2)(g), 0, 0)),
                   pl.BlockSpec(st_block, lambda g: (mirrored(2)(g), 0, 0))],
        out_shape=[jax.ShapeDtypeStruct((n, D_ATTN), BF16), st_shape, st_shape],
        scratch_shapes=[pltpu.VMEM((3, ATTN_HEADS, 3 * BLK, BLK), F32),
                        pltpu.VMEM((2, ATTN_NB, npair, 3 * BLK, 2 * BLK), BF16),
                        pltpu.VMEM((2, ATTN_NB, npair, 8, 2 * BLK), F32),
                        pltpu.VMEM((2, ATTN_NB, D_ATTN, BLK), F32),
                        pltpu.VMEM((rows, D_ATTN), F32),
                        pltpu.VMEM((SSM_GROUPS * D_STATE, GROUP_W), F32),
                        pltpu.VMEM((SSM_GROUPS * D_STATE, GROUP_W), F32)],
        compiler_params=pltpu.CompilerParams(dimension_semantics=("arbitrary",),
                                             vmem_limit_bytes=VMEM_LIMIT),
        name="attn_scan",
    )(rel_bias, bucket_t, sink_x, q, kd, kd, kd, v, v, v, ga,
      a_log, e, tril, xs, bm, dt, xs, bm, dt)


def _arrange_w_in(w_in):
    wa = w_in[:, 0:WA_COLS].astype(BF16)
    wdt = jnp.pad(w_in[:, WA_COLS:WB_START], ((0, 0), (0, LANES - 2 * SSM_HEADS))).astype(BF16)
    wb = w_in[:, WB_START:WB_START + WB_COLS].astype(BF16)
    wk = wb[:, D_ATTN:OFF_V].reshape(D_MODEL, KV_HEADS, 1, ATTN_HEAD_DIM)
    wkd = jnp.broadcast_to(wk, (D_MODEL, KV_HEADS, 2, ATTN_HEAD_DIM)).reshape(D_MODEL, -1)
    return wa, wb, wkd, wdt


def _expand_matrices():
    e = np.zeros((Q, 2, 2 * LANES, D_SSM), np.float32)
    for c in range(Q):
        for d in range(2):
            for h in range(SSM_HEADS):
                r = (c * 2 + d) * SSM_HEADS + h
                e[c, d, r, h * SSM_HEAD_DIM:(h + 1) * SSM_HEAD_DIM] = 1.0
                e[c, d, LANES + r, h * SSM_HEAD_DIM:(h + 1) * SSM_HEAD_DIM] = 1.0
    return e


def _bcast_matrices():
    m = np.zeros((Q, 2 * LANES, SSM_HEADS * CHUNK), np.float32)
    for c in range(Q):
        for h in range(SSM_HEADS):
            r = (c * 2 + 1) * SSM_HEADS + h
            m[c, r, h * CHUNK:(h + 1) * CHUNK] = 1.0
            m[c, LANES + r, h * CHUNK:(h + 1) * CHUNK] = 1.0
    return m


def kernel(x, norm_in_g, w_in, conv_w, conv_b, dt_bias, a_log, d_skip, ssd_norm_g, rel_bias, sink,
           attn_norm_g, w_out, final_norm_g):
    batch, seq, _ = x.shape
    assert w_out.shape[0] == 1 and seq % (Q * CHUNK) == 0 and seq % TM == 0
    x2 = x.reshape(batch * seq, D_MODEL)

    weights = _arrange_w_in(w_in)
    e = jnp.asarray(_expand_matrices(), BF16)
    tril = jnp.asarray(np.tril(np.ones((CHUNK, CHUNK), np.float32)), BF16)
    a_log2 = jnp.tile(a_log.reshape(1, 2 * SSM_HEADS).astype(F32), (1, Q))
    dt_bias_x = jnp.pad(dt_bias.reshape(1, 2 * SSM_HEADS).astype(F32), ((0, 0), (0, LANES - 2 * SSM_HEADS)))
    gains = jnp.concatenate([ssd_norm_g, attn_norm_g]).astype(F32)
    w_out_g = (w_out[0] * gains[:, None]).astype(BF16)
    sink_x = jnp.repeat(sink.astype(F32), BLK).reshape(ATTN_HEADS // 2, 2 * BLK)

    z, xs, bm, cm, dt, q, kd, v, ga = _inproj(
        x2, norm_in_g.reshape(1, D_MODEL), weights, conv_w, conv_b.reshape(1, CONV_CH),
        dt_bias_x, seq)
    y_attn, sf, sb = _attn(rel_bias.astype(F32), _bucket_table_t(), sink_x,
                           q, kd, v, ga, a_log2, e, tril, xs, bm, dt, batch, seq)
    out = _ssd_out(a_log2, d_skip.astype(F32), jnp.asarray(_bcast_matrices(), BF16),
                   jnp.stack([tril, tril.T]), xs, bm, cm, dt, z, sf, sb, y_attn, x2, w_out_g,
                   final_norm_g.reshape(1, D_MODEL))
    return out.reshape(batch, seq, D_MODEL)
```
